```python
import jax
import jax.numpy as jnp
from jax import lax
import numpy as np

D_MODEL = 2048
BATCH = 2
SEQ = 8192
DEPTH = 2

GRID_W = 64
CTX_LEN = 256
HEAD_DIM = 128
ROPE_BASE = 10000.0
EPS = 1e-6
Q_BLOCK = 128
NEG = -1e30

A_HEADS = 8
A_KV_HEADS = 2
B_HEADS = 8
B_Q_RANK = 512
B_KV_RANK = 512
B_NOPE = 128
B_ROPE = 64
B_V = 128
C_HEADS = 8
C_KV_HEADS = 2
C_WINDOW = 128
D_HEADS = 8
NA_KH = 8
NA_KW = 16

A_IN = (A_HEADS + 2 * A_KV_HEADS) * HEAD_DIM
B_IN = B_Q_RANK + B_KV_RANK + B_ROPE
AB_IN = A_IN + B_IN
AB_MIX = A_HEADS * HEAD_DIM + B_HEADS * B_V
C_IN = (C_HEADS + 2 * C_KV_HEADS) * HEAD_DIM
D_IN = 3 * D_HEADS * HEAD_DIM
CD_IN = C_IN + D_IN
CD_MIX = (C_HEADS + D_HEADS) * HEAD_DIM
D_FF = -(-8 * D_MODEL // (3 * 256)) * 256
N_AB = (DEPTH + 1) // 2
N_CD = DEPTH // 2

kernel_name = 'hybrid_dit_prefix_block'


def rmsnorm(x, g):
    xf = x.astype(jnp.float32)
    xf = xf * lax.rsqrt(jnp.mean(xf * xf, axis=-1, keepdims=True) + EPS)
    return (xf * g.astype(jnp.float32)).astype(x.dtype)


def modulate(n, shift, scale):
    return n * (1 + scale) + shift


def swiglu(u, wg, wu, wd):
    return (jax.nn.silu(u @ wg) * (u @ wu)) @ wd


def rope_2d(n_tok, dim, dtype):
    t = jnp.arange(n_tok, dtype=jnp.int32)
    row = (t // GRID_W).astype(jnp.float32)
    col = (t % GRID_W).astype(jnp.float32)
    half = dim // 2
    inv_freq = ROPE_BASE ** (-jnp.arange(0, half, 2, dtype=jnp.float32) / half)
    ar = row[:, None] * inv_freq[None, :]
    ac = col[:, None] * inv_freq[None, :]
    ang = jnp.concatenate([ar, ar, ac, ac], axis=-1)
    return jnp.cos(ang).astype(dtype), jnp.sin(ang).astype(dtype)


def apply_rope(x, cos, sin):
    x1, x2, x3, x4 = jnp.split(x, 4, axis=-1)
    rot = jnp.concatenate([-x2, x1, -x4, x3], axis=-1)
    return x * cos[:, None, :] + rot * sin[:, None, :]


def softmax_attend(q, k, v, scale):
    s = jnp.einsum('bqgrd,bkgd->bgrqk', q, k, preferred_element_type=jnp.float32) * scale
    p = jax.nn.softmax(s, axis=-1).astype(v.dtype)
    return jnp.einsum('bgrqk,bkgd->bqgrd', p, v)


def dense_blocked(q, k, v, scale):
    B, L = q.shape[:2]
    nb = L // Q_BLOCK
    qb = jnp.moveaxis(q.reshape((B, nb, Q_BLOCK) + q.shape[2:]), 1, 0)
    ob = lax.map(lambda qi: softmax_attend(qi, k, v, scale), qb)
    return jnp.moveaxis(ob, 0, 1).reshape((B, L) + ob.shape[3:])


def qkv_heads(p, n_q, n_kv, qn, kn):
    B, L = p.shape[:2]
    q, k, v = jnp.split(p, [n_q * HEAD_DIM, (n_q + n_kv) * HEAD_DIM], axis=-1)
    q = rmsnorm(q.reshape(B, L, n_q, HEAD_DIM), qn)
    k = rmsnorm(k.reshape(B, L, n_kv, HEAD_DIM), kn)
    return q, k, v.reshape(B, L, n_kv, HEAD_DIM)


def mixer_a(p_lat, p_ctx, qn, kn, cos, sin, need_ctx):
    B, L = p_lat.shape[:2]
    n_ctx = p_ctx.shape[1]
    R = A_HEADS // A_KV_HEADS
    scale = HEAD_DIM ** -0.5
    qc, kc, vc = qkv_heads(p_ctx, A_HEADS, A_KV_HEADS, qn, kn)
    ql, kl, vl = qkv_heads(p_lat, A_HEADS, A_KV_HEADS, qn, kn)
    ql = apply_rope(ql, cos, sin)
    kl = apply_rope(kl, cos, sin)
    k_all = jnp.concatenate([kc, kl], axis=1)
    v_all = jnp.concatenate([vc, vl], axis=1)
    o_lat = dense_blocked(ql.reshape(B, L, A_KV_HEADS, R, HEAD_DIM), k_all, v_all, scale).reshape(B, L, A_HEADS * HEAD_DIM)
    o_ctx = None
    if need_ctx:
        o_ctx = softmax_attend(qc.reshape(B, n_ctx, A_KV_HEADS, R, HEAD_DIM), kc, vc, scale).reshape(B, n_ctx, A_HEADS * HEAD_DIM)
    return o_lat, o_ctx


def mla_heads(p, qa_n, kva_n, w_uq, w_ukv, rope):
    B, L = p.shape[:2]
    cq, ckv, kr = jnp.split(p, [B_Q_RANK, B_Q_RANK + B_KV_RANK], axis=-1)
    q = (rmsnorm(cq, qa_n) @ w_uq).reshape(B, L, B_HEADS, B_NOPE + B_ROPE)
    kv = (rmsnorm(ckv, kva_n) @ w_ukv).reshape(B, L, B_HEADS, B_NOPE + B_V)
    q_nope, q_rope = jnp.split(q, [B_NOPE], axis=-1)
    k_nope, v = jnp.split(kv, [B_NOPE], axis=-1)
    kr = kr[:, :, None, :]
    if rope is not None:
        q_rope = apply_rope(q_rope, rope[0], rope[1])
        kr = apply_rope(kr, rope[0], rope[1])
    q = jnp.concatenate([q_nope, q_rope], axis=-1)[:, :, :, None, :]
    k = jnp.concatenate([k_nope, jnp.broadcast_to(kr, (B, L, B_HEADS, B_ROPE))], axis=-1)
    return q, k, v


def mixer_b(p_lat, p_ctx, qa_n, kva_n, w_uq, w_ukv, cos, sin, need_ctx):
    B, L = p_lat.shape[:2]
    n_ctx = p_ctx.shape[1]
    scale = (B_NOPE + B_ROPE) ** -0.5
    qc, kc, vc = mla_heads(p_ctx, qa_n, kva_n, w_uq, w_ukv, None)
    ql, kl, vl = mla_heads(p_lat, qa_n, kva_n, w_uq, w_ukv, (cos, sin))
    k_all = jnp.concatenate([kc, kl], axis=1)
    v_all = jnp.concatenate([vc, vl], axis=1)
    o_lat = dense_blocked(ql, k_all, v_all, scale).reshape(B, L, B_HEADS * B_V)
    o_ctx = None
    if need_ctx:
        o_ctx = softmax_attend(qc, kc, vc, scale).reshape(B, n_ctx, B_HEADS * B_V)
    return o_lat, o_ctx


def mixer_c(p_lat, p_ctx, qn, kn, sink, cos, sin, need_ctx):
    B, L = p_lat.shape[:2]
    n_ctx = p_ctx.shape[1]
    G, R = C_KV_HEADS, C_HEADS // C_KV_HEADS
    nb = L // Q_BLOCK
    scale = HEAD_DIM ** -0.5
    sink_gr = sink.reshape(G, R).astype(jnp.float32)
    qc, kc, vc = qkv_heads(p_ctx, C_HEADS, C_KV_HEADS, qn, kn)
    ql, kl, vl = qkv_heads(p_lat, C_HEADS, C_KV_HEADS, qn, kn)
    ql = apply_rope(ql, cos, sin)
    kl = apply_rope(kl, cos, sin)
    qb = ql.reshape(B, nb, Q_BLOCK, G, R, HEAD_DIM)

    def band(t):
        tp = jnp.pad(t, ((0, 0), (Q_BLOCK, Q_BLOCK), (0, 0), (0, 0)))
        tb = tp.reshape(B, nb + 2, Q_BLOCK, G, HEAD_DIM)
        return jnp.concatenate([tb[:, :-2], tb[:, 1:-1], tb[:, 2:]], axis=2)

    kb, vb = band(kl), band(vl)
    n_idx = jnp.arange(nb)[:, None, None]
    qpos = n_idx * Q_BLOCK + jnp.arange(Q_BLOCK)[None, :, None]
    kpos = (n_idx - 1) * Q_BLOCK + jnp.arange(3 * Q_BLOCK)[None, None, :]
    valid = (jnp.abs(kpos - qpos) <= C_WINDOW) & (kpos >= 0) & (kpos < L)
    s_win = jnp.einsum('bnqgrd,bnkgd->bngrqk', qb, kb, preferred_element_type=jnp.float32) * scale
    s_win = jnp.where(valid[None, :, None, None], s_win, NEG)
    s_ctx = jnp.einsum('bnqgrd,bkgd->bngrqk', qb, kc, preferred_element_type=jnp.float32) * scale
    s_sink = jnp.broadcast_to(sink_gr[None, None, :, :, None, None], s_ctx.shape[:-1] + (1,))
    p = jax.nn.softmax(jnp.concatenate([s_ctx, s_win, s_sink], axis=-1), axis=-1).astype(vb.dtype)
    o = (jnp.einsum('bngrqk,bkgd->bnqgrd', p[..., :n_ctx], vc)
         + jnp.einsum('bngrqk,bnkgd->bnqgrd', p[..., n_ctx:n_ctx + 3 * Q_BLOCK], vb))
    o_lat = o.reshape(B, L, C_HEADS * HEAD_DIM)
    o_ctx = None
    if need_ctx:
        qcg = qc.reshape(B, n_ctx, G, R, HEAD_DIM)
        s = jnp.einsum('bqgrd,bkgd->bgrqk', qcg, kc, preferred_element_type=jnp.float32) * scale
        s_sink_c = jnp.broadcast_to(sink_gr[None, :, :, None, None], s.shape[:-1] + (1,))
        pc = jax.nn.softmax(jnp.concatenate([s, s_sink_c], axis=-1), axis=-1)[..., :n_ctx].astype(vc.dtype)
        o_ctx = jnp.einsum('bgrqk,bkgd->bqgrd', pc, vc).reshape(B, n_ctx, C_HEADS * HEAD_DIM)
    return o_lat, o_ctx


def mixer_d(p_lat, p_ctx, qn, kn, rpb, need_ctx):
    B, L = p_lat.shape[:2]
    n_ctx = p_ctx.shape[1]
    rows = L // GRID_W
    kh = min(NA_KH, rows)
    scale = HEAD_DIM ** -0.5
    qc, kc, vc = qkv_heads(p_ctx, D_HEADS, D_HEADS, qn, kn)
    ql, kl, vl = qkv_heads(p_lat, D_HEADS, D_HEADS, qn, kn)
    grid = (B, rows, GRID_W, D_HEADS, HEAD_DIM)
    qg, kg, vg = ql.reshape(grid), kl.reshape(grid), vl.reshape(grid)
    cols = jnp.arange(GRID_W)
    col_start = jnp.clip(cols - NA_KW // 2, 0, GRID_W - NA_KW)
    col_idx = col_start[:, None] + jnp.arange(NA_KW)[None, :]
    bias_cols = rpb[:, :, col_idx - cols[:, None] + NA_KW - 1].astype(jnp.float32)

    def row_block(args):
        qr, r = args
        rs = jnp.clip(r - kh // 2, 0, rows - kh)
        kq = lax.dynamic_slice_in_dim(kg, rs, kh, axis=1)[:, :, col_idx]
        vq = lax.dynamic_slice_in_dim(vg, rs, kh, axis=1)[:, :, col_idx]
        s_nb = jnp.einsum('bchd,bicjhd->bhcij', qr, kq, preferred_element_type=jnp.float32) * scale
        di = rs + jnp.arange(kh) - r
        s_nb = s_nb + jnp.transpose(bias_cols[:, di + NA_KH - 1], (0, 2, 1, 3))[None]
        s_ctx = jnp.einsum('bchd,bkhd->bhck', qr, kc, preferred_element_type=jnp.float32) * scale
        s = jnp.concatenate([s_ctx, s_nb.reshape(B, D_HEADS, GRID_W, kh * NA_KW)], axis=-1)
        p = jax.nn.softmax(s, axis=-1).astype(vq.dtype)
        p_nb = p[..., n_ctx:].reshape(B, D_HEADS, GRID_W, kh, NA_KW)
        return (jnp.einsum('bhck,bkhd->bchd', p[..., :n_ctx], vc)
                + jnp.einsum('bhcij,bicjhd->bchd', p_nb, vq))

    o = lax.map(row_block, (jnp.moveaxis(qg, 1, 0), jnp.arange(rows)))
    o_lat = jnp.moveaxis(o, 0, 1).reshape(B, L, D_HEADS * HEAD_DIM)
    o_ctx = None
    if need_ctx:
        o_ctx = softmax_attend(qc[:, :, :, None, :], kc, vc, scale).reshape(B, n_ctx, D_HEADS * HEAD_DIM)
    return o_lat, o_ctx


def setup_inputs(seed: int = 0) -> dict:
    key = jax.random.key(seed)
    ks = iter(jax.random.split(key, 32))
    f32 = jnp.float32

    def nrm(shape, s=1.0):
        return jax.random.normal(next(ks), shape, f32) * s

    def gain(shape):
        return 1.0 + nrm(shape, 0.02)

    D = D_MODEL
    return {
        'x': nrm((BATCH, SEQ, D)),
        'c': nrm((BATCH, D)),
        'ctx': nrm((BATCH, CTX_LEN, D)),
        'c_ctx': nrm((D,)),
        'w_mod': nrm((DEPTH, D, 6 * D), D ** -0.5),
        'b_mod': nrm((DEPTH, 6 * D), 0.02),
        'g_mix_pre': gain((DEPTH, D)),
        'g_mix_post': gain((DEPTH, D)),
        'g_ffn_pre': gain((DEPTH, D)),
        'g_ffn_post': gain((DEPTH, D)),
        'w_gate': nrm((DEPTH, D, D_FF), D ** -0.5),
        'w_up': nrm((DEPTH, D, D_FF), D ** -0.5),
        'w_down': nrm((DEPTH, D_FF, D), D_FF ** -0.5),
        'ab_w_in': nrm((N_AB, D, AB_IN), D ** -0.5),
        'ab_w_out': nrm((N_AB, AB_MIX, D), AB_MIX ** -0.5),
        'a_q_norm': gain((N_AB, HEAD_DIM)),
        'a_k_norm': gain((N_AB, HEAD_DIM)),
        'b_q_norm': gain((N_AB, B_Q_RANK)),
        'b_kv_norm': gain((N_AB, B_KV_RANK)),
        'b_w_uq': nrm((N_AB, B_Q_RANK, B_HEADS * (B_NOPE + B_ROPE)), B_Q_RANK ** -0.5),
        'b_w_ukv': nrm((N_AB, B_KV_RANK, B_HEADS * (B_NOPE + B_V)), B_KV_RANK ** -0.5),
        'cd_w_in': nrm((N_CD, D, CD_IN), D ** -0.5),
        'cd_w_out': nrm((N_CD, CD_MIX, D), CD_MIX ** -0.5),
        'c_q_norm': gain((N_CD, HEAD_DIM)),
        'c_k_norm': gain((N_CD, HEAD_DIM)),
        'c_sink': nrm((N_CD, C_HEADS), 0.5),
        'd_q_norm': gain((N_CD, HEAD_DIM)),
        'd_k_norm': gain((N_CD, HEAD_DIM)),
        'd_rpb': nrm((N_CD, D_HEADS, 2 * NA_KH - 1, 2 * NA_KW - 1), 0.1),
    }


def reference(x, c, ctx, c_ctx, w_mod, b_mod, g_mix_pre, g_mix_post, g_ffn_pre, g_ffn_post,
              w_gate, w_up, w_down, ab_w_in, ab_w_out, a_q_norm, a_k_norm, b_q_norm, b_kv_norm,
              b_w_uq, b_w_ukv, cd_w_in, cd_w_out, c_q_norm, c_k_norm, c_sink, d_q_norm, d_k_norm, d_rpb):
    L = x.shape[1]
    cos_h, sin_h = rope_2d(L, HEAD_DIM, x.dtype)
    cos_r, sin_r = rope_2d(L, B_ROPE, x.dtype)
    h_lat, h_ctx = x, ctx
    for i in range(DEPTH):
        last = i == DEPTH - 1
        need_ctx = not last
        mod_lat = (jax.nn.silu(c) @ w_mod[i] + b_mod[i])[:, None, :]
        mod_ctx = (jax.nn.silu(c_ctx) @ w_mod[i] + b_mod[i])[None, None, :]
        sh1_l, sc1_l, gt1_l, sh2_l, sc2_l, gt2_l = jnp.split(mod_lat, 6, axis=-1)
        sh1_c, sc1_c, gt1_c, sh2_c, sc2_c, gt2_c = jnp.split(mod_ctx, 6, axis=-1)

        u_lat = modulate(rmsnorm(h_lat, g_mix_pre[i]), sh1_l, sc1_l)
        u_ctx = modulate(rmsnorm(h_ctx, g_mix_pre[i]), sh1_c, sc1_c)
        if i % 2 == 0:
            j = i // 2
            p_lat = u_lat @ ab_w_in[j]
            p_ctx = u_ctx @ ab_w_in[j]
            oa_l, oa_c = mixer_a(p_lat[..., :A_IN], p_ctx[..., :A_IN], a_q_norm[j], a_k_norm[j], cos_h, sin_h, need_ctx)
            ob_l, ob_c = mixer_b(p_lat[..., A_IN:], p_ctx[..., A_IN:], b_q_norm[j], b_kv_norm[j],
                                 b_w_uq[j], b_w_ukv[j], cos_r, sin_r, need_ctx)
            o_lat = jnp.concatenate([oa_l, ob_l], axis=-1) @ ab_w_out[j]
            if need_ctx:
                o_ctx = jnp.concatenate([oa_c, ob_c], axis=-1) @ ab_w_out[j]
        else:
            j = i // 2
            p_lat = u_lat @ cd_w_in[j]
            p_ctx = u_ctx @ cd_w_in[j]
            oc_l, oc_c = mixer_c(p_lat[..., :C_IN], p_ctx[..., :C_IN], c_q_norm[j], c_k_norm[j], c_sink[j],
                                 cos_h, sin_h, need_ctx)
            od_l, od_c = mixer_d(p_lat[..., C_IN:], p_ctx[..., C_IN:], d_q_norm[j], d_k_norm[j], d_rpb[j], need_ctx)
            o_lat = jnp.concatenate([oc_l, od_l], axis=-1) @ cd_w_out[j]
            if need_ctx:
                o_ctx = jnp.concatenate([oc_c, od_c], axis=-1) @ cd_w_out[j]
        h_lat = h_lat + gt1_l * rmsnorm(o_lat, g_mix_post[i])
        if need_ctx:
            h_ctx = h_ctx + gt1_c * rmsnorm(o_ctx, g_mix_post[i])

        f_lat = swiglu(modulate(rmsnorm(h_lat, g_ffn_pre[i]), sh2_l, sc2_l), w_gate[i], w_up[i], w_down[i])
        h_lat = h_lat + gt2_l * rmsnorm(f_lat, g_ffn_post[i])
        if need_ctx:
            f_ctx = swiglu(modulate(rmsnorm(h_ctx, g_ffn_pre[i]), sh2_c, sc2_c), w_gate[i], w_up[i], w_down[i])
            h_ctx = h_ctx + gt2_c * rmsnorm(f_ctx, g_ffn_post[i])
    return h_lat
```

```python
import functools

import jax
import jax.numpy as jnp
from jax import lax
from jax.experimental import pallas as pl
from jax.experimental.pallas import tpu as pltpu

F32 = jnp.float32
BF16 = jnp.bfloat16

D_MODEL = 2048
GRID_W = 64
HEAD_DIM = 128
ROPE_BASE = 10000.0
EPS = 1e-6
NEG = -1e30

A_HEADS, A_KV_HEADS = 8, 2
B_HEADS, B_Q_RANK, B_KV_RANK, B_NOPE, B_ROPE, B_V = 8, 512, 512, 128, 64, 128
C_HEADS, C_KV_HEADS, C_WINDOW = 8, 2, 128
D_HEADS, NA_KH, NA_KW = 8, 8, 16
A_IN = (A_HEADS + 2 * A_KV_HEADS) * HEAD_DIM
C_IN = (C_HEADS + 2 * C_KV_HEADS) * HEAD_DIM
D_FF = 5632

LANES = 128
MXU_COLS = 256
B_QK_PAD = 256
BAND_TILE = 256
VMEM_LIMIT = 56 * 1024 * 1024
ROW_CHUNK = 128


def _cparams(sem):
    return pltpu.CompilerParams(dimension_semantics=sem, vmem_limit_bytes=VMEM_LIMIT)


def _resident(shape):
    nd = len(shape)
    return pl.BlockSpec(shape, lambda *_: (0,) * nd, pipeline_mode=pl.Buffered(1))


def _silu(x):
    return x * (1.0 / (1.0 + jnp.exp(-x)))


def _rms(x, g):
    ms = jnp.mean(x * x, axis=-1, keepdims=True)
    return x * lax.rsqrt(ms + EPS) * g


def _dot(a, b):
    return jnp.dot(a, b, preferred_element_type=F32)


def _dot_nt(a, b):
    return lax.dot_general(a, b, (((1,), (1,)), ((), ())), preferred_element_type=F32)


def _mod_kernel(c_ref, w_ref, b_ref, o_ref):
    s = _silu(c_ref[...])
    s_hi = s.astype(BF16)
    s_lo = (s - s_hi.astype(F32)).astype(BF16)
    w = w_ref[...]
    w_hi = w.astype(BF16)
    w_lo = (w - w_hi.astype(F32)).astype(BF16)
    o_ref[...] = _dot(s_hi, w_hi) + (_dot(s_lo, w_hi) + _dot(s_hi, w_lo)) + b_ref[...]


def _modulation(cvec, w_mod, b_mod):
    depth, d, n = w_mod.shape
    bn = 1024
    return pl.pallas_call(
        _mod_kernel,
        grid=(depth, n // bn),
        in_specs=[
            pl.BlockSpec((8, d), lambda i, j: (0, 0)),
            pl.BlockSpec((None, d, bn), lambda i, j: (i, 0, j)),
            pl.BlockSpec((None, 1, bn), lambda i, j: (i, 0, j)),
        ],
        out_specs=pl.BlockSpec((None, 8, bn), lambda i, j: (i, 0, j)),
        out_shape=jax.ShapeDtypeStruct((depth, 8, n), F32),
        compiler_params=_cparams(("parallel", "parallel")),
        name="modulation",
    )(cvec, w_mod, b_mod.reshape(depth, 1, n))


def _norm_mod_to(h_ref, g_ref, mod_ref, shift_row, scale_row, u_ref):
    bm = h_ref.shape[0]
    chunk = min(ROW_CHUNK, bm)
    g = g_ref[...]
    sh = mod_ref[shift_row:shift_row + 1, :]
    sc = mod_ref[scale_row:scale_row + 1, :]

    def body(c, carry):
        r = pl.multiple_of(c * chunk, chunk)
        n = _rms(h_ref[pl.ds(r, chunk), :], g)
        u_ref[pl.ds(r, chunk), :] = (n * (1.0 + sc) + sh).astype(BF16)
        return carry

    lax.fori_loop(0, bm // chunk, body, 0)


def _post_residual_to(acc_ref, h_ref, g_ref, mod_ref, gate_row, o_ref):
    bm = h_ref.shape[0]
    chunk = min(ROW_CHUNK, bm)
    g = g_ref[...]
    gate = mod_ref[gate_row:gate_row + 1, :]

    def body(c, carry):
        r = pl.multiple_of(c * chunk, chunk)
        o_ref[pl.ds(r, chunk), :] = h_ref[pl.ds(r, chunk), :] + gate * _rms(acc_ref[pl.ds(r, chunk), :], g)
        return carry

    lax.fori_loop(0, bm // chunk, body, 0)


def _rope(x, cos, sin, quarter):
    lane = lax.broadcasted_iota(jnp.int32, x.shape, 1)
    first = (lane % (2 * quarter)) < quarter
    rot = jnp.where(first, -pltpu.roll(x, LANES - quarter, 1), pltpu.roll(x, quarter, 1))
    return x * cos + rot * sin


def _emit_heads(u_ref, w_ref, col0, n_heads, gain, scale, rope, o_ref, ocol0):
    h = 0
    while h < n_heads:
        nh = min(MXU_COLS // HEAD_DIM, n_heads - h)
        c = col0 + h * HEAD_DIM
        p = _dot(u_ref[...], w_ref[:, c:c + nh * HEAD_DIM])
        for t in range(nh):
            x = p[:, t * HEAD_DIM:(t + 1) * HEAD_DIM]
            if gain is not None:
                x = _rms(x, gain)
                if rope is not None:
                    x = _rope(x, rope[0], rope[1], HEAD_DIM // 4)
                if scale != 1.0:
                    x = x * scale
            oc = ocol0 + (h + t) * HEAD_DIM
            o_ref[:, oc:oc + HEAD_DIM] = x.astype(BF16)
        h += nh


def _inproj_ab_kernel(use_rope, h_ref, g_ref, mod_ref, w_ref, aqn_ref, akn_ref, bqn_ref, bkvn_ref,
                      wuq_ref, wuk_ref, wuv_ref, cosh_ref, sinh_ref, cosr_ref, sinr_ref,
                      qa_ref, ka_ref, va_ref, qb_ref, kb_ref, vb_ref, u_ref, c_ref):
    _norm_mod_to(h_ref, g_ref, mod_ref, 0, 1, u_ref)
    rope_h = (cosh_ref[...], sinh_ref[...]) if use_rope else None
    a_scale = HEAD_DIM ** -0.5
    _emit_heads(u_ref, w_ref, 0, A_HEADS, aqn_ref[...], a_scale, rope_h, qa_ref, 0)
    _emit_heads(u_ref, w_ref, A_HEADS * HEAD_DIM, A_KV_HEADS, akn_ref[...], 1.0, rope_h, ka_ref, 0)
    _emit_heads(u_ref, w_ref, (A_HEADS + A_KV_HEADS) * HEAD_DIM, A_KV_HEADS, None, 1.0, None, va_ref, 0)

    b_scale = (B_NOPE + B_ROPE) ** -0.5
    cq = _dot(u_ref[...], w_ref[:, A_IN:A_IN + B_Q_RANK])
    c_ref[...] = _rms(cq, bqn_ref[...]).astype(BF16)
    for hd in range(B_HEADS):
        q = _dot(c_ref[...], wuq_ref[:, hd * B_QK_PAD:(hd + 1) * B_QK_PAD])
        q_nope = q[:, :B_NOPE]
        q_rope = q[:, B_NOPE:]
        if use_rope:
            q_rope = _rope(q_rope, cosr_ref[...], sinr_ref[...], B_ROPE // 4)
        qb_ref[:, hd * B_QK_PAD:hd * B_QK_PAD + B_NOPE] = (q_nope * b_scale).astype(BF16)
        qb_ref[:, hd * B_QK_PAD + B_NOPE:(hd + 1) * B_QK_PAD] = (q_rope * b_scale).astype(BF16)
    kr = _dot(u_ref[...], w_ref[:, A_IN + B_Q_RANK + B_KV_RANK:A_IN + B_Q_RANK + B_KV_RANK + LANES])
    if use_rope:
        kr = _rope(kr, cosr_ref[...], sinr_ref[...], B_ROPE // 4)
    kr = kr.astype(BF16)
    ckv = _dot(u_ref[...], w_ref[:, A_IN + B_Q_RANK:A_IN + B_Q_RANK + B_KV_RANK])
    c_ref[...] = _rms(ckv, bkvn_ref[...]).astype(BF16)
    for pair in range(B_HEADS * B_NOPE // MXU_COLS):
        kn = _dot(c_ref[...], wuk_ref[:, pair * MXU_COLS:(pair + 1) * MXU_COLS])
        for t in range(MXU_COLS // B_NOPE):
            hd = pair * (MXU_COLS // B_NOPE) + t
            kb_ref[:, hd * B_QK_PAD:hd * B_QK_PAD + B_NOPE] = kn[:, t * B_NOPE:(t + 1) * B_NOPE].astype(BF16)
            kb_ref[:, hd * B_QK_PAD + B_NOPE:(hd + 1) * B_QK_PAD] = kr
        vb_ref[:, pair * MXU_COLS:(pair + 1) * MXU_COLS] = _dot(
            c_ref[...], wuv_ref[:, pair * MXU_COLS:(pair + 1) * MXU_COLS]).astype(BF16)


def _inproj_ab(h, g_pre, mod, w, aqn, akn, bqn, bkvn, wuq, wuk, wuv, rope_tabs, rows_per_batch, bm):
    rows, d = h.shape
    use_rope = rope_tabs is not None
    tiles_per_batch = rows_per_batch // bm
    nb = mod.shape[0]

    def row_map(i):
        return (i, 0)

    def mod_map(i):
        return (i // tiles_per_batch if nb > 1 else 0, 0, 0)

    def tab_map(i):
        return (i % tiles_per_batch, 0)

    if use_rope:
        tabs = list(rope_tabs)
    else:
        tabs = [jnp.zeros((bm, LANES), F32)] * 4

        def tab_map(i):
            return (0, 0)

    outs = [
        (A_HEADS * HEAD_DIM), (A_KV_HEADS * HEAD_DIM), (A_KV_HEADS * HEAD_DIM),
        (B_HEADS * B_QK_PAD), (B_HEADS * B_QK_PAD), (B_HEADS * B_V),
    ]
    return pl.pallas_call(
        functools.partial(_inproj_ab_kernel, use_rope),
        grid=(rows // bm,),
        in_specs=[
            pl.BlockSpec((bm, d), row_map),
            _resident((1, d)),
            pl.BlockSpec((None, 8, d), mod_map),
            _resident(w.shape),
            _resident((1, HEAD_DIM)), _resident((1, HEAD_DIM)),
            _resident((1, B_Q_RANK)), _resident((1, B_KV_RANK)),
            _resident(wuq.shape), _resident(wuk.shape), _resident(wuv.shape),
        ] + [pl.BlockSpec((bm, LANES), tab_map)] * 4,
        out_specs=[pl.BlockSpec((bm, n), row_map) for n in outs],
        out_shape=[jax.ShapeDtypeStruct((rows, n), BF16) for n in outs],
        scratch_shapes=[pltpu.VMEM((bm, d), BF16), pltpu.VMEM((bm, B_Q_RANK), BF16)],
        compiler_params=_cparams(("parallel",)),
        name="inproj_ab",
    )(h, g_pre, mod, w, aqn, akn, bqn, bkvn, wuq, wuk, wuv, *tabs)


def _inproj_cd_kernel(use_rope, h_ref, g_ref, mod_ref, w_ref, cqn_ref, ckn_ref, dqn_ref, dkn_ref,
                      cosh_ref, sinh_ref, q_ref, k_ref, v_ref, u_ref):
    _norm_mod_to(h_ref, g_ref, mod_ref, 0, 1, u_ref)
    rope_h = (cosh_ref[...], sinh_ref[...]) if use_rope else None
    scale = HEAD_DIM ** -0.5
    kv_c = C_KV_HEADS * HEAD_DIM
    _emit_heads(u_ref, w_ref, 0, C_HEADS, cqn_ref[...], scale, rope_h, q_ref, 0)
    _emit_heads(u_ref, w_ref, C_HEADS * HEAD_DIM, C_KV_HEADS, ckn_ref[...], 1.0, rope_h, k_ref, 0)
    _emit_heads(u_ref, w_ref, C_HEADS * HEAD_DIM + kv_c, C_KV_HEADS, None, 1.0, None, v_ref, 0)
    _emit_heads(u_ref, w_ref, C_IN, D_HEADS, dqn_ref[...], scale, None, q_ref, C_HEADS * HEAD_DIM)
    _emit_heads(u_ref, w_ref, C_IN + D_HEADS * HEAD_DIM, D_HEADS, dkn_ref[...], 1.0, None, k_ref, kv_c)
    _emit_heads(u_ref, w_ref, C_IN + 2 * D_HEADS * HEAD_DIM, D_HEADS, None, 1.0, None, v_ref, kv_c)


def _inproj_cd(h, g_pre, mod, w, cqn, ckn, dqn, dkn, rope_tabs, rows_per_batch, bm):
    rows, d = h.shape
    use_rope = rope_tabs is not None
    tiles_per_batch = rows_per_batch // bm
    nb = mod.shape[0]

    def row_map(i):
        return (i, 0)

    def mod_map(i):
        return (i // tiles_per_batch if nb > 1 else 0, 0, 0)

    if use_rope:
        tabs = list(rope_tabs)

        def tab_map(i):
            return (i % tiles_per_batch, 0)
    else:
        tabs = [jnp.zeros((bm, LANES), F32)] * 2

        def tab_map(i):
            return (0, 0)

    outs = [(C_HEADS + D_HEADS) * HEAD_DIM, (C_KV_HEADS + D_HEADS) * HEAD_DIM, (C_KV_HEADS + D_HEADS) * HEAD_DIM]
    return pl.pallas_call(
        functools.partial(_inproj_cd_kernel, use_rope),
        grid=(rows // bm,),
        in_specs=[
            pl.BlockSpec((bm, d), row_map),
            _resident((1, d)),
            pl.BlockSpec((None, 8, d), mod_map),
            _resident(w.shape),
            _resident((1, HEAD_DIM)), _resident((1, HEAD_DIM)), _resident((1, HEAD_DIM)), _resident((1, HEAD_DIM)),
        ] + [pl.BlockSpec((bm, LANES), tab_map)] * 2,
        out_specs=[pl.BlockSpec((bm, n), row_map) for n in outs],
        out_shape=[jax.ShapeDtypeStruct((rows, n), BF16) for n in outs],
        scratch_shapes=[pltpu.VMEM((bm, d), BF16)],
        compiler_params=_cparams(("parallel",)),
        name="inproj_cd",
    )(h, g_pre, mod, w, cqn, ckn, dqn, dkn, *tabs)


def _flash_kernel(rep, dq, has_ctx, *refs):
    if has_ctx:
        q_ref, k_ref, v_ref, kc_ref, vc_ref, o_ref, m_ref, l_ref, acc_ref = refs
    else:
        q_ref, k_ref, v_ref, o_ref, m_ref, l_ref, acc_ref = refs
    j = pl.program_id(3)
    dv = HEAD_DIM

    @pl.when(j == 0)
    def _init():
        if has_ctx:
            kc = kc_ref[...]
            vc = vc_ref[...]
            for r in range(rep):
                s = _dot_nt(q_ref[:, r * dq:(r + 1) * dq], kc)
                m = jnp.max(s, axis=-1, keepdims=True)
                p = jnp.exp(s - m)
                m_ref[r] = m
                l_ref[r] = jnp.sum(p, axis=-1, keepdims=True)
                acc_ref[:, r * dv:(r + 1) * dv] = _dot(p.astype(BF16), vc)
        else:
            m_ref[...] = jnp.full(m_ref.shape, NEG, F32)
            l_ref[...] = jnp.zeros(l_ref.shape, F32)
            acc_ref[...] = jnp.zeros(acc_ref.shape, F32)

    k = k_ref[...]
    v = v_ref[...]
    for r in range(rep):
        s = _dot_nt(q_ref[:, r * dq:(r + 1) * dq], k)
        m_prev = m_ref[r]
        m_new = jnp.maximum(m_prev, jnp.max(s, axis=-1, keepdims=True))
        alpha = jnp.exp(m_prev - m_new)
        p = jnp.exp(s - m_new)
        l_ref[r] = alpha * l_ref[r] + jnp.sum(p, axis=-1, keepdims=True)
        acc_ref[:, r * dv:(r + 1) * dv] = alpha * acc_ref[:, r * dv:(r + 1) * dv] + _dot(p.astype(BF16), v)
        m_ref[r] = m_new

    @pl.when(j == pl.num_programs(3) - 1)
    def _fin():
        for r in range(rep):
            o_ref[:, r * dv:(r + 1) * dv] = (acc_ref[:, r * dv:(r + 1) * dv] / l_ref[r]).astype(BF16)


def _flash(q, k, v, kc, vc, batch, n_groups, rep, dq, bq, bk):
    lq = q.shape[0] // batch
    lk = k.shape[0] // batch
    nq, nk = lq // bq, lk // bk
    has_ctx = kc is not None
    dv = HEAD_DIM
    in_specs = [
        pl.BlockSpec((bq, rep * dq), lambda b, g, i, j: (b * nq + i, g)),
        pl.BlockSpec((bk, dq), lambda b, g, i, j: (b * nk + j, g)),
        pl.BlockSpec((bk, dv), lambda b, g, i, j: (b * nk + j, g)),
    ]
    args = [q, k, v]
    if has_ctx:
        lc = kc.shape[0] // batch
        in_specs += [
            pl.BlockSpec((lc, dq), lambda b, g, i, j: (b, g)),
            pl.BlockSpec((lc, dv), lambda b, g, i, j: (b, g)),
        ]
        args += [kc, vc]
    return pl.pallas_call(
        functools.partial(_flash_kernel, rep, dq, has_ctx),
        grid=(batch, n_groups, nq, nk),
        in_specs=in_specs,
        out_specs=pl.BlockSpec((bq, rep * dv), lambda b, g, i, j: (b * nq + i, g)),
        out_shape=jax.ShapeDtypeStruct((batch * lq, n_groups * rep * dv), BF16),
        scratch_shapes=[
            pltpu.VMEM((rep, bq, 1), F32),
            pltpu.VMEM((rep, bq, 1), F32),
            pltpu.VMEM((bq, rep * dv), F32),
        ],
        compiler_params=_cparams(("parallel", "parallel", "parallel", "arbitrary")),
        name="flash_attention",
    )(*args)


def _band_kernel(q_ref, kp_ref, kx_ref, kn_ref, vp_ref, vx_ref, vn_ref, kc_ref, vc_ref, bias_ref, sink_ref, o_ref):
    t = BAND_TILE
    q = q_ref[...]
    s0 = _dot_nt(q, kc_ref[...])
    s1 = _dot_nt(q, kp_ref[...]) + bias_ref[:, 0:t]
    s2 = _dot_nt(q, kx_ref[...]) + bias_ref[:, t:2 * t]
    s3 = _dot_nt(q, kn_ref[...]) + bias_ref[:, 2 * t:3 * t]
    sink = sink_ref[0:1, 0:1]
    m = jnp.max(s0, axis=-1, keepdims=True)
    for s in (s1, s2, s3):
        m = jnp.maximum(m, jnp.max(s, axis=-1, keepdims=True))
    m = jnp.maximum(m, sink)
    p0, p1, p2, p3 = (jnp.exp(s - m) for s in (s0, s1, s2, s3))
    l = jnp.exp(sink - m)
    for p in (p0, p1, p2, p3):
        l = l + jnp.sum(p, axis=-1, keepdims=True)
    acc = _dot(p0.astype(BF16), vc_ref[...]) + _dot(p1.astype(BF16), vp_ref[...])
    acc = acc + _dot(p2.astype(BF16), vx_ref[...]) + _dot(p3.astype(BF16), vn_ref[...])
    o_ref[...] = (acc / l).astype(BF16)


def _band_attention(q, k, v, kc, vc, bias, sink, batch):
    t = BAND_TILE
    seq = q.shape[0] // batch
    nq = seq // t
    lc = kc.shape[0] // batch
    n_heads = C_HEADS + D_HEADS
    rep_c = C_HEADS // C_KV_HEADS

    def kv_head(h):
        return jnp.where(h < C_HEADS, h // rep_c, h - C_HEADS + C_KV_HEADS)

    def tile_map(off):
        def f(b, h, i):
            return (b * nq + jnp.clip(i + off, 0, nq - 1), kv_head(h))
        return f

    def bias_map(b, h, i):
        cls = jnp.where(i == 0, 0, jnp.where(i == nq - 1, 2, 1))
        return (jnp.maximum(h - C_HEADS + 1, 0), cls, 0, 0)

    kv_spec = [pl.BlockSpec((t, HEAD_DIM), tile_map(off)) for off in (-1, 0, 1)]
    return pl.pallas_call(
        _band_kernel,
        grid=(batch, n_heads, nq),
        in_specs=[pl.BlockSpec((t, HEAD_DIM), lambda b, h, i: (b * nq + i, h))] + kv_spec + kv_spec + [
            pl.BlockSpec((lc, HEAD_DIM), lambda b, h, i: (b, kv_head(h))),
            pl.BlockSpec((lc, HEAD_DIM), lambda b, h, i: (b, kv_head(h))),
            pl.BlockSpec((None, None, t, 3 * t), bias_map),
            pl.BlockSpec((None, 8, LANES), lambda b, h, i: (h, 0, 0)),
        ],
        out_specs=pl.BlockSpec((t, HEAD_DIM), lambda b, h, i: (b * nq + i, h)),
        out_shape=jax.ShapeDtypeStruct((batch * seq, n_heads * HEAD_DIM), BF16),
        compiler_params=_cparams(("parallel", "parallel", "parallel")),
        name="band_attention",
    )(q, k, k, k, v, v, v, kc, vc, bias, sink)


def _outproj_kernel(n_in, *refs):
    a_refs = refs[:n_in]
    w_ref, g_ref, mod_ref, h_ref, o_ref, acc_ref = refs[n_in:]
    k0 = 0
    acc = None
    for a_ref in a_refs:
        kk = a_ref.shape[1]
        part = _dot(a_ref[...], w_ref[k0:k0 + kk, :])
        acc = part if acc is None else acc + part
        k0 += kk
    acc_ref[...] = acc
    _post_residual_to(acc_ref, h_ref, g_ref, mod_ref, 2, o_ref)


def _outproj(a_list, w, g_post, mod, h, rows_per_batch, bm):
    rows, d = h.shape
    tiles_per_batch = rows_per_batch // bm
    nb = mod.shape[0]

    def row_map(i):
        return (i, 0)

    def mod_map(i):
        return (i // tiles_per_batch if nb > 1 else 0, 0, 0)

    return pl.pallas_call(
        functools.partial(_outproj_kernel, len(a_list)),
        grid=(rows // bm,),
        in_specs=[pl.BlockSpec((bm, a.shape[1]), row_map) for a in a_list] + [
            _resident(w.shape),
            _resident((1, d)),
            pl.BlockSpec((None, 8, d), mod_map),
            pl.BlockSpec((bm, d), row_map),
        ],
        out_specs=pl.BlockSpec((bm, d), row_map),
        out_shape=jax.ShapeDtypeStruct((rows, d), F32),
        scratch_shapes=[pltpu.VMEM((bm, d), F32)],
        compiler_params=_cparams(("parallel",)),
        name="outproj",
    )(*a_list, w, g_post, mod, h)


def _ffn_kernel(h_ref, gpre_ref, gpost_ref, mod_ref, wg_ref, wu_ref, wd_ref, o_ref, u_ref):
    f = pl.program_id(1)

    @pl.when(f == 0)
    def _pro():
        _norm_mod_to(h_ref, gpre_ref, mod_ref, 3, 4, u_ref)

    u = u_ref[...]
    act = (_silu(_dot(u, wg_ref[...])) * _dot(u, wu_ref[...])).astype(BF16)
    part = _dot(act, wd_ref[...])

    @pl.when(f == 0)
    def _first():
        o_ref[...] = part

    @pl.when(f > 0)
    def _rest():
        o_ref[...] += part

    @pl.when(f == pl.num_programs(1) - 1)
    def _epi():
        _post_residual_to(o_ref, h_ref, gpost_ref, mod_ref, 5, o_ref)


def _ffn(h, g_pre, g_post, mod, wg, wu, wd, rows_per_batch, bm, bf):
    rows, d = h.shape
    dff = wg.shape[1]
    tiles_per_batch = rows_per_batch // bm
    nb = mod.shape[0]

    def row_map(i, f):
        return (i, 0)

    def mod_map(i, f):
        return (i // tiles_per_batch if nb > 1 else 0, 0, 0)

    return pl.pallas_call(
        _ffn_kernel,
        grid=(rows // bm, dff // bf),
        in_specs=[
            pl.BlockSpec((bm, d), row_map),
            _resident((1, d)), _resident((1, d)),
            pl.BlockSpec((None, 8, d), mod_map),
            pl.BlockSpec((d, bf), lambda i, f: (0, f)),
            pl.BlockSpec((d, bf), lambda i, f: (0, f)),
            pl.BlockSpec((bf, d), lambda i, f: (f, 0)),
        ],
        out_specs=pl.BlockSpec((bm, d), row_map),
        out_shape=jax.ShapeDtypeStruct((rows, d), F32),
        scratch_shapes=[pltpu.VMEM((bm, d), BF16)],
        compiler_params=_cparams(("parallel", "arbitrary")),
        name="ffn",
    )(h, g_pre, g_post, mod, wg, wu, wd)


def _rope_table(n_tok, dim):
    t = jnp.arange(n_tok, dtype=jnp.int32)
    row = (t // GRID_W).astype(F32)
    col = (t % GRID_W).astype(F32)
    half = dim // 2
    inv_freq = ROPE_BASE ** (-jnp.arange(0, half, 2, dtype=F32) / half)
    ar = row[:, None] * inv_freq[None, :]
    ac = col[:, None] * inv_freq[None, :]
    ang = jnp.concatenate([ar, ar, ac, ac], axis=-1)
    return jnp.cos(ang), jnp.sin(ang)


def _band_bias(rpb, seq):
    t = BAND_TILE
    nq = seq // t
    rows = seq // GRID_W
    kh = min(NA_KH, rows)
    tile_idx = jnp.array([0, 1, nq - 1], jnp.int32)
    qpos = tile_idx[:, None, None] * t + jnp.arange(t)[None, :, None]
    kpos = (tile_idx[:, None, None] - 1) * t + jnp.arange(3 * t)[None, None, :]
    in_seq = (kpos >= 0) & (kpos < seq)
    win = (jnp.abs(kpos - qpos) <= C_WINDOW) & in_seq
    bias_c = jnp.where(win, 0.0, NEG).astype(F32)[None]
    qr, qc = qpos // GRID_W, qpos % GRID_W
    kr, kc = jnp.floor_divide(kpos, GRID_W), jnp.mod(kpos, GRID_W)
    rs = jnp.clip(qr - kh // 2, 0, rows - kh)
    cs = jnp.clip(qc - NA_KW // 2, 0, GRID_W - NA_KW)
    ok = (kr >= rs) & (kr < rs + kh) & (kc >= cs) & (kc < cs + NA_KW) & in_seq
    di = jnp.clip(kr - qr + NA_KH - 1, 0, 2 * NA_KH - 2)
    dj = jnp.clip(kc - qc + NA_KW - 1, 0, 2 * NA_KW - 2)
    bias_d = jnp.where(ok[None], rpb[:, di, dj].astype(F32), NEG)
    return jnp.concatenate([bias_c, bias_d], axis=0)


def _pad_mod(m):
    nb = m.shape[0]
    return jnp.pad(m.reshape(nb, 6, D_MODEL), ((0, 0), (0, 2), (0, 0)))


def _row(v):
    return v.reshape(1, -1).astype(F32)


def kernel(x, c, ctx, c_ctx, w_mod, b_mod, g_mix_pre, g_mix_post, g_ffn_pre, g_ffn_post, w_gate, w_up, w_down,
           ab_w_in, ab_w_out, a_q_norm, a_k_norm, b_q_norm, b_kv_norm, b_w_uq, b_w_ukv, cd_w_in, cd_w_out,
           c_q_norm, c_k_norm, c_sink, d_q_norm, d_k_norm, d_rpb):
    batch, seq, d = x.shape
    n_ctx = ctx.shape[1]
    depth = w_mod.shape[0]
    assert d == D_MODEL and seq % 512 == 0 and n_ctx % 256 == 0 and batch + 1 <= 8
    assert depth == 2

    cvec = jnp.zeros((8, d), F32).at[:batch].set(c).at[batch].set(c_ctx)
    mod_all = _modulation(cvec, w_mod, b_mod)

    cos_h, sin_h = _rope_table(seq, HEAD_DIM)
    cos_r, sin_r = _rope_table(seq, B_ROPE)
    pad_r = ((0, 0), (0, LANES - B_ROPE))
    tabs_ab = (cos_h, sin_h, jnp.pad(cos_r, pad_r), jnp.pad(sin_r, pad_r))
    tabs_cd = (cos_h, sin_h)

    bm_lat = 512
    bm_ctx = min(512, batch * n_ctx)
    bm_ffn = 512
    bq = 512
    bk = 512

    h_lat = x.reshape(batch * seq, d)
    h_ctx = ctx.reshape(batch * n_ctx, d)

    for i in range(depth):
        last = i == depth - 1
        need_ctx = not last
        j = i // 2
        mod_lat = _pad_mod(mod_all[i, :batch])
        mod_ctx = _pad_mod(mod_all[i, batch:batch + 1])
        g1, g2, g3, g4 = _row(g_mix_pre[i]), _row(g_mix_post[i]), _row(g_ffn_pre[i]), _row(g_ffn_post[i])

        if i % 2 == 0:
            w_in = ab_w_in[j]
            kr_cols = jnp.pad(w_in[:, A_IN + B_Q_RANK + B_KV_RANK:], ((0, 0), (0, LANES - B_ROPE)))
            w_ab = jnp.concatenate([w_in[:, :A_IN + B_Q_RANK + B_KV_RANK], kr_cols], axis=1).astype(BF16)
            wuq = jnp.pad(b_w_uq[j].reshape(B_Q_RANK, B_HEADS, B_NOPE + B_ROPE),
                          ((0, 0), (0, 0), (0, B_QK_PAD - B_NOPE - B_ROPE))).reshape(B_Q_RANK, B_HEADS * B_QK_PAD)
            wukv = b_w_ukv[j].reshape(B_KV_RANK, B_HEADS, B_NOPE + B_V)
            wuk = wukv[:, :, :B_NOPE].reshape(B_KV_RANK, B_HEADS * B_NOPE)
            wuv = wukv[:, :, B_NOPE:].reshape(B_KV_RANK, B_HEADS * B_V)
            norms = (_row(a_q_norm[j]), _row(a_k_norm[j]), _row(b_q_norm[j]), _row(b_kv_norm[j]),
                     wuq.astype(BF16), wuk.astype(BF16), wuv.astype(BF16))
            qa, ka, va, qb, kb, vb = _inproj_ab(h_lat, g1, mod_lat, w_ab, *norms, tabs_ab, seq, bm_lat)
            qac, kac, vac, qbc, kbc, vbc = _inproj_ab(h_ctx, g1, mod_ctx, w_ab, *norms, None, batch * n_ctx, bm_ctx)
            rep_a = A_HEADS // A_KV_HEADS
            oa = _flash(qa, ka, va, kac, vac, batch, A_KV_HEADS, rep_a, HEAD_DIM, bq, bk)
            ob = _flash(qb, kb, vb, kbc, vbc, batch, B_HEADS, 1, B_QK_PAD, bq, bk)
            w_out = ab_w_out[j].astype(BF16)
            attn_lat = [oa, ob]
            if need_ctx:
                oac = _flash(qac, kac, vac, None, None, batch, A_KV_HEADS, rep_a, HEAD_DIM, n_ctx, n_ctx)
                obc = _flash(qbc, kbc, vbc, None, None, batch, B_HEADS, 1, B_QK_PAD, n_ctx, n_ctx)
                attn_ctx = [oac, obc]
        else:
            w_cd = cd_w_in[j].astype(BF16)
            norms = (_row(c_q_norm[j]), _row(c_k_norm[j]), _row(d_q_norm[j]), _row(d_k_norm[j]))
            q_l, k_l, v_l = _inproj_cd(h_lat, g1, mod_lat, w_cd, *norms, tabs_cd, seq, bm_lat)
            q_c, k_c, v_c = _inproj_cd(h_ctx, g1, mod_ctx, w_cd, *norms, None, batch * n_ctx, bm_ctx)
            bias = _band_bias(d_rpb[j], seq)
            sink = jnp.concatenate([c_sink[j].astype(F32), jnp.full((D_HEADS,), NEG, F32)])
            sink = jnp.broadcast_to(sink[:, None, None], (C_HEADS + D_HEADS, 8, LANES))
            attn_lat = [_band_attention(q_l, k_l, v_l, k_c, v_c, bias, sink, batch)]
            w_out = cd_w_out[j].astype(BF16)

        wg, wu, wd = w_gate[i].astype(BF16), w_up[i].astype(BF16), w_down[i].astype(BF16)
        h_lat = _outproj(attn_lat, w_out, g2, mod_lat, h_lat, seq, bm_lat)
        h_lat = _ffn(h_lat, g3, g4, mod_lat, wg, wu, wd, seq, bm_ffn, 512)
        if need_ctx:
            h_ctx = _outproj(attn_ctx, w_out, g2, mod_ctx, h_ctx, batch * n_ctx, bm_ctx)
            h_ctx = _ffn(h_ctx, g3, g4, mod_ctx, wg, wu, wd, batch * n_ctx, bm_ctx, 512)
    return h_lat.reshape(batch, seq, d)
```

```python
import functools

import jax
import jax.numpy as jnp
import numpy as np
from jax import lax
from jax.experimental import pallas as pl
from jax.experimental.pallas import tpu as pltpu

F32 = jnp.float32
BF16 = jnp.bfloat16

D_MODEL = 2048
GRID_W = 64
HEAD_DIM = 128
ROPE_BASE = 10000.0
EPS = 1e-6
NEG = -1e30
LOG2E = 1.4426950408889634

A_HEADS, A_KV_HEADS = 8, 2
B_HEADS, B_Q_RANK, B_KV_RANK, B_NOPE, B_ROPE, B_V = 8, 512, 512, 128, 64, 128
C_HEADS, C_KV_HEADS, C_WINDOW = 8, 2, 128
D_HEADS, NA_KH, NA_KW = 8, 8, 16
A_IN = (A_HEADS + 2 * A_KV_HEADS) * HEAD_DIM
C_IN = (C_HEADS + 2 * C_KV_HEADS) * HEAD_DIM
D_FF = 5632

LANES = 128
MXU_COLS = 256
B_QK_PAD = 256
BAND_TILE = 256
VMEM_LIMIT = 56 * 1024 * 1024
ROW_CHUNK = 128


def _cparams(sem):
    return pltpu.CompilerParams(dimension_semantics=sem, vmem_limit_bytes=VMEM_LIMIT)


def _resident(shape):
    nd = len(shape)
    return pl.BlockSpec(shape, lambda *_: (0,) * nd, pipeline_mode=pl.Buffered(1))


def _silu(x):
    return x * (1.0 / (1.0 + jnp.exp(-x)))


def _rms(x, g):
    ms = jnp.mean(x * x, axis=-1, keepdims=True)
    return x * lax.rsqrt(ms + EPS) * g


def _dot(a, b):
    return jnp.dot(a, b, preferred_element_type=F32)


def _dot_nt(a, b):
    return lax.dot_general(a, b, (((1,), (1,)), ((), ())), preferred_element_type=F32)


def _mod_kernel(c_ref, w_ref, b_ref, o_ref):
    s = _silu(c_ref[...])
    s_hi = s.astype(BF16)
    s_lo = (s - s_hi.astype(F32)).astype(BF16)
    w = w_ref[...]
    w_hi = w.astype(BF16)
    w_lo = (w - w_hi.astype(F32)).astype(BF16)
    o_ref[...] = _dot(s_hi, w_hi) + (_dot(s_lo, w_hi) + _dot(s_hi, w_lo)) + b_ref[...]


def _modulation(cvec, w_mod, b_mod):
    depth, d, n = w_mod.shape
    bn = 1024
    return pl.pallas_call(
        _mod_kernel,
        grid=(depth, n // bn),
        in_specs=[
            pl.BlockSpec((8, d), lambda i, j: (0, 0)),
            pl.BlockSpec((None, d, bn), lambda i, j: (i, 0, j)),
            pl.BlockSpec((None, 1, bn), lambda i, j: (i, 0, j)),
        ],
        out_specs=pl.BlockSpec((None, 8, bn), lambda i, j: (i, 0, j)),
        out_shape=jax.ShapeDtypeStruct((depth, 8, n), F32),
        compiler_params=_cparams(("parallel", "parallel")),
        name="modulation",
    )(cvec, w_mod, b_mod.reshape(depth, 1, n))


def _norm_mod_to(h_ref, g_ref, mod_ref, shift_row, scale_row, u_ref):
    bm = h_ref.shape[0]
    chunk = min(ROW_CHUNK, bm)
    g = g_ref[...]
    sh = mod_ref[shift_row:shift_row + 1, :]
    sc = mod_ref[scale_row:scale_row + 1, :]

    def body(c, carry):
        r = pl.multiple_of(c * chunk, chunk)
        n = _rms(h_ref[pl.ds(r, chunk), :], g)
        u_ref[pl.ds(r, chunk), :] = (n * (1.0 + sc) + sh).astype(BF16)
        return carry

    lax.fori_loop(0, bm // chunk, body, 0)


def _post_residual_to(acc_ref, h_ref, g_ref, mod_ref, gate_row, o_ref):
    bm = h_ref.shape[0]
    chunk = min(ROW_CHUNK, bm)
    g = g_ref[...]
    gate = mod_ref[gate_row:gate_row + 1, :]

    def body(c, carry):
        r = pl.multiple_of(c * chunk, chunk)
        o_ref[pl.ds(r, chunk), :] = h_ref[pl.ds(r, chunk), :] + gate * _rms(acc_ref[pl.ds(r, chunk), :], g)
        return carry

    lax.fori_loop(0, bm // chunk, body, 0)


def _rope(x, cos, sin, quarter):
    lane = lax.broadcasted_iota(jnp.int32, x.shape, 1)
    first = (lane % (2 * quarter)) < quarter
    rot = jnp.where(first, -pltpu.roll(x, LANES - quarter, 1), pltpu.roll(x, quarter, 1))
    return x * cos + rot * sin


def _emit_heads(u_ref, w_ref, col0, n_heads, gain, scale, rope, o_ref, ocol0):
    h = 0
    while h < n_heads:
        nh = min(MXU_COLS // HEAD_DIM, n_heads - h)
        c = col0 + h * HEAD_DIM
        p = _dot(u_ref[...], w_ref[:, c:c + nh * HEAD_DIM])
        for t in range(nh):
            x = p[:, t * HEAD_DIM:(t + 1) * HEAD_DIM]
            if gain is not None:
                x = _rms(x, gain)
                if rope is not None:
                    x = _rope(x, rope[0], rope[1], HEAD_DIM // 4)
                if scale != 1.0:
                    x = x * scale
            oc = ocol0 + (h + t) * HEAD_DIM
            o_ref[:, oc:oc + HEAD_DIM] = x.astype(BF16)
        h += nh


def _inproj_ab_kernel(use_rope, h_ref, g_ref, mod_ref, w_ref, aqn_ref, akn_ref, bqn_ref, bkvn_ref,
                      wuq_ref, wuk_ref, wuv_ref, cosh_ref, sinh_ref, cosr_ref, sinr_ref,
                      qa_ref, ka_ref, va_ref, qb_ref, kb_ref, vb_ref, u_ref, c_ref):
    _norm_mod_to(h_ref, g_ref, mod_ref, 0, 1, u_ref)
    rope_h = (cosh_ref[...], sinh_ref[...]) if use_rope else None
    a_scale = HEAD_DIM ** -0.5 * LOG2E
    _emit_heads(u_ref, w_ref, 0, A_HEADS, aqn_ref[...], a_scale, rope_h, qa_ref, 0)
    _emit_heads(u_ref, w_ref, A_HEADS * HEAD_DIM, A_KV_HEADS, akn_ref[...], 1.0, rope_h, ka_ref, 0)
    _emit_heads(u_ref, w_ref, (A_HEADS + A_KV_HEADS) * HEAD_DIM, A_KV_HEADS, None, 1.0, None, va_ref, 0)

    b_scale = (B_NOPE + B_ROPE) ** -0.5 * LOG2E
    cq = _dot(u_ref[...], w_ref[:, A_IN:A_IN + B_Q_RANK])
    c_ref[...] = _rms(cq, bqn_ref[...]).astype(BF16)
    for hd in range(B_HEADS):
        q = _dot(c_ref[...], wuq_ref[:, hd * B_QK_PAD:(hd + 1) * B_QK_PAD])
        q_nope = q[:, :B_NOPE]
        q_rope = q[:, B_NOPE:]
        if use_rope:
            q_rope = _rope(q_rope, cosr_ref[...], sinr_ref[...], B_ROPE // 4)
        qb_ref[:, hd * B_QK_PAD:hd * B_QK_PAD + B_NOPE] = (q_nope * b_scale).astype(BF16)
        qb_ref[:, hd * B_QK_PAD + B_NOPE:(hd + 1) * B_QK_PAD] = (q_rope * b_scale).astype(BF16)
    kr = _dot(u_ref[...], w_ref[:, A_IN + B_Q_RANK + B_KV_RANK:A_IN + B_Q_RANK + B_KV_RANK + LANES])
    if use_rope:
        kr = _rope(kr, cosr_ref[...], sinr_ref[...], B_ROPE // 4)
    kr = kr.astype(BF16)
    ckv = _dot(u_ref[...], w_ref[:, A_IN + B_Q_RANK:A_IN + B_Q_RANK + B_KV_RANK])
    c_ref[...] = _rms(ckv, bkvn_ref[...]).astype(BF16)
    for pair in range(B_HEADS * B_NOPE // MXU_COLS):
        kn = _dot(c_ref[...], wuk_ref[:, pair * MXU_COLS:(pair + 1) * MXU_COLS])
        for t in range(MXU_COLS // B_NOPE):
            hd = pair * (MXU_COLS // B_NOPE) + t
            kb_ref[:, hd * B_QK_PAD:hd * B_QK_PAD + B_NOPE] = kn[:, t * B_NOPE:(t + 1) * B_NOPE].astype(BF16)
            kb_ref[:, hd * B_QK_PAD + B_NOPE:(hd + 1) * B_QK_PAD] = kr
        vb_ref[:, pair * MXU_COLS:(pair + 1) * MXU_COLS] = _dot(
            c_ref[...], wuv_ref[:, pair * MXU_COLS:(pair + 1) * MXU_COLS]).astype(BF16)


def _inproj_ab(h, g_pre, mod, w, aqn, akn, bqn, bkvn, wuq, wuk, wuv, rope_tabs, rows_per_batch, bm):
    rows, d = h.shape
    use_rope = rope_tabs is not None
    tiles_per_batch = rows_per_batch // bm
    nb = mod.shape[0]

    def row_map(i):
        return (i, 0)

    def mod_map(i):
        return (i // tiles_per_batch if nb > 1 else 0, 0, 0)

    def tab_map(i):
        return (i % tiles_per_batch, 0)

    if use_rope:
        tabs = list(rope_tabs)
    else:
        tabs = [jnp.zeros((bm, LANES), F32)] * 4

        def tab_map(i):
            return (0, 0)

    outs = [
        (A_HEADS * HEAD_DIM), (A_KV_HEADS * HEAD_DIM), (A_KV_HEADS * HEAD_DIM),
        (B_HEADS * B_QK_PAD), (B_HEADS * B_QK_PAD), (B_HEADS * B_V),
    ]
    return pl.pallas_call(
        functools.partial(_inproj_ab_kernel, use_rope),
        grid=(rows // bm,),
        in_specs=[
            pl.BlockSpec((bm, d), row_map),
            _resident((1, d)),
            pl.BlockSpec((None, 8, d), mod_map),
            _resident(w.shape),
            _resident((1, HEAD_DIM)), _resident((1, HEAD_DIM)),
            _resident((1, B_Q_RANK)), _resident((1, B_KV_RANK)),
            _resident(wuq.shape), _resident(wuk.shape), _resident(wuv.shape),
        ] + [pl.BlockSpec((bm, LANES), tab_map)] * 4,
        out_specs=[pl.BlockSpec((bm, n), row_map) for n in outs],
        out_shape=[jax.ShapeDtypeStruct((rows, n), BF16) for n in outs],
        scratch_shapes=[pltpu.VMEM((bm, d), BF16), pltpu.VMEM((bm, B_Q_RANK), BF16)],
        compiler_params=_cparams(("parallel",)),
        name="inproj_ab",
    )(h, g_pre, mod, w, aqn, akn, bqn, bkvn, wuq, wuk, wuv, *tabs)


def _inproj_cd_kernel(use_rope, h_ref, g_ref, mod_ref, w_ref, cqn_ref, ckn_ref, dqn_ref, dkn_ref,
                      cosh_ref, sinh_ref, q_ref, k_ref, v_ref, u_ref):
    _norm_mod_to(h_ref, g_ref, mod_ref, 0, 1, u_ref)
    rope_h = (cosh_ref[...], sinh_ref[...]) if use_rope else None
    scale = HEAD_DIM ** -0.5 * LOG2E
    kv_c = C_KV_HEADS * HEAD_DIM
    kv_d = D_HEADS * HEAD_DIM
    _emit_heads(u_ref, w_ref, 0, C_HEADS, cqn_ref[...], scale, rope_h, q_ref, 0)
    _emit_heads(u_ref, w_ref, C_HEADS * HEAD_DIM, C_KV_HEADS, ckn_ref[...], 1.0, rope_h, k_ref, kv_d)
    _emit_heads(u_ref, w_ref, C_HEADS * HEAD_DIM + kv_c, C_KV_HEADS, None, 1.0, None, v_ref, kv_d)
    _emit_heads(u_ref, w_ref, C_IN, D_HEADS, dqn_ref[...], scale, None, q_ref, C_HEADS * HEAD_DIM)
    _emit_heads(u_ref, w_ref, C_IN + D_HEADS * HEAD_DIM, D_HEADS, dkn_ref[...], 1.0, None, k_ref, 0)
    _emit_heads(u_ref, w_ref, C_IN + 2 * D_HEADS * HEAD_DIM, D_HEADS, None, 1.0, None, v_ref, 0)


def _inproj_cd(h, g_pre, mod, w, cqn, ckn, dqn, dkn, rope_tabs, rows_per_batch, bm):
    rows, d = h.shape
    use_rope = rope_tabs is not None
    tiles_per_batch = rows_per_batch // bm
    nb = mod.shape[0]

    def row_map(i):
        return (i, 0)

    def mod_map(i):
        return (i // tiles_per_batch if nb > 1 else 0, 0, 0)

    if use_rope:
        tabs = list(rope_tabs)

        def tab_map(i):
            return (i % tiles_per_batch, 0)
    else:
        tabs = [jnp.zeros((bm, LANES), F32)] * 2

        def tab_map(i):
            return (0, 0)

    outs = [(C_HEADS + D_HEADS) * HEAD_DIM, (C_KV_HEADS + D_HEADS) * HEAD_DIM, (C_KV_HEADS + D_HEADS) * HEAD_DIM]
    return pl.pallas_call(
        functools.partial(_inproj_cd_kernel, use_rope),
        grid=(rows // bm,),
        in_specs=[
            pl.BlockSpec((bm, d), row_map),
            _resident((1, d)),
            pl.BlockSpec((None, 8, d), mod_map),
            _resident(w.shape),
            _resident((1, HEAD_DIM)), _resident((1, HEAD_DIM)), _resident((1, HEAD_DIM)), _resident((1, HEAD_DIM)),
        ] + [pl.BlockSpec((bm, LANES), tab_map)] * 2,
        out_specs=[pl.BlockSpec((bm, n), row_map) for n in outs],
        out_shape=[jax.ShapeDtypeStruct((rows, n), BF16) for n in outs],
        scratch_shapes=[pltpu.VMEM((bm, d), BF16)],
        compiler_params=_cparams(("parallel",)),
        name="inproj_cd",
    )(h, g_pre, mod, w, cqn, ckn, dqn, dkn, *tabs)


def _flash_step(q_ref, kt, vt, m_ref, acc_ref, rep, dq, sub):
    bq = q_ref.shape[0]
    n_chunks = kt.shape[0] // LANES
    for r in range(rep):
        for sb in range(bq // sub):
            rows = slice(sb * sub, (sb + 1) * sub)
            mcol = slice(r * LANES, (r + 1) * LANES)
            acol = slice(r * 2 * LANES, (r + 1) * 2 * LANES)
            s = _dot_nt(q_ref[rows, r * dq:(r + 1) * dq], kt)
            chunks = [s[:, c * LANES:(c + 1) * LANES] for c in range(n_chunks)]
            lane_max = functools.reduce(jnp.maximum, chunks)
            m_prev = m_ref[rows, mcol]
            m_new = jnp.maximum(m_prev, jnp.max(lane_max, axis=-1, keepdims=True))
            alpha = jnp.exp2(m_prev - m_new)
            p = jnp.concatenate([jnp.exp2(ch - m_new) for ch in chunks], axis=1).astype(BF16)
            acc_ref[rows, acol] = jnp.concatenate([alpha, alpha], axis=1) * acc_ref[rows, acol] + _dot(p, vt)
            m_ref[rows, mcol] = m_new


def _flash_kernel(rep, dq, has_ctx, sub, *refs):
    if has_ctx:
        q_ref, k_ref, v_ref, kc_ref, vc_ref, o_ref, m_ref, acc_ref = refs
    else:
        q_ref, k_ref, v_ref, o_ref, m_ref, acc_ref = refs
    j = pl.program_id(3)

    def with_ones(v):
        return jnp.concatenate([v, jnp.ones(v.shape, BF16)], axis=1)

    @pl.when(j == 0)
    def _init():
        m_ref[...] = jnp.full(m_ref.shape, NEG, F32)
        acc_ref[...] = jnp.zeros(acc_ref.shape, F32)
        if has_ctx:
            _flash_step(q_ref, kc_ref[...], with_ones(vc_ref[...]), m_ref, acc_ref, rep, dq, sub)

    _flash_step(q_ref, k_ref[...], with_ones(v_ref[...]), m_ref, acc_ref, rep, dq, sub)

    @pl.when(j == pl.num_programs(3) - 1)
    def _fin():
        for r in range(rep):
            c0 = r * 2 * LANES
            o_ref[:, r * HEAD_DIM:(r + 1) * HEAD_DIM] = (
                acc_ref[:, c0:c0 + LANES] / acc_ref[:, c0 + LANES:c0 + 2 * LANES]).astype(BF16)


def _flash(q, k, v, kc, vc, batch, n_groups, rep, dq, bq, bk, sub):
    lq = q.shape[0] // batch
    lk = k.shape[0] // batch
    nq, nk = lq // bq, lk // bk
    has_ctx = kc is not None
    dv = HEAD_DIM
    assert dv == LANES
    in_specs = [
        pl.BlockSpec((bq, rep * dq), lambda b, g, i, j: (b * nq + i, g)),
        pl.BlockSpec((bk, dq), lambda b, g, i, j: (b * nk + j, g)),
        pl.BlockSpec((bk, dv), lambda b, g, i, j: (b * nk + j, g)),
    ]
    args = [q, k, v]
    if has_ctx:
        lc = kc.shape[0] // batch
        in_specs += [
            pl.BlockSpec((lc, dq), lambda b, g, i, j: (b, g)),
            pl.BlockSpec((lc, dv), lambda b, g, i, j: (b, g)),
        ]
        args += [kc, vc]
    return pl.pallas_call(
        functools.partial(_flash_kernel, rep, dq, has_ctx, min(sub, bq)),
        grid=(batch, n_groups, nq, nk),
        in_specs=in_specs,
        out_specs=pl.BlockSpec((bq, rep * dv), lambda b, g, i, j: (b * nq + i, g)),
        out_shape=jax.ShapeDtypeStruct((batch * lq, n_groups * rep * dv), BF16),
        scratch_shapes=[
            pltpu.VMEM((bq, rep * LANES), F32),
            pltpu.VMEM((bq, rep * 2 * LANES), F32),
        ],
        compiler_params=_cparams(("parallel", "parallel", "parallel", "arbitrary")),
        name="flash_attention",
    )(*args)


def _band_kernel(rep, shared, has_sink, *refs):
    if has_sink:
        q_ref, kp_ref, kx_ref, kn_ref, vp_ref, vx_ref, vn_ref, kc_ref, vc_ref, bias_ref, sink_ref, o_ref = refs
    else:
        q_ref, kp_ref, kx_ref, kn_ref, vp_ref, vx_ref, vn_ref, kc_ref, vc_ref, bias_ref, o_ref = refs
    t = BAND_TILE
    for r in range(rep):
        hcol = slice(r * HEAD_DIM, (r + 1) * HEAD_DIM)
        kcol = slice(0, HEAD_DIM) if shared else hcol
        bias = bias_ref if shared else bias_ref.at[r]
        q = q_ref[:, hcol]
        scores = [_dot_nt(q, kc_ref[:, kcol])]
        for n, k_ref in enumerate((kp_ref, kx_ref, kn_ref)):
            scores.append(_dot_nt(q, k_ref[:, kcol]) + bias[:, n * t:(n + 1) * t])
        chunks = [s[:, c * LANES:(c + 1) * LANES] for s in scores for c in range(s.shape[1] // LANES)]
        m = jnp.max(functools.reduce(jnp.maximum, chunks), axis=-1, keepdims=True)
        if has_sink:
            sink = sink_ref[r, 0:1, 0:1]
            m = jnp.maximum(m, sink)
        probs = [jnp.exp2(s - m) for s in scores]
        pch = [p[:, c * LANES:(c + 1) * LANES] for p in probs for c in range(p.shape[1] // LANES)]
        l = jnp.sum(functools.reduce(jnp.add, pch), axis=-1, keepdims=True)
        if has_sink:
            l = l + jnp.exp2(sink - m)
        acc = _dot(probs[0].astype(BF16), vc_ref[:, kcol])
        for p, v_ref in zip(probs[1:], (vp_ref, vx_ref, vn_ref)):
            acc = acc + _dot(p.astype(BF16), v_ref[:, kcol])
        o_ref[:, hcol] = (acc / l).astype(BF16)


def _band_attention(q, k, v, kc, vc, bias, sink, batch, q_col0, kv_col0, shared):
    t = BAND_TILE
    rep = 4
    seq = q.shape[0] // batch
    nq = seq // t
    lc = kc.shape[0] // batch
    qw = rep * HEAD_DIM
    kw = HEAD_DIM if shared else qw
    n_groups = 2
    has_sink = sink is not None

    def tile_map(off):
        def f(g, b, i):
            return (b * nq + jnp.clip(i + off, 0, nq - 1), kv_col0 // kw + g)
        return f

    def cls(i):
        return jnp.where(i == 0, 0, jnp.where(i == nq - 1, 2, 1))

    kv_spec = [pl.BlockSpec((t, kw), tile_map(off)) for off in (-1, 0, 1)]
    ctx_spec = pl.BlockSpec((lc, kw), lambda g, b, i: (b, kv_col0 // kw + g))
    if shared:
        bias_spec = pl.BlockSpec((None, t, 3 * t), lambda g, b, i: (cls(i), 0, 0))
    else:
        bias_spec = pl.BlockSpec((rep, None, t, 3 * t), lambda g, b, i: (g, cls(i), 0, 0))
    in_specs = [pl.BlockSpec((t, qw), lambda g, b, i: (b * nq + i, q_col0 // qw + g))]
    in_specs += kv_spec + kv_spec + [ctx_spec, ctx_spec, bias_spec]
    args = [q, k, k, k, v, v, v, kc, vc, bias]
    if has_sink:
        in_specs.append(pl.BlockSpec((rep, 8, LANES), lambda g, b, i: (g, 0, 0)))
        args.append(sink)
    return pl.pallas_call(
        functools.partial(_band_kernel, rep, shared, has_sink),
        grid=(n_groups, batch, nq),
        in_specs=in_specs,
        out_specs=pl.BlockSpec((t, qw), lambda g, b, i: (b * nq + i, g)),
        out_shape=jax.ShapeDtypeStruct((batch * seq, n_groups * qw), BF16),
        compiler_params=_cparams(("parallel", "parallel", "parallel")),
        name="band_attention",
    )(*args)


def _outproj_kernel(n_in, *refs):
    a_refs = refs[:n_in]
    w_ref, g_ref, mod_ref, h_ref, o_ref, acc_ref = refs[n_in:]
    k0 = 0
    acc = None
    for a_ref in a_refs:
        kk = a_ref.shape[1]
        part = _dot(a_ref[...], w_ref[k0:k0 + kk, :])
        acc = part if acc is None else acc + part
        k0 += kk
    acc_ref[...] = acc
    _post_residual_to(acc_ref, h_ref, g_ref, mod_ref, 2, o_ref)


def _outproj(a_list, w, g_post, mod, h, rows_per_batch, bm):
    rows, d = h.shape
    tiles_per_batch = rows_per_batch // bm
    nb = mod.shape[0]

    def row_map(i):
        return (i, 0)

    def mod_map(i):
        return (i // tiles_per_batch if nb > 1 else 0, 0, 0)

    return pl.pallas_call(
        functools.partial(_outproj_kernel, len(a_list)),
        grid=(rows // bm,),
        in_specs=[pl.BlockSpec((bm, a.shape[1]), row_map) for a in a_list] + [
            _resident(w.shape),
            _resident((1, d)),
            pl.BlockSpec((None, 8, d), mod_map),
            pl.BlockSpec((bm, d), row_map),
        ],
        out_specs=pl.BlockSpec((bm, d), row_map),
        out_shape=jax.ShapeDtypeStruct((rows, d), F32),
        scratch_shapes=[pltpu.VMEM((bm, d), F32)],
        compiler_params=_cparams(("parallel",)),
        name="outproj",
    )(*a_list, w, g_post, mod, h)


def _ffn_kernel(h_ref, gpre_ref, gpost_ref, mod_ref, wg_ref, wu_ref, wd_ref, o_ref, u_ref):
    f = pl.program_id(1)

    @pl.when(f == 0)
    def _pro():
        _norm_mod_to(h_ref, gpre_ref, mod_ref, 3, 4, u_ref)
        o_ref[...] = jnp.zeros(o_ref.shape, F32)

    u = u_ref[...]
    act = (_silu(_dot(u, wg_ref[...])) * _dot(u, wu_ref[...])).astype(BF16)
    o_ref[...] += _dot(act, wd_ref[...])

    @pl.when(f == pl.num_programs(1) - 1)
    def _epi():
        _post_residual_to(o_ref, h_ref, gpost_ref, mod_ref, 5, o_ref)


def _ffn(h, g_pre, g_post, mod, wg, wu, wd, rows_per_batch, bm, bf):
    rows, d = h.shape
    dff = wg.shape[1]
    tiles_per_batch = rows_per_batch // bm
    nb = mod.shape[0]

    def row_map(i, f):
        return (i, 0)

    def mod_map(i, f):
        return (i // tiles_per_batch if nb > 1 else 0, 0, 0)

    return pl.pallas_call(
        _ffn_kernel,
        grid=(rows // bm, dff // bf),
        in_specs=[
            pl.BlockSpec((bm, d), row_map),
            _resident((1, d)), _resident((1, d)),
            pl.BlockSpec((None, 8, d), mod_map),
            pl.BlockSpec((d, bf), lambda i, f: (0, f)),
            pl.BlockSpec((d, bf), lambda i, f: (0, f)),
            pl.BlockSpec((bf, d), lambda i, f: (f, 0)),
        ],
        out_specs=pl.BlockSpec((bm, d), row_map),
        out_shape=jax.ShapeDtypeStruct((rows, d), F32),
        scratch_shapes=[pltpu.VMEM((bm, d), BF16)],
        compiler_params=_cparams(("parallel", "arbitrary")),
        name="ffn",
    )(h, g_pre, g_post, mod, wg, wu, wd)


def _rope_table(n_tok, dim):
    t = jnp.arange(n_tok, dtype=jnp.int32)
    row = (t // GRID_W).astype(F32)
    col = (t % GRID_W).astype(F32)
    half = dim // 2
    inv_freq = ROPE_BASE ** (-jnp.arange(0, half, 2, dtype=F32) / half)
    ar = row[:, None] * inv_freq[None, :]
    ac = col[:, None] * inv_freq[None, :]
    ang = jnp.concatenate([ar, ar, ac, ac], axis=-1)
    return jnp.cos(ang), jnp.sin(ang)


def _rpb_cols_kernel(r_ref, e_ref, o_ref):
    x = r_ref[...]
    hi = x.astype(BF16)
    r1 = x - hi.astype(F32)
    mid = r1.astype(BF16)
    lo = (r1 - mid.astype(F32)).astype(BF16)
    e = e_ref[...]
    o_ref[...] = (_dot(hi, e) + _dot(mid, e)) + _dot(lo, e)


def _band_bias(rpb, seq):
    t, g = BAND_TILE, GRID_W
    rpt = t // g
    nq = seq // t
    rows = seq // g
    kh = min(NA_KH, rows)
    n_h, n_di, n_dj = rpb.shape
    tiles = (0, 1, nq - 1)

    qpos = np.array(tiles)[:, None, None] * t + np.arange(t)[None, :, None]
    kpos = (np.array(tiles)[:, None, None] - 1) * t + np.arange(3 * t)[None, None, :]
    win = (np.abs(kpos - qpos) <= C_WINDOW) & (kpos >= 0) & (kpos < seq)
    bias_c = jnp.asarray(np.where(win, 0.0, NEG).astype(np.float32))

    qc = np.arange(g)[:, None]
    kc = np.arange(g)[None, :]
    cs = np.clip(qc - NA_KW // 2, 0, g - NA_KW)
    col_ok = (kc >= cs) & (kc < cs + NA_KW)
    onehot = np.zeros((LANES, g, g), np.float32)
    qq, kk = np.nonzero(col_ok)
    onehot[(kk - qq + NA_KW - 1), qq, kk] = 1.0
    assert n_h * n_di <= LANES and n_dj <= LANES
    rpb2d = jnp.pad(rpb.reshape(n_h * n_di, n_dj).astype(F32), ((0, LANES - n_h * n_di), (0, LANES - n_dj)))
    cols = pl.pallas_call(
        _rpb_cols_kernel,
        out_shape=jax.ShapeDtypeStruct((LANES, g * g), F32),
        name="rpb_columns",
    )(rpb2d, jnp.asarray(onehot.reshape(LANES, g * g), BF16))
    cols = cols[:n_h * n_di].reshape(n_h, n_di, g, g) + jnp.asarray(np.where(col_ok, 0.0, NEG).astype(np.float32))
    ext = jnp.concatenate([cols, jnp.full((n_h, 1, g, g), NEG, F32)], axis=1)

    blocks = []
    for ti in tiles:
        for a in range(rpt):
            qr = ti * rpt + a
            rs = min(max(qr - kh // 2, 0), rows - kh)
            for b in range(3 * rpt):
                kr = (ti - 1) * rpt + b
                blocks.append(ext[:, kr - qr + NA_KH - 1] if rs <= kr < rs + kh else ext[:, n_di])
    bias_d = jnp.stack(blocks, axis=1).reshape(n_h, len(tiles), rpt, 3 * rpt, g, g)
    bias_d = bias_d.transpose(0, 1, 2, 4, 3, 5).reshape(n_h, len(tiles), t, 3 * t)
    return bias_c * LOG2E, bias_d * LOG2E


def _pad_mod(m):
    nb = m.shape[0]
    return jnp.pad(m.reshape(nb, 6, D_MODEL), ((0, 0), (0, 2), (0, 0)))


def _row(v):
    return v.reshape(1, -1).astype(F32)


def kernel(x, c, ctx, c_ctx, w_mod, b_mod, g_mix_pre, g_mix_post, g_ffn_pre, g_ffn_post, w_gate, w_up, w_down,
           ab_w_in, ab_w_out, a_q_norm, a_k_norm, b_q_norm, b_kv_norm, b_w_uq, b_w_ukv, cd_w_in, cd_w_out,
           c_q_norm, c_k_norm, c_sink, d_q_norm, d_k_norm, d_rpb):
    batch, seq, d = x.shape
    n_ctx = ctx.shape[1]
    depth = w_mod.shape[0]
    assert d == D_MODEL and seq % 512 == 0 and n_ctx % 256 == 0 and batch + 1 <= 8
    assert depth == 2

    cvec = jnp.zeros((8, d), F32).at[:batch].set(c).at[batch].set(c_ctx)
    mod_all = _modulation(cvec, w_mod, b_mod)

    cos_h, sin_h = _rope_table(seq, HEAD_DIM)
    cos_r, sin_r = _rope_table(seq, B_ROPE)
    pad_r = ((0, 0), (0, LANES - B_ROPE))
    tabs_ab = (cos_h, sin_h, jnp.pad(cos_r, pad_r), jnp.pad(sin_r, pad_r))
    tabs_cd = (cos_h, sin_h)

    bm_lat = 512
    bm_ctx = min(512, batch * n_ctx)
    bm_ffn = 512
    bq_a = min(512, seq)
    bq_b = min(1024, seq)
    bk = min(1024, seq)
    sub = 256

    h_lat = x.reshape(batch * seq, d)
    h_ctx = ctx.reshape(batch * n_ctx, d)

    for i in range(depth):
        last = i == depth - 1
        need_ctx = not last
        j = i // 2
        mod_lat = _pad_mod(mod_all[i, :batch])
        mod_ctx = _pad_mod(mod_all[i, batch:batch + 1])
        g1, g2, g3, g4 = _row(g_mix_pre[i]), _row(g_mix_post[i]), _row(g_ffn_pre[i]), _row(g_ffn_post[i])

        if i % 2 == 0:
            w_in = ab_w_in[j]
            kr_cols = jnp.pad(w_in[:, A_IN + B_Q_RANK + B_KV_RANK:], ((0, 0), (0, LANES - B_ROPE)))
            w_ab = jnp.concatenate([w_in[:, :A_IN + B_Q_RANK + B_KV_RANK], kr_cols], axis=1).astype(BF16)
            wuq = jnp.pad(b_w_uq[j].reshape(B_Q_RANK, B_HEADS, B_NOPE + B_ROPE),
                          ((0, 0), (0, 0), (0, B_QK_PAD - B_NOPE - B_ROPE))).reshape(B_Q_RANK, B_HEADS * B_QK_PAD)
            wukv = b_w_ukv[j].reshape(B_KV_RANK, B_HEADS, B_NOPE + B_V)
            wuk = wukv[:, :, :B_NOPE].reshape(B_KV_RANK, B_HEADS * B_NOPE)
            wuv = wukv[:, :, B_NOPE:].reshape(B_KV_RANK, B_HEADS * B_V)
            norms = (_row(a_q_norm[j]), _row(a_k_norm[j]), _row(b_q_norm[j]), _row(b_kv_norm[j]),
                     wuq.astype(BF16), wuk.astype(BF16), wuv.astype(BF16))
            qa, ka, va, qb, kb, vb = _inproj_ab(h_lat, g1, mod_lat, w_ab, *norms, tabs_ab, seq, bm_lat)
            qac, kac, vac, qbc, kbc, vbc = _inproj_ab(h_ctx, g1, mod_ctx, w_ab, *norms, None, batch * n_ctx, bm_ctx)
            rep_a = A_HEADS // A_KV_HEADS
            oa = _flash(qa, ka, va, kac, vac, batch, A_KV_HEADS, rep_a, HEAD_DIM, bq_a, bk, sub)
            ob = _flash(qb, kb, vb, kbc, vbc, batch, B_HEADS, 1, B_QK_PAD, bq_b, bk, sub)
            w_out = ab_w_out[j].astype(BF16)
            attn_lat = [oa, ob]
            if need_ctx:
                oac = _flash(qac, kac, vac, None, None, batch, A_KV_HEADS, rep_a, HEAD_DIM, n_ctx, n_ctx, sub)
                obc = _flash(qbc, kbc, vbc, None, None, batch, B_HEADS, 1, B_QK_PAD, n_ctx, n_ctx, sub)
                attn_ctx = [oac, obc]
        else:
            w_cd = cd_w_in[j].astype(BF16)
            norms = (_row(c_q_norm[j]), _row(c_k_norm[j]), _row(d_q_norm[j]), _row(d_k_norm[j]))
            q_l, k_l, v_l = _inproj_cd(h_lat, g1, mod_lat, w_cd, *norms, tabs_cd, seq, bm_lat)
            q_c, k_c, v_c = _inproj_cd(h_ctx, g1, mod_ctx, w_cd, *norms, None, batch * n_ctx, bm_ctx)
            bias_c, bias_d = _band_bias(d_rpb[j], seq)
            sink = jnp.broadcast_to((c_sink[j].astype(F32) * LOG2E)[:, None, None], (C_HEADS, 8, LANES))
            kv_d = D_HEADS * HEAD_DIM
            attn_lat = [
                _band_attention(q_l, k_l, v_l, k_c, v_c, bias_c, sink, batch, 0, kv_d, True),
                _band_attention(q_l, k_l, v_l, k_c, v_c, bias_d, None, batch, C_HEADS * HEAD_DIM, 0, False),
            ]
            w_out = cd_w_out[j].astype(BF16)

        wg, wu, wd = w_gate[i].astype(BF16), w_up[i].astype(BF16), w_down[i].astype(BF16)
        h_lat = _outproj(attn_lat, w_out, g2, mod_lat, h_lat, seq, bm_lat)
        h_lat = _ffn(h_lat, g3, g4, mod_lat, wg, wu, wd, seq, bm_ffn, 512)
        if need_ctx:
            h_ctx = _outproj(attn_ctx, w_out, g2, mod_ctx, h_ctx, batch * n_ctx, bm_ctx)
            h_ctx = _ffn(h_ctx, g3, g4, mod_ctx, wg, wu, wd, batch * n_ctx, bm_ctx, 512)
    return h_lat.reshape(batch, seq, d)
```

```python
import functools

import jax
import jax.numpy as jnp
import numpy as np
from jax import lax
from jax.experimental import pallas as pl
from jax.experimental.pallas import tpu as pltpu

F32 = jnp.float32
BF16 = jnp.bfloat16

D_MODEL = 2048
GRID_W = 64
HEAD_DIM = 128
ROPE_BASE = 10000.0
EPS = 1e-6
NEG = -1e30
LOG2E = 1.4426950408889634

A_HEADS, A_KV_HEADS = 8, 2
B_HEADS, B_Q_RANK, B_KV_RANK, B_NOPE, B_ROPE, B_V = 8, 512, 512, 128, 64, 128
C_HEADS, C_KV_HEADS, C_WINDOW = 8, 2, 128
D_HEADS, NA_KH, NA_KW = 8, 8, 16
A_IN = (A_HEADS + 2 * A_KV_HEADS) * HEAD_DIM
C_IN = (C_HEADS + 2 * C_KV_HEADS) * HEAD_DIM
D_FF = 5632

LANES = 128
MXU_COLS = 256
B_QK_PAD = 256
BAND_TILE = 256
VMEM_LIMIT = 56 * 1024 * 1024
ROW_CHUNK = 128


def _cparams(sem):
    return pltpu.CompilerParams(dimension_semantics=sem, vmem_limit_bytes=VMEM_LIMIT)


def _resident(shape):
    nd = len(shape)
    return pl.BlockSpec(shape, lambda *_: (0,) * nd, pipeline_mode=pl.Buffered(1))


def _silu(x):
    return x * (1.0 / (1.0 + jnp.exp(-x)))


def _rms(x, g):
    ms = jnp.mean(x * x, axis=-1, keepdims=True)
    return x * lax.rsqrt(ms + EPS) * g


def _dot(a, b):
    return jnp.dot(a, b, preferred_element_type=F32)


def _dot_nt(a, b):
    return lax.dot_general(a, b, (((1,), (1,)), ((), ())), preferred_element_type=F32)


def _mod_kernel(c_ref, w_ref, b_ref, o_ref):
    s = _silu(c_ref[...])
    s_hi = s.astype(BF16)
    s_lo = (s - s_hi.astype(F32)).astype(BF16)
    w = w_ref[...]
    w_hi = w.astype(BF16)
    w_lo = (w - w_hi.astype(F32)).astype(BF16)
    o_ref[...] = _dot(s_hi, w_hi) + (_dot(s_lo, w_hi) + _dot(s_hi, w_lo)) + b_ref[...]


def _modulation(cvec, w_mod, b_mod):
    depth, d, n = w_mod.shape
    bn = 1024
    return pl.pallas_call(
        _mod_kernel,
        grid=(depth, n // bn),
        in_specs=[
            pl.BlockSpec((8, d), lambda i, j: (0, 0)),
            pl.BlockSpec((None, d, bn), lambda i, j: (i, 0, j)),
            pl.BlockSpec((None, 1, bn), lambda i, j: (i, 0, j)),
        ],
        out_specs=pl.BlockSpec((None, 8, bn), lambda i, j: (i, 0, j)),
        out_shape=jax.ShapeDtypeStruct((depth, 8, n), F32),
        compiler_params=_cparams(("parallel", "parallel")),
        name="modulation",
    )(cvec, w_mod, b_mod.reshape(depth, 1, n))


def _norm_mod_to(h_ref, g_ref, mod_ref, shift_row, scale_row, u_ref):
    bm = h_ref.shape[0]
    chunk = min(ROW_CHUNK, bm)
    g = g_ref[...]
    sh = mod_ref[shift_row:shift_row + 1, :]
    sc = mod_ref[scale_row:scale_row + 1, :]

    def body(c, carry):
        r = pl.multiple_of(c * chunk, chunk)
        n = _rms(h_ref[pl.ds(r, chunk), :], g)
        u_ref[pl.ds(r, chunk), :] = (n * (1.0 + sc) + sh).astype(BF16)
        return carry

    lax.fori_loop(0, bm // chunk, body, 0)


def _post_residual_to(acc_ref, h_ref, g_ref, mod_ref, gate_row, o_ref):
    bm = h_ref.shape[0]
    chunk = min(ROW_CHUNK, bm)
    g = g_ref[...]
    gate = mod_ref[gate_row:gate_row + 1, :]

    def body(c, carry):
        r = pl.multiple_of(c * chunk, chunk)
        o_ref[pl.ds(r, chunk), :] = h_ref[pl.ds(r, chunk), :] + gate * _rms(acc_ref[pl.ds(r, chunk), :], g)
        return carry

    lax.fori_loop(0, bm // chunk, body, 0)


def _rope(x, cos, sin, quarter):
    lane = lax.broadcasted_iota(jnp.int32, x.shape, 1)
    first = (lane % (2 * quarter)) < quarter
    rot = jnp.where(first, -pltpu.roll(x, LANES - quarter, 1), pltpu.roll(x, quarter, 1))
    return x * cos + rot * sin


def _emit_heads(u_ref, w_ref, col0, n_heads, gain, scale, rope, o_ref, ocol0):
    h = 0
    while h < n_heads:
        nh = min(MXU_COLS // HEAD_DIM, n_heads - h)
        c = col0 + h * HEAD_DIM
        p = _dot(u_ref[...], w_ref[:, c:c + nh * HEAD_DIM])
        for t in range(nh):
            x = p[:, t * HEAD_DIM:(t + 1) * HEAD_DIM]
            if gain is not None:
                x = _rms(x, gain)
                if rope is not None:
                    x = _rope(x, rope[0], rope[1], HEAD_DIM // 4)
                if scale != 1.0:
                    x = x * scale
            oc = ocol0 + (h + t) * HEAD_DIM
            o_ref[:, oc:oc + HEAD_DIM] = x.astype(BF16)
        h += nh


def _inproj_ab_kernel(use_rope, h_ref, g_ref, mod_ref, w_ref, aqn_ref, akn_ref, bqn_ref, bkvn_ref,
                      wuq_ref, wuk_ref, wuv_ref, cosh_ref, sinh_ref, cosr_ref, sinr_ref,
                      qa_ref, ka_ref, va_ref, qb_ref, kb_ref, vb_ref, u_ref, c_ref):
    _norm_mod_to(h_ref, g_ref, mod_ref, 0, 1, u_ref)
    rope_h = (cosh_ref[...], sinh_ref[...]) if use_rope else None
    a_scale = HEAD_DIM ** -0.5 * LOG2E
    _emit_heads(u_ref, w_ref, 0, A_HEADS, aqn_ref[...], a_scale, rope_h, qa_ref, 0)
    _emit_heads(u_ref, w_ref, A_HEADS * HEAD_DIM, A_KV_HEADS, akn_ref[...], 1.0, rope_h, ka_ref, 0)
    _emit_heads(u_ref, w_ref, (A_HEADS + A_KV_HEADS) * HEAD_DIM, A_KV_HEADS, None, 1.0, None, va_ref, 0)

    b_scale = (B_NOPE + B_ROPE) ** -0.5 * LOG2E
    cq = _dot(u_ref[...], w_ref[:, A_IN:A_IN + B_Q_RANK])
    c_ref[...] = _rms(cq, bqn_ref[...]).astype(BF16)
    for hd in range(B_HEADS):
        q = _dot(c_ref[...], wuq_ref[:, hd * B_QK_PAD:(hd + 1) * B_QK_PAD])
        q_nope = q[:, :B_NOPE]
        q_rope = q[:, B_NOPE:]
        if use_rope:
            q_rope = _rope(q_rope, cosr_ref[...], sinr_ref[...], B_ROPE // 4)
        qb_ref[:, hd * B_QK_PAD:hd * B_QK_PAD + B_NOPE] = (q_nope * b_scale).astype(BF16)
        qb_ref[:, hd * B_QK_PAD + B_NOPE:(hd + 1) * B_QK_PAD] = (q_rope * b_scale).astype(BF16)
    kr = _dot(u_ref[...], w_ref[:, A_IN + B_Q_RANK + B_KV_RANK:A_IN + B_Q_RANK + B_KV_RANK + LANES])
    if use_rope:
        kr = _rope(kr, cosr_ref[...], sinr_ref[...], B_ROPE // 4)
    kr = kr.astype(BF16)
    ckv = _dot(u_ref[...], w_ref[:, A_IN + B_Q_RANK:A_IN + B_Q_RANK + B_KV_RANK])
    c_ref[...] = _rms(ckv, bkvn_ref[...]).astype(BF16)
    for pair in range(B_HEADS * B_NOPE // MXU_COLS):
        kn = _dot(c_ref[...], wuk_ref[:, pair * MXU_COLS:(pair + 1) * MXU_COLS])
        for t in range(MXU_COLS // B_NOPE):
            hd = pair * (MXU_COLS // B_NOPE) + t
            kb_ref[:, hd * B_QK_PAD:hd * B_QK_PAD + B_NOPE] = kn[:, t * B_NOPE:(t + 1) * B_NOPE].astype(BF16)
            kb_ref[:, hd * B_QK_PAD + B_NOPE:(hd + 1) * B_QK_PAD] = kr
        vb_ref[:, pair * MXU_COLS:(pair + 1) * MXU_COLS] = _dot(
            c_ref[...], wuv_ref[:, pair * MXU_COLS:(pair + 1) * MXU_COLS]).astype(BF16)


def _inproj_ab(h, g_pre, mod, w, aqn, akn, bqn, bkvn, wuq, wuk, wuv, rope_tabs, rows_per_batch, bm):
    rows, d = h.shape
    use_rope = rope_tabs is not None
    tiles_per_batch = rows_per_batch // bm
    nb = mod.shape[0]

    def row_map(i):
        return (i, 0)

    def mod_map(i):
        return (i // tiles_per_batch if nb > 1 else 0, 0, 0)

    def tab_map(i):
        return (i % tiles_per_batch, 0)

    if use_rope:
        tabs = list(rope_tabs)
    else:
        tabs = [jnp.zeros((bm, LANES), F32)] * 4

        def tab_map(i):
            return (0, 0)

    outs = [
        (A_HEADS * HEAD_DIM), (A_KV_HEADS * HEAD_DIM), (A_KV_HEADS * HEAD_DIM),
        (B_HEADS * B_QK_PAD), (B_HEADS * B_QK_PAD), (B_HEADS * B_V),
    ]
    return pl.pallas_call(
        functools.partial(_inproj_ab_kernel, use_rope),
        grid=(rows // bm,),
        in_specs=[
            pl.BlockSpec((bm, d), row_map),
            _resident((1, d)),
            pl.BlockSpec((None, 8, d), mod_map),
            _resident(w.shape),
            _resident((1, HEAD_DIM)), _resident((1, HEAD_DIM)),
            _resident((1, B_Q_RANK)), _resident((1, B_KV_RANK)),
            _resident(wuq.shape), _resident(wuk.shape), _resident(wuv.shape),
        ] + [pl.BlockSpec((bm, LANES), tab_map)] * 4,
        out_specs=[pl.BlockSpec((bm, n), row_map) for n in outs],
        out_shape=[jax.ShapeDtypeStruct((rows, n), BF16) for n in outs],
        scratch_shapes=[pltpu.VMEM((bm, d), BF16), pltpu.VMEM((bm, B_Q_RANK), BF16)],
        compiler_params=_cparams(("parallel",)),
        name="inproj_ab",
    )(h, g_pre, mod, w, aqn, akn, bqn, bkvn, wuq, wuk, wuv, *tabs)


def _inproj_cd_kernel(use_rope, h_ref, g_ref, mod_ref, w_ref, cqn_ref, ckn_ref, dqn_ref, dkn_ref,
                      cosh_ref, sinh_ref, q_ref, k_ref, v_ref, u_ref):
    _norm_mod_to(h_ref, g_ref, mod_ref, 0, 1, u_ref)
    rope_h = (cosh_ref[...], sinh_ref[...]) if use_rope else None
    scale = HEAD_DIM ** -0.5 * LOG2E
    kv_c = C_KV_HEADS * HEAD_DIM
    kv_d = D_HEADS * HEAD_DIM
    _emit_heads(u_ref, w_ref, 0, C_HEADS, cqn_ref[...], scale, rope_h, q_ref, 0)
    _emit_heads(u_ref, w_ref, C_HEADS * HEAD_DIM, C_KV_HEADS, ckn_ref[...], 1.0, rope_h, k_ref, kv_d)
    _emit_heads(u_ref, w_ref, C_HEADS * HEAD_DIM + kv_c, C_KV_HEADS, None, 1.0, None, v_ref, kv_d)
    _emit_heads(u_ref, w_ref, C_IN, D_HEADS, dqn_ref[...], scale, None, q_ref, C_HEADS * HEAD_DIM)
    _emit_heads(u_ref, w_ref, C_IN + D_HEADS * HEAD_DIM, D_HEADS, dkn_ref[...], 1.0, None, k_ref, 0)
    _emit_heads(u_ref, w_ref, C_IN + 2 * D_HEADS * HEAD_DIM, D_HEADS, None, 1.0, None, v_ref, 0)


def _inproj_cd(h, g_pre, mod, w, cqn, ckn, dqn, dkn, rope_tabs, rows_per_batch, bm):
    rows, d = h.shape
    use_rope = rope_tabs is not None
    tiles_per_batch = rows_per_batch // bm
    nb = mod.shape[0]

    def row_map(i):
        return (i, 0)

    def mod_map(i):
        return (i // tiles_per_batch if nb > 1 else 0, 0, 0)

    if use_rope:
        tabs = list(rope_tabs)

        def tab_map(i):
            return (i % tiles_per_batch, 0)
    else:
        tabs = [jnp.zeros((bm, LANES), F32)] * 2

        def tab_map(i):
            return (0, 0)

    outs = [(C_HEADS + D_HEADS) * HEAD_DIM, (C_KV_HEADS + D_HEADS) * HEAD_DIM, (C_KV_HEADS + D_HEADS) * HEAD_DIM]
    return pl.pallas_call(
        functools.partial(_inproj_cd_kernel, use_rope),
        grid=(rows // bm,),
        in_specs=[
            pl.BlockSpec((bm, d), row_map),
            _resident((1, d)),
            pl.BlockSpec((None, 8, d), mod_map),
            _resident(w.shape),
            _resident((1, HEAD_DIM)), _resident((1, HEAD_DIM)), _resident((1, HEAD_DIM)), _resident((1, HEAD_DIM)),
        ] + [pl.BlockSpec((bm, LANES), tab_map)] * 2,
        out_specs=[pl.BlockSpec((bm, n), row_map) for n in outs],
        out_shape=[jax.ShapeDtypeStruct((rows, n), BF16) for n in outs],
        scratch_shapes=[pltpu.VMEM((bm, d), BF16)],
        compiler_params=_cparams(("parallel",)),
        name="inproj_cd",
    )(h, g_pre, mod, w, cqn, ckn, dqn, dkn, *tabs)


def _flash_step(q_ref, kt, vt, m_ref, acc_ref, rep, dq, sub):
    bq = q_ref.shape[0]
    n_chunks = kt.shape[0] // LANES
    for r in range(rep):
        for sb in range(bq // sub):
            rows = slice(sb * sub, (sb + 1) * sub)
            mcol = slice(r * LANES, (r + 1) * LANES)
            acol = slice(r * 2 * LANES, (r + 1) * 2 * LANES)
            s = _dot_nt(q_ref[rows, r * dq:(r + 1) * dq], kt)
            chunks = [s[:, c * LANES:(c + 1) * LANES] for c in range(n_chunks)]
            lane_max = functools.reduce(jnp.maximum, chunks)
            m_prev = m_ref[rows, mcol]
            m_new = jnp.maximum(m_prev, jnp.max(lane_max, axis=-1, keepdims=True))
            alpha = jnp.exp2(m_prev - m_new)
            p = jnp.concatenate([jnp.exp2(ch - m_new) for ch in chunks], axis=1).astype(BF16)
            acc_ref[rows, acol] = jnp.concatenate([alpha, alpha], axis=1) * acc_ref[rows, acol] + _dot(p, vt)
            m_ref[rows, mcol] = m_new


def _flash_kernel(rep, dq, has_ctx, sub, *refs):
    if has_ctx:
        q_ref, k_ref, v_ref, kc_ref, vc_ref, o_ref, m_ref, acc_ref = refs
    else:
        q_ref, k_ref, v_ref, o_ref, m_ref, acc_ref = refs
    j = pl.program_id(3)

    def with_ones(v):
        return jnp.concatenate([v, jnp.ones(v.shape, BF16)], axis=1)

    @pl.when(j == 0)
    def _init():
        m_ref[...] = jnp.full(m_ref.shape, NEG, F32)
        acc_ref[...] = jnp.zeros(acc_ref.shape, F32)
        if has_ctx:
            _flash_step(q_ref, kc_ref[...], with_ones(vc_ref[...]), m_ref, acc_ref, rep, dq, sub)

    _flash_step(q_ref, k_ref[...], with_ones(v_ref[...]), m_ref, acc_ref, rep, dq, sub)

    @pl.when(j == pl.num_programs(3) - 1)
    def _fin():
        for r in range(rep):
            c0 = r * 2 * LANES
            o_ref[:, r * HEAD_DIM:(r + 1) * HEAD_DIM] = (
                acc_ref[:, c0:c0 + LANES] / acc_ref[:, c0 + LANES:c0 + 2 * LANES]).astype(BF16)


def _flash(q, k, v, kc, vc, batch, n_groups, rep, dq, bq, bk, sub):
    lq = q.shape[0] // batch
    lk = k.shape[0] // batch
    nq, nk = lq // bq, lk // bk
    has_ctx = kc is not None
    dv = HEAD_DIM
    assert dv == LANES
    in_specs = [
        pl.BlockSpec((bq, rep * dq), lambda b, g, i, j: (b * nq + i, g)),
        pl.BlockSpec((bk, dq), lambda b, g, i, j: (b * nk + j, g)),
        pl.BlockSpec((bk, dv), lambda b, g, i, j: (b * nk + j, g)),
    ]
    args = [q, k, v]
    if has_ctx:
        lc = kc.shape[0] // batch
        in_specs += [
            pl.BlockSpec((lc, dq), lambda b, g, i, j: (b, g)),
            pl.BlockSpec((lc, dv), lambda b, g, i, j: (b, g)),
        ]
        args += [kc, vc]
    return pl.pallas_call(
        functools.partial(_flash_kernel, rep, dq, has_ctx, min(sub, bq)),
        grid=(batch, n_groups, nq, nk),
        in_specs=in_specs,
        out_specs=pl.BlockSpec((bq, rep * dv), lambda b, g, i, j: (b * nq + i, g)),
        out_shape=jax.ShapeDtypeStruct((batch * lq, n_groups * rep * dv), BF16),
        scratch_shapes=[
            pltpu.VMEM((bq, rep * LANES), F32),
            pltpu.VMEM((bq, rep * 2 * LANES), F32),
        ],
        compiler_params=_cparams(("parallel", "parallel", "parallel", "arbitrary")),
        name="flash_attention",
    )(*args)


def _band_kernel(rep, shared, has_sink, *refs):
    if has_sink:
        q_ref, kp_ref, kx_ref, kn_ref, vp_ref, vx_ref, vn_ref, kc_ref, vc_ref, bias_ref, sink_ref, o_ref = refs
    else:
        q_ref, kp_ref, kx_ref, kn_ref, vp_ref, vx_ref, vn_ref, kc_ref, vc_ref, bias_ref, o_ref = refs
    t = BAND_TILE
    for r in range(rep):
        hcol = slice(r * HEAD_DIM, (r + 1) * HEAD_DIM)
        kcol = slice(0, HEAD_DIM) if shared else hcol
        bias = bias_ref if shared else bias_ref.at[r]
        q = q_ref[:, hcol]
        scores = [_dot_nt(q, kc_ref[:, kcol])]
        for n, k_ref in enumerate((kp_ref, kx_ref, kn_ref)):
            scores.append(_dot_nt(q, k_ref[:, kcol]) + bias[:, n * t:(n + 1) * t])
        chunks = [s[:, c * LANES:(c + 1) * LANES] for s in scores for c in range(s.shape[1] // LANES)]
        m = jnp.max(functools.reduce(jnp.maximum, chunks), axis=-1, keepdims=True)
        if has_sink:
            sink = sink_ref[r, 0:1, 0:1]
            m = jnp.maximum(m, sink)
        probs = [jnp.exp2(s - m) for s in scores]
        pch = [p[:, c * LANES:(c + 1) * LANES] for p in probs for c in range(p.shape[1] // LANES)]
        l = jnp.sum(functools.reduce(jnp.add, pch), axis=-1, keepdims=True)
        if has_sink:
            l = l + jnp.exp2(sink - m)
        acc = _dot(probs[0].astype(BF16), vc_ref[:, kcol])
        for p, v_ref in zip(probs[1:], (vp_ref, vx_ref, vn_ref)):
            acc = acc + _dot(p.astype(BF16), v_ref[:, kcol])
        o_ref[:, hcol] = (acc / l).astype(BF16)


def _band_attention(q, k, v, kc, vc, bias, sink, batch, q_col0, kv_col0, shared):
    t = BAND_TILE
    rep = 4
    seq = q.shape[0] // batch
    nq = seq // t
    lc = kc.shape[0] // batch
    qw = rep * HEAD_DIM
    kw = HEAD_DIM if shared else qw
    n_groups = 2
    has_sink = sink is not None

    def tile_map(off):
        def f(g, b, i):
            return (b * nq + jnp.clip(i + off, 0, nq - 1), kv_col0 // kw + g)
        return f

    def cls(i):
        return jnp.where(i == 0, 0, jnp.where(i == nq - 1, 2, 1))

    kv_spec = [pl.BlockSpec((t, kw), tile_map(off)) for off in (-1, 0, 1)]
    ctx_spec = pl.BlockSpec((lc, kw), lambda g, b, i: (b, kv_col0 // kw + g))
    if shared:
        bias_spec = pl.BlockSpec((None, t, 3 * t), lambda g, b, i: (cls(i), 0, 0))
    else:
        bias_spec = pl.BlockSpec((rep, None, t, 3 * t), lambda g, b, i: (g, cls(i), 0, 0))
    in_specs = [pl.BlockSpec((t, qw), lambda g, b, i: (b * nq + i, q_col0 // qw + g))]
    in_specs += kv_spec + kv_spec + [ctx_spec, ctx_spec, bias_spec]
    args = [q, k, k, k, v, v, v, kc, vc, bias]
    if has_sink:
        in_specs.append(pl.BlockSpec((rep, 8, LANES), lambda g, b, i: (g, 0, 0)))
        args.append(sink)
    return pl.pallas_call(
        functools.partial(_band_kernel, rep, shared, has_sink),
        grid=(n_groups, batch, nq),
        in_specs=in_specs,
        out_specs=pl.BlockSpec((t, qw), lambda g, b, i: (b * nq + i, g)),
        out_shape=jax.ShapeDtypeStruct((batch * seq, n_groups * qw), BF16),
        compiler_params=_cparams(("parallel", "parallel", "parallel")),
        name="band_attention",
    )(*args)


def _outproj_kernel(n_in, *refs):
    a_refs = refs[:n_in]
    w_ref, g_ref, mod_ref, h_ref, o_ref, acc_ref = refs[n_in:]
    k0 = 0
    acc = None
    for a_ref in a_refs:
        kk = a_ref.shape[1]
        part = _dot(a_ref[...], w_ref[k0:k0 + kk, :])
        acc = part if acc is None else acc + part
        k0 += kk
    acc_ref[...] = acc
    _post_residual_to(acc_ref, h_ref, g_ref, mod_ref, 2, o_ref)


def _outproj(a_list, w, g_post, mod, h, rows_per_batch, bm):
    rows, d = h.shape
    tiles_per_batch = rows_per_batch // bm
    nb = mod.shape[0]

    def row_map(i):
        return (i, 0)

    def mod_map(i):
        return (i // tiles_per_batch if nb > 1 else 0, 0, 0)

    return pl.pallas_call(
        functools.partial(_outproj_kernel, len(a_list)),
        grid=(rows // bm,),
        in_specs=[pl.BlockSpec((bm, a.shape[1]), row_map) for a in a_list] + [
            _resident(w.shape),
            _resident((1, d)),
            pl.BlockSpec((None, 8, d), mod_map),
            pl.BlockSpec((bm, d), row_map),
        ],
        out_specs=pl.BlockSpec((bm, d), row_map),
        out_shape=jax.ShapeDtypeStruct((rows, d), F32),
        scratch_shapes=[pltpu.VMEM((bm, d), F32)],
        compiler_params=_cparams(("parallel",)),
        name="outproj",
    )(*a_list, w, g_post, mod, h)


def _ffn_kernel(h_ref, gpre_ref, gpost_ref, mod_ref, wg_ref, wu_ref, wd_ref, o_ref, u_ref):
    f = pl.program_id(1)

    @pl.when(f == 0)
    def _pro():
        _norm_mod_to(h_ref, gpre_ref, mod_ref, 3, 4, u_ref)
        o_ref[...] = jnp.zeros(o_ref.shape, F32)

    u = u_ref[...]
    act = (_silu(_dot(u, wg_ref[...])) * _dot(u, wu_ref[...])).astype(BF16)
    o_ref[...] += _dot(act, wd_ref[...])

    @pl.when(f == pl.num_programs(1) - 1)
    def _epi():
        _post_residual_to(o_ref, h_ref, gpost_ref, mod_ref, 5, o_ref)


def _ffn(h, g_pre, g_post, mod, wg, wu, wd, rows_per_batch, bm, bf):
    rows, d = h.shape
    dff = wg.shape[1]
    tiles_per_batch = rows_per_batch // bm
    nb = mod.shape[0]

    def row_map(i, f):
        return (i, 0)

    def mod_map(i, f):
        return (i // tiles_per_batch if nb > 1 else 0, 0, 0)

    assert rows % bm == 0 and dff % bf == 0
    return pl.pallas_call(
        _ffn_kernel,
        grid=(rows // bm, dff // bf),
        in_specs=[
            pl.BlockSpec((bm, d), row_map, pipeline_mode=pl.Buffered(1)),
            _resident((1, d)), _resident((1, d)),
            pl.BlockSpec((None, 8, d), mod_map),
            pl.BlockSpec((d, bf), lambda i, f: (0, f)),
            pl.BlockSpec((d, bf), lambda i, f: (0, f)),
            pl.BlockSpec((bf, d), lambda i, f: (f, 0)),
        ],
        out_specs=pl.BlockSpec((bm, d), row_map),
        out_shape=jax.ShapeDtypeStruct((rows, d), F32),
        scratch_shapes=[pltpu.VMEM((bm, d), BF16)],
        compiler_params=_cparams(("parallel", "arbitrary")),
        name="ffn",
    )(h, g_pre, g_post, mod, wg, wu, wd)


def _rope_table(n_tok, dim):
    t = jnp.arange(n_tok, dtype=jnp.int32)
    row = (t // GRID_W).astype(F32)
    col = (t % GRID_W).astype(F32)
    half = dim // 2
    inv_freq = ROPE_BASE ** (-jnp.arange(0, half, 2, dtype=F32) / half)
    ar = row[:, None] * inv_freq[None, :]
    ac = col[:, None] * inv_freq[None, :]
    ang = jnp.concatenate([ar, ar, ac, ac], axis=-1)
    return jnp.cos(ang), jnp.sin(ang)


def _rpb_cols_kernel(r_ref, e_ref, o_ref):
    x = r_ref[...]
    hi = x.astype(BF16)
    r1 = x - hi.astype(F32)
    mid = r1.astype(BF16)
    lo = (r1 - mid.astype(F32)).astype(BF16)
    e = e_ref[...]
    o_ref[...] = (_dot(hi, e) + _dot(mid, e)) + _dot(lo, e)


def _band_bias(rpb, seq):
    t, g = BAND_TILE, GRID_W
    rpt = t // g
    nq = seq // t
    rows = seq // g
    kh = min(NA_KH, rows)
    n_h, n_di, n_dj = rpb.shape
    tiles = (0, 1, nq - 1)

    qpos = np.array(tiles)[:, None, None] * t + np.arange(t)[None, :, None]
    kpos = (np.array(tiles)[:, None, None] - 1) * t + np.arange(3 * t)[None, None, :]
    win = (np.abs(kpos - qpos) <= C_WINDOW) & (kpos >= 0) & (kpos < seq)
    bias_c = jnp.asarray(np.where(win, 0.0, NEG).astype(np.float32))

    qc = np.arange(g)[:, None]
    kc = np.arange(g)[None, :]
    cs = np.clip(qc - NA_KW // 2, 0, g - NA_KW)
    col_ok = (kc >= cs) & (kc < cs + NA_KW)
    onehot = np.zeros((LANES, g, g), np.float32)
    qq, kk = np.nonzero(col_ok)
    onehot[(kk - qq + NA_KW - 1), qq, kk] = 1.0
    assert n_h * n_di <= LANES and n_dj <= LANES
    rpb2d = jnp.pad(rpb.reshape(n_h * n_di, n_dj).astype(F32), ((0, LANES - n_h * n_di), (0, LANES - n_dj)))
    cols = pl.pallas_call(
        _rpb_cols_kernel,
        out_shape=jax.ShapeDtypeStruct((LANES, g * g), F32),
        name="rpb_columns",
    )(rpb2d, jnp.asarray(onehot.reshape(LANES, g * g), BF16))
    cols = cols[:n_h * n_di].reshape(n_h, n_di, g, g) + jnp.asarray(np.where(col_ok, 0.0, NEG).astype(np.float32))
    ext = jnp.concatenate([cols, jnp.full((n_h, 1, g, g), NEG, F32)], axis=1)

    blocks = []
    for ti in tiles:
        for a in range(rpt):
            qr = ti * rpt + a
            rs = min(max(qr - kh // 2, 0), rows - kh)
            for b in range(3 * rpt):
                kr = (ti - 1) * rpt + b
                blocks.append(ext[:, kr - qr + NA_KH - 1] if rs <= kr < rs + kh else ext[:, n_di])
    bias_d = jnp.stack(blocks, axis=1).reshape(n_h, len(tiles), rpt, 3 * rpt, g, g)
    bias_d = bias_d.transpose(0, 1, 2, 4, 3, 5).reshape(n_h, len(tiles), t, 3 * t)
    return bias_c * LOG2E, bias_d * LOG2E


def _pad_mod(m):
    nb = m.shape[0]
    return jnp.pad(m.reshape(nb, 6, D_MODEL), ((0, 0), (0, 2), (0, 0)))


def _row(v):
    return v.reshape(1, -1).astype(F32)


def kernel(x, c, ctx, c_ctx, w_mod, b_mod, g_mix_pre, g_mix_post, g_ffn_pre, g_ffn_post, w_gate, w_up, w_down,
           ab_w_in, ab_w_out, a_q_norm, a_k_norm, b_q_norm, b_kv_norm, b_w_uq, b_w_ukv, cd_w_in, cd_w_out,
           c_q_norm, c_k_norm, c_sink, d_q_norm, d_k_norm, d_rpb):
    batch, seq, d = x.shape
    n_ctx = ctx.shape[1]
    depth = w_mod.shape[0]
    assert d == D_MODEL and seq % 512 == 0 and n_ctx % 256 == 0 and batch + 1 <= 8
    assert depth == 2

    cvec = jnp.zeros((8, d), F32).at[:batch].set(c).at[batch].set(c_ctx)
    mod_all = _modulation(cvec, w_mod, b_mod)

    cos_h, sin_h = _rope_table(seq, HEAD_DIM)
    cos_r, sin_r = _rope_table(seq, B_ROPE)
    pad_r = ((0, 0), (0, LANES - B_ROPE))
    tabs_ab = (cos_h, sin_h, jnp.pad(cos_r, pad_r), jnp.pad(sin_r, pad_r))
    tabs_cd = (cos_h, sin_h)

    bm_lat = 512
    bm_ctx = min(512, batch * n_ctx)
    bm_ffn = 1024 if seq % 1024 == 0 else 512
    bq_a = min(1024, seq)
    bq_b = min(2048, seq)
    bk = min(2048, seq)
    sub = 256

    h_lat = x.reshape(batch * seq, d)
    h_ctx = ctx.reshape(batch * n_ctx, d)

    for i in range(depth):
        last = i == depth - 1
        need_ctx = not last
        j = i // 2
        mod_lat = _pad_mod(mod_all[i, :batch])
        mod_ctx = _pad_mod(mod_all[i, batch:batch + 1])
        g1, g2, g3, g4 = _row(g_mix_pre[i]), _row(g_mix_post[i]), _row(g_ffn_pre[i]), _row(g_ffn_post[i])

        if i % 2 == 0:
            w_in = ab_w_in[j]
            kr_cols = jnp.pad(w_in[:, A_IN + B_Q_RANK + B_KV_RANK:], ((0, 0), (0, LANES - B_ROPE)))
            w_ab = jnp.concatenate([w_in[:, :A_IN + B_Q_RANK + B_KV_RANK], kr_cols], axis=1).astype(BF16)
            wuq = jnp.pad(b_w_uq[j].reshape(B_Q_RANK, B_HEADS, B_NOPE + B_ROPE),
                          ((0, 0), (0, 0), (0, B_QK_PAD - B_NOPE - B_ROPE))).reshape(B_Q_RANK, B_HEADS * B_QK_PAD)
            wukv = b_w_ukv[j].reshape(B_KV_RANK, B_HEADS, B_NOPE + B_V)
            wuk = wukv[:, :, :B_NOPE].reshape(B_KV_RANK, B_HEADS * B_NOPE)
            wuv = wukv[:, :, B_NOPE:].reshape(B_KV_RANK, B_HEADS * B_V)
            norms = (_row(a_q_norm[j]), _row(a_k_norm[j]), _row(b_q_norm[j]), _row(b_kv_norm[j]),
                     wuq.astype(BF16), wuk.astype(BF16), wuv.astype(BF16))
            qa, ka, va, qb, kb, vb = _inproj_ab(h_lat, g1, mod_lat, w_ab, *norms, tabs_ab, seq, bm_lat)
            qac, kac, vac, qbc, kbc, vbc = _inproj_ab(h_ctx, g1, mod_ctx, w_ab, *norms, None, batch * n_ctx, bm_ctx)
            rep_a = A_HEADS // A_KV_HEADS
            oa = _flash(qa, ka, va, kac, vac, batch, A_KV_HEADS, rep_a, HEAD_DIM, bq_a, bk, sub)
            ob = _flash(qb, kb, vb, kbc, vbc, batch, B_HEADS, 1, B_QK_PAD, bq_b, bk, sub)
            w_out = ab_w_out[j].astype(BF16)
            attn_lat = [oa, ob]
            if need_ctx:
                oac = _flash(qac, kac, vac, None, None, batch, A_KV_HEADS, rep_a, HEAD_DIM, n_ctx, n_ctx, sub)
                obc = _flash(qbc, kbc, vbc, None, None, batch, B_HEADS, 1, B_QK_PAD, n_ctx, n_ctx, sub)
                attn_ctx = [oac, obc]
        else:
            w_cd = cd_w_in[j].astype(BF16)
            norms = (_row(c_q_norm[j]), _row(c_k_norm[j]), _row(d_q_norm[j]), _row(d_k_norm[j]))
            q_l, k_l, v_l = _inproj_cd(h_lat, g1, mod_lat, w_cd, *norms, tabs_cd, seq, bm_lat)
            q_c, k_c, v_c = _inproj_cd(h_ctx, g1, mod_ctx, w_cd, *norms, None, batch * n_ctx, bm_ctx)
            bias_c, bias_d = _band_bias(d_rpb[j], seq)
            sink = jnp.broadcast_to((c_sink[j].astype(F32) * LOG2E)[:, None, None], (C_HEADS, 8, LANES))
            kv_d = D_HEADS * HEAD_DIM
            attn_lat = [
                _band_attention(q_l, k_l, v_l, k_c, v_c, bias_c, sink, batch, 0, kv_d, True),
                _band_attention(q_l, k_l, v_l, k_c, v_c, bias_d, None, batch, C_HEADS * HEAD_DIM, 0, False),
            ]
            w_out = cd_w_out[j].astype(BF16)

        wg, wu, wd = w_gate[i].astype(BF16), w_up[i].astype(BF16), w_down[i].astype(BF16)
        h_lat = _outproj(attn_lat, w_out, g2, mod_lat, h_lat, seq, bm_lat)
        h_lat = _ffn(h_lat, g3, g4, mod_lat, wg, wu, wd, seq, bm_ffn, 512)
        if need_ctx:
            h_ctx = _outproj(attn_ctx, w_out, g2, mod_ctx, h_ctx, batch * n_ctx, bm_ctx)
            h_ctx = _ffn(h_ctx, g3, g4, mod_ctx, wg, wu, wd, batch * n_ctx, bm_ctx, 512)
    return h_lat.reshape(batch, seq, d)
```

```python
import functools

import jax
import jax.numpy as jnp
import numpy as np
from jax import lax
from jax.experimental import pallas as pl
from jax.experimental.pallas import tpu as pltpu

F32 = jnp.float32
BF16 = jnp.bfloat16

D_MODEL = 2048
GRID_W = 64
HEAD_DIM = 128
ROPE_BASE = 10000.0
EPS = 1e-6
NEG = -1e30
LOG2E = 1.4426950408889634

A_HEADS, A_KV_HEADS = 8, 2
B_HEADS, B_Q_RANK, B_KV_RANK, B_NOPE, B_ROPE, B_V = 8, 512, 512, 128, 64, 128
C_HEADS, C_KV_HEADS, C_WINDOW = 8, 2, 128
D_HEADS, NA_KH, NA_KW = 8, 8, 16
A_IN = (A_HEADS + 2 * A_KV_HEADS) * HEAD_DIM
C_IN = (C_HEADS + 2 * C_KV_HEADS) * HEAD_DIM
D_FF = 5632

LANES = 128
MXU_COLS = 256
B_QK_PAD = 256
BAND_TILE = 256
VMEM_LIMIT = 56 * 1024 * 1024
ROW_CHUNK = 32
CHUNK_UNROLL = 4


def _cparams(sem):
    return pltpu.CompilerParams(dimension_semantics=sem, vmem_limit_bytes=VMEM_LIMIT)


def _resident(shape):
    nd = len(shape)
    return pl.BlockSpec(shape, lambda *_: (0,) * nd, pipeline_mode=pl.Buffered(1))


def _silu(x):
    return x * (1.0 / (1.0 + jnp.exp(-x)))


def _rms(x, g):
    ms = jnp.mean(x * x, axis=-1, keepdims=True)
    return x * lax.rsqrt(ms + EPS) * g


def _dot(a, b):
    return jnp.dot(a, b, preferred_element_type=F32)


def _dot_nt(a, b):
    return lax.dot_general(a, b, (((1,), (1,)), ((), ())), preferred_element_type=F32)


def _mod_kernel(c_ref, w_ref, b_ref, o_ref):
    s = _silu(c_ref[...])
    s_hi = s.astype(BF16)
    s_lo = (s - s_hi.astype(F32)).astype(BF16)
    w = w_ref[...]
    w_hi = w.astype(BF16)
    w_lo = (w - w_hi.astype(F32)).astype(BF16)
    o_ref[...] = _dot(s_hi, w_hi) + (_dot(s_lo, w_hi) + _dot(s_hi, w_lo)) + b_ref[...]


def _modulation(cvec, w_mod, b_mod):
    depth, d, n = w_mod.shape
    bn = 1024
    return pl.pallas_call(
        _mod_kernel,
        grid=(depth, n // bn),
        in_specs=[
            pl.BlockSpec((8, d), lambda i, j: (0, 0)),
            pl.BlockSpec((None, d, bn), lambda i, j: (i, 0, j)),
            pl.BlockSpec((None, 1, bn), lambda i, j: (i, 0, j)),
        ],
        out_specs=pl.BlockSpec((None, 8, bn), lambda i, j: (i, 0, j)),
        out_shape=jax.ShapeDtypeStruct((depth, 8, n), F32),
        compiler_params=_cparams(("parallel", "parallel")),
        name="modulation",
    )(cvec, w_mod, b_mod.reshape(depth, 1, n))


def _norm_mod_to(h_ref, g_ref, mod_ref, shift_row, scale_row, u_ref):
    bm = h_ref.shape[0]
    chunk = min(ROW_CHUNK, bm)
    gs = g_ref[...] * (1.0 + mod_ref[scale_row:scale_row + 1, :])
    sh = mod_ref[shift_row:shift_row + 1, :]

    def body(c, carry):
        r = pl.multiple_of(c * chunk, chunk)
        h = h_ref[pl.ds(r, chunk), :]
        inv = lax.rsqrt(jnp.mean(h * h, axis=-1, keepdims=True) + EPS)
        u_ref[pl.ds(r, chunk), :] = ((h * inv) * gs + sh).astype(BF16)
        return carry

    lax.fori_loop(0, bm // chunk, body, 0, unroll=CHUNK_UNROLL)


def _post_residual_to(acc_ref, h_ref, g_ref, mod_ref, gate_row, o_ref):
    bm = h_ref.shape[0]
    chunk = min(ROW_CHUNK, bm)
    gg = g_ref[...] * mod_ref[gate_row:gate_row + 1, :]

    def body(c, carry):
        r = pl.multiple_of(c * chunk, chunk)
        a = acc_ref[pl.ds(r, chunk), :]
        inv = lax.rsqrt(jnp.mean(a * a, axis=-1, keepdims=True) + EPS)
        o_ref[pl.ds(r, chunk), :] = h_ref[pl.ds(r, chunk), :] + (a * inv) * gg
        return carry

    lax.fori_loop(0, bm // chunk, body, 0, unroll=CHUNK_UNROLL)


def _rope(x, cos, sin, quarter):
    lane = lax.broadcasted_iota(jnp.int32, x.shape, 1)
    first = (lane % (2 * quarter)) < quarter
    rot = jnp.where(first, -pltpu.roll(x, LANES - quarter, 1), pltpu.roll(x, quarter, 1))
    return x * cos + rot * sin


def _emit_heads(u_ref, w_ref, col0, n_heads, gain, scale, rope, o_ref, ocol0):
    h = 0
    while h < n_heads:
        nh = min(MXU_COLS // HEAD_DIM, n_heads - h)
        c = col0 + h * HEAD_DIM
        p = _dot(u_ref[...], w_ref[:, c:c + nh * HEAD_DIM])
        for t in range(nh):
            x = p[:, t * HEAD_DIM:(t + 1) * HEAD_DIM]
            if gain is not None:
                x = _rms(x, gain)
                if rope is not None:
                    x = _rope(x, rope[0], rope[1], HEAD_DIM // 4)
                if scale != 1.0:
                    x = x * scale
            oc = ocol0 + (h + t) * HEAD_DIM
            o_ref[:, oc:oc + HEAD_DIM] = x.astype(BF16)
        h += nh


def _inproj_ab_kernel(use_rope, h_ref, g_ref, mod_ref, w_ref, aqn_ref, akn_ref, bqn_ref, bkvn_ref,
                      wuq_ref, wuk_ref, wuv_ref, cosh_ref, sinh_ref, cosr_ref, sinr_ref,
                      qa_ref, ka_ref, va_ref, qb_ref, kb_ref, vb_ref, u_ref, c_ref):
    _norm_mod_to(h_ref, g_ref, mod_ref, 0, 1, u_ref)
    rope_h = (cosh_ref[...], sinh_ref[...]) if use_rope else None
    a_scale = HEAD_DIM ** -0.5 * LOG2E
    _emit_heads(u_ref, w_ref, 0, A_HEADS, aqn_ref[...], a_scale, rope_h, qa_ref, 0)
    _emit_heads(u_ref, w_ref, A_HEADS * HEAD_DIM, A_KV_HEADS, akn_ref[...], 1.0, rope_h, ka_ref, 0)
    _emit_heads(u_ref, w_ref, (A_HEADS + A_KV_HEADS) * HEAD_DIM, A_KV_HEADS, None, 1.0, None, va_ref, 0)

    b_scale = (B_NOPE + B_ROPE) ** -0.5 * LOG2E
    cq = _dot(u_ref[...], w_ref[:, A_IN:A_IN + B_Q_RANK])
    c_ref[...] = _rms(cq, bqn_ref[...]).astype(BF16)
    for hd in range(B_HEADS):
        q = _dot(c_ref[...], wuq_ref[:, hd * B_QK_PAD:(hd + 1) * B_QK_PAD])
        q_nope = q[:, :B_NOPE]
        q_rope = q[:, B_NOPE:]
        if use_rope:
            q_rope = _rope(q_rope, cosr_ref[...], sinr_ref[...], B_ROPE // 4)
        qb_ref[:, hd * B_QK_PAD:hd * B_QK_PAD + B_NOPE] = (q_nope * b_scale).astype(BF16)
        qb_ref[:, hd * B_QK_PAD + B_NOPE:(hd + 1) * B_QK_PAD] = (q_rope * b_scale).astype(BF16)
    kr = _dot(u_ref[...], w_ref[:, A_IN + B_Q_RANK + B_KV_RANK:A_IN + B_Q_RANK + B_KV_RANK + LANES])
    if use_rope:
        kr = _rope(kr, cosr_ref[...], sinr_ref[...], B_ROPE // 4)
    kr = kr.astype(BF16)
    ckv = _dot(u_ref[...], w_ref[:, A_IN + B_Q_RANK:A_IN + B_Q_RANK + B_KV_RANK])
    c_ref[...] = _rms(ckv, bkvn_ref[...]).astype(BF16)
    for pair in range(B_HEADS * B_NOPE // MXU_COLS):
        kn = _dot(c_ref[...], wuk_ref[:, pair * MXU_COLS:(pair + 1) * MXU_COLS])
        for t in range(MXU_COLS // B_NOPE):
            hd = pair * (MXU_COLS // B_NOPE) + t
            kb_ref[:, hd * B_QK_PAD:hd * B_QK_PAD + B_NOPE] = kn[:, t * B_NOPE:(t + 1) * B_NOPE].astype(BF16)
            kb_ref[:, hd * B_QK_PAD + B_NOPE:(hd + 1) * B_QK_PAD] = kr
        vb_ref[:, pair * MXU_COLS:(pair + 1) * MXU_COLS] = _dot(
            c_ref[...], wuv_ref[:, pair * MXU_COLS:(pair + 1) * MXU_COLS]).astype(BF16)


def _inproj_ab(h, g_pre, mod, w, aqn, akn, bqn, bkvn, wuq, wuk, wuv, rope_tabs, rows_per_batch, bm):
    rows, d = h.shape
    use_rope = rope_tabs is not None
    tiles_per_batch = rows_per_batch // bm
    nb = mod.shape[0]

    def row_map(i):
        return (i, 0)

    def mod_map(i):
        return (i // tiles_per_batch if nb > 1 else 0, 0, 0)

    def tab_map(i):
        return (i % tiles_per_batch, 0)

    if use_rope:
        tabs = list(rope_tabs)
    else:
        tabs = [jnp.zeros((bm, LANES), F32)] * 4

        def tab_map(i):
            return (0, 0)

    outs = [
        (A_HEADS * HEAD_DIM), (A_KV_HEADS * HEAD_DIM), (A_KV_HEADS * HEAD_DIM),
        (B_HEADS * B_QK_PAD), (B_HEADS * B_QK_PAD), (B_HEADS * B_V),
    ]
    return pl.pallas_call(
        functools.partial(_inproj_ab_kernel, use_rope),
        grid=(rows // bm,),
        in_specs=[
            pl.BlockSpec((bm, d), row_map),
            _resident((1, d)),
            pl.BlockSpec((None, 8, d), mod_map),
            _resident(w.shape),
            _resident((1, HEAD_DIM)), _resident((1, HEAD_DIM)),
            _resident((1, B_Q_RANK)), _resident((1, B_KV_RANK)),
            _resident(wuq.shape), _resident(wuk.shape), _resident(wuv.shape),
        ] + [pl.BlockSpec((bm, LANES), tab_map)] * 4,
        out_specs=[pl.BlockSpec((bm, n), row_map) for n in outs],
        out_shape=[jax.ShapeDtypeStruct((rows, n), BF16) for n in outs],
        scratch_shapes=[pltpu.VMEM((bm, d), BF16), pltpu.VMEM((bm, B_Q_RANK), BF16)],
        compiler_params=_cparams(("parallel",)),
        name="inproj_ab",
    )(h, g_pre, mod, w, aqn, akn, bqn, bkvn, wuq, wuk, wuv, *tabs)


def _inproj_cd_kernel(use_rope, h_ref, g_ref, mod_ref, w_ref, cqn_ref, ckn_ref, dqn_ref, dkn_ref,
                      cosh_ref, sinh_ref, q_ref, k_ref, v_ref, u_ref):
    _norm_mod_to(h_ref, g_ref, mod_ref, 0, 1, u_ref)
    rope_h = (cosh_ref[...], sinh_ref[...]) if use_rope else None
    scale = HEAD_DIM ** -0.5 * LOG2E
    kv_c = C_KV_HEADS * HEAD_DIM
    kv_d = D_HEADS * HEAD_DIM
    _emit_heads(u_ref, w_ref, 0, C_HEADS, cqn_ref[...], scale, rope_h, q_ref, 0)
    _emit_heads(u_ref, w_ref, C_HEADS * HEAD_DIM, C_KV_HEADS, ckn_ref[...], 1.0, rope_h, k_ref, kv_d)
    _emit_heads(u_ref, w_ref, C_HEADS * HEAD_DIM + kv_c, C_KV_HEADS, None, 1.0, None, v_ref, kv_d)
    _emit_heads(u_ref, w_ref, C_IN, D_HEADS, dqn_ref[...], scale, None, q_ref, C_HEADS * HEAD_DIM)
    _emit_heads(u_ref, w_ref, C_IN + D_HEADS * HEAD_DIM, D_HEADS, dkn_ref[...], 1.0, None, k_ref, 0)
    _emit_heads(u_ref, w_ref, C_IN + 2 * D_HEADS * HEAD_DIM, D_HEADS, None, 1.0, None, v_ref, 0)


def _inproj_cd(h, g_pre, mod, w, cqn, ckn, dqn, dkn, rope_tabs, rows_per_batch, bm):
    rows, d = h.shape
    use_rope = rope_tabs is not None
    tiles_per_batch = rows_per_batch // bm
    nb = mod.shape[0]

    def row_map(i):
        return (i, 0)

    def mod_map(i):
        return (i // tiles_per_batch if nb > 1 else 0, 0, 0)

    if use_rope:
        tabs = list(rope_tabs)

        def tab_map(i):
            return (i % tiles_per_batch, 0)
    else:
        tabs = [jnp.zeros((bm, LANES), F32)] * 2

        def tab_map(i):
            return (0, 0)

    outs = [(C_HEADS + D_HEADS) * HEAD_DIM, (C_KV_HEADS + D_HEADS) * HEAD_DIM, (C_KV_HEADS + D_HEADS) * HEAD_DIM]
    return pl.pallas_call(
        functools.partial(_inproj_cd_kernel, use_rope),
        grid=(rows // bm,),
        in_specs=[
            pl.BlockSpec((bm, d), row_map),
            _resident((1, d)),
            pl.BlockSpec((None, 8, d), mod_map),
            _resident(w.shape),
            _resident((1, HEAD_DIM)), _resident((1, HEAD_DIM)), _resident((1, HEAD_DIM)), _resident((1, HEAD_DIM)),
        ] + [pl.BlockSpec((bm, LANES), tab_map)] * 2,
        out_specs=[pl.BlockSpec((bm, n), row_map) for n in outs],
        out_shape=[jax.ShapeDtypeStruct((rows, n), BF16) for n in outs],
        scratch_shapes=[pltpu.VMEM((bm, d), BF16)],
        compiler_params=_cparams(("parallel",)),
        name="inproj_cd",
    )(h, g_pre, mod, w, cqn, ckn, dqn, dkn, *tabs)


def _flash_step(q_ref, kt, vt, m_ref, acc_ref, rep, dq, sub):
    bq = q_ref.shape[0]
    n_chunks = kt.shape[0] // LANES
    for r in range(rep):
        for sb in range(bq // sub):
            rows = slice(sb * sub, (sb + 1) * sub)
            mcol = slice(r * LANES, (r + 1) * LANES)
            acol = slice(r * 2 * LANES, (r + 1) * 2 * LANES)
            s = _dot_nt(q_ref[rows, r * dq:(r + 1) * dq], kt)
            chunks = [s[:, c * LANES:(c + 1) * LANES] for c in range(n_chunks)]
            lane_max = functools.reduce(jnp.maximum, chunks)
            m_prev = m_ref[rows, mcol]
            m_new = jnp.maximum(m_prev, jnp.max(lane_max, axis=-1, keepdims=True))
            alpha = jnp.exp2(m_prev - m_new)
            p = jnp.concatenate([jnp.exp2(ch - m_new) for ch in chunks], axis=1).astype(BF16)
            acc_ref[rows, acol] = jnp.concatenate([alpha, alpha], axis=1) * acc_ref[rows, acol] + _dot(p, vt)
            m_ref[rows, mcol] = m_new


def _flash_kernel(rep, dq, has_ctx, sub, *refs):
    if has_ctx:
        q_ref, k_ref, v_ref, kc_ref, vc_ref, o_ref, m_ref, acc_ref = refs
    else:
        q_ref, k_ref, v_ref, o_ref, m_ref, acc_ref = refs
    j = pl.program_id(3)

    def with_ones(v):
        return jnp.concatenate([v, jnp.ones(v.shape, BF16)], axis=1)

    @pl.when(j == 0)
    def _init():
        m_ref[...] = jnp.full(m_ref.shape, NEG, F32)
        acc_ref[...] = jnp.zeros(acc_ref.shape, F32)
        if has_ctx:
            _flash_step(q_ref, kc_ref[...], with_ones(vc_ref[...]), m_ref, acc_ref, rep, dq, sub)

    _flash_step(q_ref, k_ref[...], with_ones(v_ref[...]), m_ref, acc_ref, rep, dq, sub)

    @pl.when(j == pl.num_programs(3) - 1)
    def _fin():
        for r in range(rep):
            c0 = r * 2 * LANES
            o_ref[:, r * HEAD_DIM:(r + 1) * HEAD_DIM] = (
                acc_ref[:, c0:c0 + LANES] / acc_ref[:, c0 + LANES:c0 + 2 * LANES]).astype(BF16)


def _flash(q, k, v, kc, vc, batch, n_groups, rep, dq, bq, bk, sub):
    lq = q.shape[0] // batch
    lk = k.shape[0] // batch
    nq, nk = lq // bq, lk // bk
    has_ctx = kc is not None
    dv = HEAD_DIM
    assert dv == LANES
    in_specs = [
        pl.BlockSpec((bq, rep * dq), lambda b, g, i, j: (b * nq + i, g)),
        pl.BlockSpec((bk, dq), lambda b, g, i, j: (b * nk + j, g)),
        pl.BlockSpec((bk, dv), lambda b, g, i, j: (b * nk + j, g)),
    ]
    args = [q, k, v]
    if has_ctx:
        lc = kc.shape[0] // batch
        in_specs += [
            pl.BlockSpec((lc, dq), lambda b, g, i, j: (b, g)),
            pl.BlockSpec((lc, dv), lambda b, g, i, j: (b, g)),
        ]
        args += [kc, vc]
    return pl.pallas_call(
        functools.partial(_flash_kernel, rep, dq, has_ctx, min(sub, bq)),
        grid=(batch, n_groups, nq, nk),
        in_specs=in_specs,
        out_specs=pl.BlockSpec((bq, rep * dv), lambda b, g, i, j: (b * nq + i, g)),
        out_shape=jax.ShapeDtypeStruct((batch * lq, n_groups * rep * dv), BF16),
        scratch_shapes=[
            pltpu.VMEM((bq, rep * LANES), F32),
            pltpu.VMEM((bq, rep * 2 * LANES), F32),
        ],
        compiler_params=_cparams(("parallel", "parallel", "parallel", "arbitrary")),
        name="flash_attention",
    )(*args)


def _band_kernel(rep, shared, has_sink, *refs):
    if has_sink:
        q_ref, kp_ref, kx_ref, kn_ref, vp_ref, vx_ref, vn_ref, kc_ref, vc_ref, bias_ref, sink_ref, o_ref = refs
    else:
        q_ref, kp_ref, kx_ref, kn_ref, vp_ref, vx_ref, vn_ref, kc_ref, vc_ref, bias_ref, o_ref = refs
    t = BAND_TILE
    for r in range(rep):
        hcol = slice(r * HEAD_DIM, (r + 1) * HEAD_DIM)
        kcol = slice(0, HEAD_DIM) if shared else hcol
        bias = bias_ref if shared else bias_ref.at[r]
        q = q_ref[:, hcol]
        scores = [_dot_nt(q, kc_ref[:, kcol])]
        for n, k_ref in enumerate((kp_ref, kx_ref, kn_ref)):
            scores.append(_dot_nt(q, k_ref[:, kcol]) + bias[:, n * t:(n + 1) * t])
        chunks = [s[:, c * LANES:(c + 1) * LANES] for s in scores for c in range(s.shape[1] // LANES)]
        m = jnp.max(functools.reduce(jnp.maximum, chunks), axis=-1, keepdims=True)
        if has_sink:
            sink = sink_ref[r, 0:1, 0:1]
            m = jnp.maximum(m, sink)
        probs = [jnp.exp2(s - m) for s in scores]
        pch = [p[:, c * LANES:(c + 1) * LANES] for p in probs for c in range(p.shape[1] // LANES)]
        l = jnp.sum(functools.reduce(jnp.add, pch), axis=-1, keepdims=True)
        if has_sink:
            l = l + jnp.exp2(sink - m)
        acc = _dot(probs[0].astype(BF16), vc_ref[:, kcol])
        for p, v_ref in zip(probs[1:], (vp_ref, vx_ref, vn_ref)):
            acc = acc + _dot(p.astype(BF16), v_ref[:, kcol])
        o_ref[:, hcol] = (acc / l).astype(BF16)


def _band_attention(q, k, v, kc, vc, bias, sink, batch, q_col0, kv_col0, shared):
    t = BAND_TILE
    rep = 4
    seq = q.shape[0] // batch
    nq = seq // t
    lc = kc.shape[0] // batch
    qw = rep * HEAD_DIM
    kw = HEAD_DIM if shared else qw
    n_groups = 2
    has_sink = sink is not None

    def tile_map(off):
        def f(g, b, i):
            return (b * nq + jnp.clip(i + off, 0, nq - 1), kv_col0 // kw + g)
        return f

    def cls(i):
        return jnp.where(i == 0, 0, jnp.where(i == nq - 1, 2, 1))

    kv_spec = [pl.BlockSpec((t, kw), tile_map(off)) for off in (-1, 0, 1)]
    ctx_spec = pl.BlockSpec((lc, kw), lambda g, b, i: (b, kv_col0 // kw + g))
    if shared:
        bias_spec = pl.BlockSpec((None, t, 3 * t), lambda g, b, i: (cls(i), 0, 0))
    else:
        bias_spec = pl.BlockSpec((rep, None, t, 3 * t), lambda g, b, i: (g, cls(i), 0, 0))
    in_specs = [pl.BlockSpec((t, qw), lambda g, b, i: (b * nq + i, q_col0 // qw + g))]
    in_specs += kv_spec + kv_spec + [ctx_spec, ctx_spec, bias_spec]
    args = [q, k, k, k, v, v, v, kc, vc, bias]
    if has_sink:
        in_specs.append(pl.BlockSpec((rep, 8, LANES), lambda g, b, i: (g, 0, 0)))
        args.append(sink)
    return pl.pallas_call(
        functools.partial(_band_kernel, rep, shared, has_sink),
        grid=(n_groups, batch, nq),
        in_specs=in_specs,
        out_specs=pl.BlockSpec((t, qw), lambda g, b, i: (b * nq + i, g)),
        out_shape=jax.ShapeDtypeStruct((batch * seq, n_groups * qw), BF16),
        compiler_params=_cparams(("parallel", "parallel", "parallel")),
        name="band_attention",
    )(*args)


def _outproj_kernel(n_in, *refs):
    a_refs = refs[:n_in]
    w_ref, g_ref, mod_ref, h_ref, o_ref, acc_ref = refs[n_in:]
    k0 = 0
    acc = None
    for a_ref in a_refs:
        kk = a_ref.shape[1]
        part = _dot(a_ref[...], w_ref[k0:k0 + kk, :])
        acc = part if acc is None else acc + part
        k0 += kk
    acc_ref[...] = acc
    _post_residual_to(acc_ref, h_ref, g_ref, mod_ref, 2, o_ref)


def _outproj(a_list, w, g_post, mod, h, rows_per_batch, bm):
    rows, d = h.shape
    tiles_per_batch = rows_per_batch // bm
    nb = mod.shape[0]

    def row_map(i):
        return (i, 0)

    def mod_map(i):
        return (i // tiles_per_batch if nb > 1 else 0, 0, 0)

    return pl.pallas_call(
        functools.partial(_outproj_kernel, len(a_list)),
        grid=(rows // bm,),
        in_specs=[pl.BlockSpec((bm, a.shape[1]), row_map) for a in a_list] + [
            _resident(w.shape),
            _resident((1, d)),
            pl.BlockSpec((None, 8, d), mod_map),
            pl.BlockSpec((bm, d), row_map),
        ],
        out_specs=pl.BlockSpec((bm, d), row_map),
        out_shape=jax.ShapeDtypeStruct((rows, d), F32),
        scratch_shapes=[pltpu.VMEM((bm, d), F32)],
        compiler_params=_cparams(("parallel",)),
        name="outproj",
    )(*a_list, w, g_post, mod, h)


def _ffn_kernel(h_ref, gpre_ref, gpost_ref, mod_ref, wg_ref, wu_ref, wd_ref, o_ref, u_ref):
    f = pl.program_id(1)

    @pl.when(f == 0)
    def _pro():
        _norm_mod_to(h_ref, gpre_ref, mod_ref, 3, 4, u_ref)
        o_ref[...] = jnp.zeros(o_ref.shape, F32)

    u = u_ref[...]
    act = (_silu(_dot(u, wg_ref[...])) * _dot(u, wu_ref[...])).astype(BF16)
    o_ref[...] += _dot(act, wd_ref[...])

    @pl.when(f == pl.num_programs(1) - 1)
    def _epi():
        _post_residual_to(o_ref, h_ref, gpost_ref, mod_ref, 5, o_ref)


def _ffn(h, g_pre, g_post, mod, wg, wu, wd, rows_per_batch, bm, bf):
    rows, d = h.shape
    dff = wg.shape[1]
    tiles_per_batch = rows_per_batch // bm
    nb = mod.shape[0]

    def row_map(i, f):
        return (i, 0)

    def mod_map(i, f):
        return (i // tiles_per_batch if nb > 1 else 0, 0, 0)

    assert rows % bm == 0 and dff % bf == 0
    return pl.pallas_call(
        _ffn_kernel,
        grid=(rows // bm, dff // bf),
        in_specs=[
            pl.BlockSpec((bm, d), row_map),
            _resident((1, d)), _resident((1, d)),
            pl.BlockSpec((None, 8, d), mod_map),
            pl.BlockSpec((d, bf), lambda i, f: (0, f)),
            pl.BlockSpec((d, bf), lambda i, f: (0, f)),
            pl.BlockSpec((bf, d), lambda i, f: (f, 0)),
        ],
        out_specs=pl.BlockSpec((bm, d), row_map),
        out_shape=jax.ShapeDtypeStruct((rows, d), F32),
        scratch_shapes=[pltpu.VMEM((bm, d), BF16)],
        compiler_params=_cparams(("parallel", "arbitrary")),
        name="ffn",
    )(h, g_pre, g_post, mod, wg, wu, wd)


def _rope_table(n_tok, dim):
    t = jnp.arange(n_tok, dtype=jnp.int32)
    row = (t // GRID_W).astype(F32)
    col = (t % GRID_W).astype(F32)
    half = dim // 2
    inv_freq = ROPE_BASE ** (-jnp.arange(0, half, 2, dtype=F32) / half)
    ar = row[:, None] * inv_freq[None, :]
    ac = col[:, None] * inv_freq[None, :]
    ang = jnp.concatenate([ar, ar, ac, ac], axis=-1)
    return jnp.cos(ang), jnp.sin(ang)


def _rpb_cols_kernel(r_ref, e_ref, o_ref):
    x = r_ref[...]
    hi = x.astype(BF16)
    r1 = x - hi.astype(F32)
    mid = r1.astype(BF16)
    lo = (r1 - mid.astype(F32)).astype(BF16)
    e = e_ref[...]
    o_ref[...] = (_dot(hi, e) + _dot(mid, e)) + _dot(lo, e)


def _band_bias(rpb, seq):
    t, g = BAND_TILE, GRID_W
    rpt = t // g
    nq = seq // t
    rows = seq // g
    kh = min(NA_KH, rows)
    n_h, n_di, n_dj = rpb.shape
    tiles = (0, 1, nq - 1)

    qpos = np.array(tiles)[:, None, None] * t + np.arange(t)[None, :, None]
    kpos = (np.array(tiles)[:, None, None] - 1) * t + np.arange(3 * t)[None, None, :]
    win = (np.abs(kpos - qpos) <= C_WINDOW) & (kpos >= 0) & (kpos < seq)
    bias_c = jnp.asarray(np.where(win, 0.0, NEG).astype(np.float32))

    qc = np.arange(g)[:, None]
    kc = np.arange(g)[None, :]
    cs = np.clip(qc - NA_KW // 2, 0, g - NA_KW)
    col_ok = (kc >= cs) & (kc < cs + NA_KW)
    onehot = np.zeros((LANES, g, g), np.float32)
    qq, kk = np.nonzero(col_ok)
    onehot[(kk - qq + NA_KW - 1), qq, kk] = 1.0
    assert n_h * n_di <= LANES and n_dj <= LANES
    rpb2d = jnp.pad(rpb.reshape(n_h * n_di, n_dj).astype(F32), ((0, LANES - n_h * n_di), (0, LANES - n_dj)))
    cols = pl.pallas_call(
        _rpb_cols_kernel,
        out_shape=jax.ShapeDtypeStruct((LANES, g * g), F32),
        name="rpb_columns",
    )(rpb2d, jnp.asarray(onehot.reshape(LANES, g * g), BF16))
    cols = cols[:n_h * n_di].reshape(n_h, n_di, g, g) + jnp.asarray(np.where(col_ok, 0.0, NEG).astype(np.float32))
    ext = jnp.concatenate([cols, jnp.full((n_h, 1, g, g), NEG, F32)], axis=1)

    blocks = []
    for ti in tiles:
        for a in range(rpt):
            qr = ti * rpt + a
            rs = min(max(qr - kh // 2, 0), rows - kh)
            for b in range(3 * rpt):
                kr = (ti - 1) * rpt + b
                blocks.append(ext[:, kr - qr + NA_KH - 1] if rs <= kr < rs + kh else ext[:, n_di])
    bias_d = jnp.stack(blocks, axis=1).reshape(n_h, len(tiles), rpt, 3 * rpt, g, g)
    bias_d = bias_d.transpose(0, 1, 2, 4, 3, 5).reshape(n_h, len(tiles), t, 3 * t)
    return bias_c * LOG2E, bias_d * LOG2E


def _pad_mod(m):
    nb = m.shape[0]
    return jnp.pad(m.reshape(nb, 6, D_MODEL), ((0, 0), (0, 2), (0, 0)))


def _row(v):
    return v.reshape(1, -1).astype(F32)


def kernel(x, c, ctx, c_ctx, w_mod, b_mod, g_mix_pre, g_mix_post, g_ffn_pre, g_ffn_post, w_gate, w_up, w_down,
           ab_w_in, ab_w_out, a_q_norm, a_k_norm, b_q_norm, b_kv_norm, b_w_uq, b_w_ukv, cd_w_in, cd_w_out,
           c_q_norm, c_k_norm, c_sink, d_q_norm, d_k_norm, d_rpb):
    batch, seq, d = x.shape
    n_ctx = ctx.shape[1]
    depth = w_mod.shape[0]
    assert d == D_MODEL and seq % 512 == 0 and n_ctx % 256 == 0 and batch + 1 <= 8
    assert depth == 2

    cvec = jnp.zeros((8, d), F32).at[:batch].set(c).at[batch].set(c_ctx)
    mod_all = _modulation(cvec, w_mod, b_mod)

    cos_h, sin_h = _rope_table(seq, HEAD_DIM)
    cos_r, sin_r = _rope_table(seq, B_ROPE)
    pad_r = ((0, 0), (0, LANES - B_ROPE))
    tabs_ab = (cos_h, sin_h, jnp.pad(cos_r, pad_r), jnp.pad(sin_r, pad_r))
    tabs_cd = (cos_h, sin_h)

    bm_lat = 512
    bm_ctx = min(512, batch * n_ctx)
    bm_ffn = 1024 if seq % 1024 == 0 else 512
    bq_a = min(1024, seq)
    bq_b = min(4096, seq)
    bk = min(2048, seq)
    sub = 256

    h_lat = x.reshape(batch * seq, d)
    h_ctx = ctx.reshape(batch * n_ctx, d)

    for i in range(depth):
        last = i == depth - 1
        need_ctx = not last
        j = i // 2
        mod_lat = _pad_mod(mod_all[i, :batch])
        mod_ctx = _pad_mod(mod_all[i, batch:batch + 1])
        g1, g2, g3, g4 = _row(g_mix_pre[i]), _row(g_mix_post[i]), _row(g_ffn_pre[i]), _row(g_ffn_post[i])

        if i % 2 == 0:
            w_in = ab_w_in[j]
            kr_cols = jnp.pad(w_in[:, A_IN + B_Q_RANK + B_KV_RANK:], ((0, 0), (0, LANES - B_ROPE)))
            w_ab = jnp.concatenate([w_in[:, :A_IN + B_Q_RANK + B_KV_RANK], kr_cols], axis=1).astype(BF16)
            wuq = jnp.pad(b_w_uq[j].reshape(B_Q_RANK, B_HEADS, B_NOPE + B_ROPE),
                          ((0, 0), (0, 0), (0, B_QK_PAD - B_NOPE - B_ROPE))).reshape(B_Q_RANK, B_HEADS * B_QK_PAD)
            wukv = b_w_ukv[j].reshape(B_KV_RANK, B_HEADS, B_NOPE + B_V)
            wuk = wukv[:, :, :B_NOPE].reshape(B_KV_RANK, B_HEADS * B_NOPE)
            wuv = wukv[:, :, B_NOPE:].reshape(B_KV_RANK, B_HEADS * B_V)
            norms = (_row(a_q_norm[j]), _row(a_k_norm[j]), _row(b_q_norm[j]), _row(b_kv_norm[j]),
                     wuq.astype(BF16), wuk.astype(BF16), wuv.astype(BF16))
            qa, ka, va, qb, kb, vb = _inproj_ab(h_lat, g1, mod_lat, w_ab, *norms, tabs_ab, seq, bm_lat)
            qac, kac, vac, qbc, kbc, vbc = _inproj_ab(h_ctx, g1, mod_ctx, w_ab, *norms, None, batch * n_ctx, bm_ctx)
            rep_a = A_HEADS // A_KV_HEADS
            oa = _flash(qa, ka, va, kac, vac, batch, A_KV_HEADS, rep_a, HEAD_DIM, bq_a, bk, sub)
            ob = _flash(qb, kb, vb, kbc, vbc, batch, B_HEADS, 1, B_QK_PAD, bq_b, bk, sub)
            w_out = ab_w_out[j].astype(BF16)
            attn_lat = [oa, ob]
            if need_ctx:
                oac = _flash(qac, kac, vac, None, None, batch, A_KV_HEADS, rep_a, HEAD_DIM, n_ctx, n_ctx, sub)
                obc = _flash(qbc, kbc, vbc, None, None, batch, B_HEADS, 1, B_QK_PAD, n_ctx, n_ctx, sub)
                attn_ctx = [oac, obc]
        else:
            w_cd = cd_w_in[j].astype(BF16)
            norms = (_row(c_q_norm[j]), _row(c_k_norm[j]), _row(d_q_norm[j]), _row(d_k_norm[j]))
            q_l, k_l, v_l = _inproj_cd(h_lat, g1, mod_lat, w_cd, *norms, tabs_cd, seq, bm_lat)
            q_c, k_c, v_c = _inproj_cd(h_ctx, g1, mod_ctx, w_cd, *norms, None, batch * n_ctx, bm_ctx)
            bias_c, bias_d = _band_bias(d_rpb[j], seq)
            sink = jnp.broadcast_to((c_sink[j].astype(F32) * LOG2E)[:, None, None], (C_HEADS, 8, LANES))
            kv_d = D_HEADS * HEAD_DIM
            attn_lat = [
                _band_attention(q_l, k_l, v_l, k_c, v_c, bias_c, sink, batch, 0, kv_d, True),
                _band_attention(q_l, k_l, v_l, k_c, v_c, bias_d, None, batch, C_HEADS * HEAD_DIM, 0, False),
            ]
            w_out = cd_w_out[j].astype(BF16)

        wg, wu, wd = w_gate[i].astype(BF16), w_up[i].astype(BF16), w_down[i].astype(BF16)
        h_lat = _outproj(attn_lat, w_out, g2, mod_lat, h_lat, seq, bm_lat)
        h_lat = _ffn(h_lat, g3, g4, mod_lat, wg, wu, wd, seq, bm_ffn, 512)
        if need_ctx:
            h_ctx = _outproj(attn_ctx, w_out, g2, mod_ctx, h_ctx, batch * n_ctx, bm_ctx)
            h_ctx = _ffn(h_ctx, g3, g4, mod_ctx, wg, wu, wd, batch * n_ctx, bm_ctx, 512)
    return h_lat.reshape(batch, seq, d)
```

```python
import functools

import jax
import jax.numpy as jnp
import numpy as np
from jax import lax
from jax.experimental import pallas as pl
from jax.experimental.pallas import tpu as pltpu

F32 = jnp.float32
BF16 = jnp.bfloat16

D_MODEL = 2048
GRID_W = 64
HEAD_DIM = 128
ROPE_BASE = 10000.0
EPS = 1e-6
NEG = -1e30
LOG2E = 1.4426950408889634

A_HEADS, A_KV_HEADS = 8, 2
B_HEADS, B_Q_RANK, B_KV_RANK, B_NOPE, B_ROPE, B_V = 8, 512, 512, 128, 64, 128
C_HEADS, C_KV_HEADS, C_WINDOW = 8, 2, 128
D_HEADS, NA_KH, NA_KW = 8, 8, 16
A_IN = (A_HEADS + 2 * A_KV_HEADS) * HEAD_DIM
C_IN = (C_HEADS + 2 * C_KV_HEADS) * HEAD_DIM
D_FF = 5632

LANES = 128
MXU_COLS = 256
B_QK_PAD = 256
BAND_TILE = 256
VMEM_LIMIT = 56 * 1024 * 1024
ROW_CHUNK = 32
CHUNK_UNROLL = 4


def _cparams(sem):
    return pltpu.CompilerParams(dimension_semantics=sem, vmem_limit_bytes=VMEM_LIMIT)


def _resident(shape):
    nd = len(shape)
    return pl.BlockSpec(shape, lambda *_: (0,) * nd, pipeline_mode=pl.Buffered(1))


def _silu(x):
    return x * (1.0 / (1.0 + jnp.exp(-x)))


def _rms(x, g):
    ms = jnp.mean(x * x, axis=-1, keepdims=True)
    return x * lax.rsqrt(ms + EPS) * g


def _dot(a, b):
    return jnp.dot(a, b, preferred_element_type=F32)


def _dot_nt(a, b):
    return lax.dot_general(a, b, (((1,), (1,)), ((), ())), preferred_element_type=F32)


def _mod_kernel(c_ref, w_ref, b_ref, o_ref):
    s = _silu(c_ref[...])
    s_hi = s.astype(BF16)
    s_lo = (s - s_hi.astype(F32)).astype(BF16)
    w = w_ref[...]
    w_hi = w.astype(BF16)
    w_lo = (w - w_hi.astype(F32)).astype(BF16)
    o_ref[...] = _dot(s_hi, w_hi) + (_dot(s_lo, w_hi) + _dot(s_hi, w_lo)) + b_ref[...]


def _modulation(cvec, w_mod, b_mod):
    depth, d, n = w_mod.shape
    bn = 1024
    return pl.pallas_call(
        _mod_kernel,
        grid=(depth, n // bn),
        in_specs=[
            pl.BlockSpec((8, d), lambda i, j: (0, 0)),
            pl.BlockSpec((None, d, bn), lambda i, j: (i, 0, j)),
            pl.BlockSpec((None, 1, bn), lambda i, j: (i, 0, j)),
        ],
        out_specs=pl.BlockSpec((None, 8, bn), lambda i, j: (i, 0, j)),
        out_shape=jax.ShapeDtypeStruct((depth, 8, n), F32),
        compiler_params=_cparams(("parallel", "parallel")),
        name="modulation",
    )(cvec, w_mod, b_mod.reshape(depth, 1, n))


def _norm_mod_to(h_ref, g_ref, mod_ref, shift_row, scale_row, u_ref):
    bm = h_ref.shape[0]
    chunk = min(ROW_CHUNK, bm)
    gs = g_ref[...] * (1.0 + mod_ref[scale_row:scale_row + 1, :])
    sh = mod_ref[shift_row:shift_row + 1, :]

    def body(c, carry):
        r = pl.multiple_of(c * chunk, chunk)
        h = h_ref[pl.ds(r, chunk), :]
        inv = lax.rsqrt(jnp.mean(h * h, axis=-1, keepdims=True) + EPS)
        u_ref[pl.ds(r, chunk), :] = ((h * inv) * gs + sh).astype(BF16)
        return carry

    lax.fori_loop(0, bm // chunk, body, 0, unroll=CHUNK_UNROLL)


def _post_residual_to(acc_ref, h_ref, g_ref, mod_ref, gate_row, o_ref):
    bm = h_ref.shape[0]
    chunk = min(ROW_CHUNK, bm)
    gg = g_ref[...] * mod_ref[gate_row:gate_row + 1, :]

    def body(c, carry):
        r = pl.multiple_of(c * chunk, chunk)
        a = acc_ref[pl.ds(r, chunk), :]
        inv = lax.rsqrt(jnp.mean(a * a, axis=-1, keepdims=True) + EPS)
        o_ref[pl.ds(r, chunk), :] = h_ref[pl.ds(r, chunk), :] + (a * inv) * gg
        return carry

    lax.fori_loop(0, bm // chunk, body, 0, unroll=CHUNK_UNROLL)


def _rope(x, cos, sin, quarter):
    lane = lax.broadcasted_iota(jnp.int32, x.shape, 1)
    first = (lane % (2 * quarter)) < quarter
    rot = jnp.where(first, -pltpu.roll(x, LANES - quarter, 1), pltpu.roll(x, quarter, 1))
    return x * cos + rot * sin


def _emit_heads(u_ref, w_ref, col0, n_heads, gain, scale, rope, o_ref, ocol0):
    h = 0
    while h < n_heads:
        nh = min(MXU_COLS // HEAD_DIM, n_heads - h)
        c = col0 + h * HEAD_DIM
        p = _dot(u_ref[...], w_ref[:, c:c + nh * HEAD_DIM])
        for t in range(nh):
            x = p[:, t * HEAD_DIM:(t + 1) * HEAD_DIM]
            if gain is not None:
                x = _rms(x, gain)
                if rope is not None:
                    x = _rope(x, rope[0], rope[1], HEAD_DIM // 4)
                if scale != 1.0:
                    x = x * scale
            oc = ocol0 + (h + t) * HEAD_DIM
            o_ref[:, oc:oc + HEAD_DIM] = x.astype(BF16)
        h += nh


def _inproj_ab_kernel(use_rope, h_ref, g_ref, mod_ref, w_ref, aqn_ref, akn_ref, bqn_ref, bkvn_ref,
                      wuq_ref, wuk_ref, wuv_ref, cosh_ref, sinh_ref, cosr_ref, sinr_ref,
                      qa_ref, ka_ref, va_ref, qb_ref, kb_ref, vb_ref, u_ref, c_ref):
    _norm_mod_to(h_ref, g_ref, mod_ref, 0, 1, u_ref)
    rope_h = (cosh_ref[...], sinh_ref[...]) if use_rope else None
    a_scale = HEAD_DIM ** -0.5 * LOG2E
    _emit_heads(u_ref, w_ref, 0, A_HEADS, aqn_ref[...], a_scale, rope_h, qa_ref, 0)
    _emit_heads(u_ref, w_ref, A_HEADS * HEAD_DIM, A_KV_HEADS, akn_ref[...], 1.0, rope_h, ka_ref, 0)
    _emit_heads(u_ref, w_ref, (A_HEADS + A_KV_HEADS) * HEAD_DIM, A_KV_HEADS, None, 1.0, None, va_ref, 0)

    b_scale = (B_NOPE + B_ROPE) ** -0.5 * LOG2E
    cq = _dot(u_ref[...], w_ref[:, A_IN:A_IN + B_Q_RANK])
    c_ref[...] = _rms(cq, bqn_ref[...]).astype(BF16)
    for hd in range(B_HEADS):
        q = _dot(c_ref[...], wuq_ref[:, hd * B_QK_PAD:(hd + 1) * B_QK_PAD])
        q_nope = q[:, :B_NOPE]
        q_rope = q[:, B_NOPE:]
        if use_rope:
            q_rope = _rope(q_rope, cosr_ref[...], sinr_ref[...], B_ROPE // 4)
        qb_ref[:, hd * B_QK_PAD:hd * B_QK_PAD + B_NOPE] = (q_nope * b_scale).astype(BF16)
        qb_ref[:, hd * B_QK_PAD + B_NOPE:(hd + 1) * B_QK_PAD] = (q_rope * b_scale).astype(BF16)
    kr = _dot(u_ref[...], w_ref[:, A_IN + B_Q_RANK + B_KV_RANK:A_IN + B_Q_RANK + B_KV_RANK + LANES])
    if use_rope:
        kr = _rope(kr, cosr_ref[...], sinr_ref[...], B_ROPE // 4)
    kr = kr.astype(BF16)
    ckv = _dot(u_ref[...], w_ref[:, A_IN + B_Q_RANK:A_IN + B_Q_RANK + B_KV_RANK])
    c_ref[...] = _rms(ckv, bkvn_ref[...]).astype(BF16)
    for pair in range(B_HEADS * B_NOPE // MXU_COLS):
        kn = _dot(c_ref[...], wuk_ref[:, pair * MXU_COLS:(pair + 1) * MXU_COLS])
        for t in range(MXU_COLS // B_NOPE):
            hd = pair * (MXU_COLS // B_NOPE) + t
            kb_ref[:, hd * B_QK_PAD:hd * B_QK_PAD + B_NOPE] = kn[:, t * B_NOPE:(t + 1) * B_NOPE].astype(BF16)
            kb_ref[:, hd * B_QK_PAD + B_NOPE:(hd + 1) * B_QK_PAD] = kr
        vb_ref[:, pair * MXU_COLS:(pair + 1) * MXU_COLS] = _dot(
            c_ref[...], wuv_ref[:, pair * MXU_COLS:(pair + 1) * MXU_COLS]).astype(BF16)


def _inproj_ab(h, g_pre, mod, w, aqn, akn, bqn, bkvn, wuq, wuk, wuv, rope_tabs, rows_per_batch, bm):
    rows, d = h.shape
    use_rope = rope_tabs is not None
    tiles_per_batch = rows_per_batch // bm
    nb = mod.shape[0]

    def row_map(i):
        return (i, 0)

    def mod_map(i):
        return (i // tiles_per_batch if nb > 1 else 0, 0, 0)

    def tab_map(i):
        return (i % tiles_per_batch, 0)

    if use_rope:
        tabs = list(rope_tabs)
    else:
        tabs = [jnp.zeros((bm, LANES), F32)] * 4

        def tab_map(i):
            return (0, 0)

    outs = [
        (A_HEADS * HEAD_DIM), (A_KV_HEADS * HEAD_DIM), (A_KV_HEADS * HEAD_DIM),
        (B_HEADS * B_QK_PAD), (B_HEADS * B_QK_PAD), (B_HEADS * B_V),
    ]
    return pl.pallas_call(
        functools.partial(_inproj_ab_kernel, use_rope),
        grid=(rows // bm,),
        in_specs=[
            pl.BlockSpec((bm, d), row_map),
            _resident((1, d)),
            pl.BlockSpec((None, 8, d), mod_map),
            _resident(w.shape),
            _resident((1, HEAD_DIM)), _resident((1, HEAD_DIM)),
            _resident((1, B_Q_RANK)), _resident((1, B_KV_RANK)),
            _resident(wuq.shape), _resident(wuk.shape), _resident(wuv.shape),
        ] + [pl.BlockSpec((bm, LANES), tab_map)] * 4,
        out_specs=[pl.BlockSpec((bm, n), row_map) for n in outs],
        out_shape=[jax.ShapeDtypeStruct((rows, n), BF16) for n in outs],
        scratch_shapes=[pltpu.VMEM((bm, d), BF16), pltpu.VMEM((bm, B_Q_RANK), BF16)],
        compiler_params=_cparams(("parallel",)),
        name="inproj_ab",
    )(h, g_pre, mod, w, aqn, akn, bqn, bkvn, wuq, wuk, wuv, *tabs)


def _inproj_cd_kernel(use_rope, h_ref, g_ref, mod_ref, w_ref, cqn_ref, ckn_ref, dqn_ref, dkn_ref,
                      cosh_ref, sinh_ref, q_ref, k_ref, v_ref, u_ref):
    _norm_mod_to(h_ref, g_ref, mod_ref, 0, 1, u_ref)
    rope_h = (cosh_ref[...], sinh_ref[...]) if use_rope else None
    scale = HEAD_DIM ** -0.5 * LOG2E
    kv_c = C_KV_HEADS * HEAD_DIM
    kv_d = D_HEADS * HEAD_DIM
    _emit_heads(u_ref, w_ref, 0, C_HEADS, cqn_ref[...], scale, rope_h, q_ref, 0)
    _emit_heads(u_ref, w_ref, C_HEADS * HEAD_DIM, C_KV_HEADS, ckn_ref[...], 1.0, rope_h, k_ref, kv_d)
    _emit_heads(u_ref, w_ref, C_HEADS * HEAD_DIM + kv_c, C_KV_HEADS, None, 1.0, None, v_ref, kv_d)
    _emit_heads(u_ref, w_ref, C_IN, D_HEADS, dqn_ref[...], scale, None, q_ref, C_HEADS * HEAD_DIM)
    _emit_heads(u_ref, w_ref, C_IN + D_HEADS * HEAD_DIM, D_HEADS, dkn_ref[...], 1.0, None, k_ref, 0)
    _emit_heads(u_ref, w_ref, C_IN + 2 * D_HEADS * HEAD_DIM, D_HEADS, None, 1.0, None, v_ref, 0)


def _inproj_cd(h, g_pre, mod, w, cqn, ckn, dqn, dkn, rope_tabs, rows_per_batch, bm):
    rows, d = h.shape
    use_rope = rope_tabs is not None
    tiles_per_batch = rows_per_batch // bm
    nb = mod.shape[0]

    def row_map(i):
        return (i, 0)

    def mod_map(i):
        return (i // tiles_per_batch if nb > 1 else 0, 0, 0)

    if use_rope:
        tabs = list(rope_tabs)

        def tab_map(i):
            return (i % tiles_per_batch, 0)
    else:
        tabs = [jnp.zeros((bm, LANES), F32)] * 2

        def tab_map(i):
            return (0, 0)

    outs = [(C_HEADS + D_HEADS) * HEAD_DIM, (C_KV_HEADS + D_HEADS) * HEAD_DIM, (C_KV_HEADS + D_HEADS) * HEAD_DIM]
    return pl.pallas_call(
        functools.partial(_inproj_cd_kernel, use_rope),
        grid=(rows // bm,),
        in_specs=[
            pl.BlockSpec((bm, d), row_map),
            _resident((1, d)),
            pl.BlockSpec((None, 8, d), mod_map),
            _resident(w.shape),
            _resident((1, HEAD_DIM)), _resident((1, HEAD_DIM)), _resident((1, HEAD_DIM)), _resident((1, HEAD_DIM)),
        ] + [pl.BlockSpec((bm, LANES), tab_map)] * 2,
        out_specs=[pl.BlockSpec((bm, n), row_map) for n in outs],
        out_shape=[jax.ShapeDtypeStruct((rows, n), BF16) for n in outs],
        scratch_shapes=[pltpu.VMEM((bm, d), BF16)],
        compiler_params=_cparams(("parallel",)),
        name="inproj_cd",
    )(h, g_pre, mod, w, cqn, ckn, dqn, dkn, *tabs)


def _flash_step(q_ref, kt, vt, m_ref, acc_ref, rep, dq, sub):
    bq = q_ref.shape[0]
    n_chunks = kt.shape[0] // LANES
    for r in range(rep):
        for sb in range(bq // sub):
            rows = slice(sb * sub, (sb + 1) * sub)
            mcol = slice(r * LANES, (r + 1) * LANES)
            acol = slice(r * 2 * LANES, (r + 1) * 2 * LANES)
            s = _dot_nt(q_ref[rows, r * dq:(r + 1) * dq], kt)
            chunks = [s[:, c * LANES:(c + 1) * LANES] for c in range(n_chunks)]
            lane_max = functools.reduce(jnp.maximum, chunks)
            m_prev = m_ref[rows, mcol]
            m_new = jnp.maximum(m_prev, jnp.max(lane_max, axis=-1, keepdims=True))
            alpha = jnp.exp2(m_prev - m_new)
            p = jnp.concatenate([jnp.exp2(ch - m_new) for ch in chunks], axis=1).astype(BF16)
            acc_ref[rows, acol] = jnp.concatenate([alpha, alpha], axis=1) * acc_ref[rows, acol] + _dot(p, vt)
            m_ref[rows, mcol] = m_new


def _flash_kernel(rep, dq, has_ctx, sub, *refs):
    if has_ctx:
        q_ref, k_ref, v_ref, kc_ref, vc_ref, o_ref, m_ref, acc_ref = refs
    else:
        q_ref, k_ref, v_ref, o_ref, m_ref, acc_ref = refs
    j = pl.program_id(3)

    def with_ones(v):
        return jnp.concatenate([v, jnp.ones(v.shape, BF16)], axis=1)

    @pl.when(j == 0)
    def _init():
        m_ref[...] = jnp.full(m_ref.shape, NEG, F32)
        acc_ref[...] = jnp.zeros(acc_ref.shape, F32)
        if has_ctx:
            _flash_step(q_ref, kc_ref[...], with_ones(vc_ref[...]), m_ref, acc_ref, rep, dq, sub)

    _flash_step(q_ref, k_ref[...], with_ones(v_ref[...]), m_ref, acc_ref, rep, dq, sub)

    @pl.when(j == pl.num_programs(3) - 1)
    def _fin():
        for r in range(rep):
            c0 = r * 2 * LANES
            o_ref[:, r * HEAD_DIM:(r + 1) * HEAD_DIM] = (
                acc_ref[:, c0:c0 + LANES] / acc_ref[:, c0 + LANES:c0 + 2 * LANES]).astype(BF16)


def _flash(q, k, v, kc, vc, batch, n_groups, rep, dq, bq, bk, sub):
    lq = q.shape[0] // batch
    lk = k.shape[0] // batch
    nq, nk = lq // bq, lk // bk
    has_ctx = kc is not None
    dv = HEAD_DIM
    assert dv == LANES
    in_specs = [
        pl.BlockSpec((bq, rep * dq), lambda b, g, i, j: (b * nq + i, g)),
        pl.BlockSpec((bk, dq), lambda b, g, i, j: (b * nk + j, g)),
        pl.BlockSpec((bk, dv), lambda b, g, i, j: (b * nk + j, g)),
    ]
    args = [q, k, v]
    if has_ctx:
        lc = kc.shape[0] // batch
        in_specs += [
            pl.BlockSpec((lc, dq), lambda b, g, i, j: (b, g)),
            pl.BlockSpec((lc, dv), lambda b, g, i, j: (b, g)),
        ]
        args += [kc, vc]
    return pl.pallas_call(
        functools.partial(_flash_kernel, rep, dq, has_ctx, min(sub, bq)),
        grid=(batch, n_groups, nq, nk),
        in_specs=in_specs,
        out_specs=pl.BlockSpec((bq, rep * dv), lambda b, g, i, j: (b * nq + i, g)),
        out_shape=jax.ShapeDtypeStruct((batch * lq, n_groups * rep * dv), BF16),
        scratch_shapes=[
            pltpu.VMEM((bq, rep * LANES), F32),
            pltpu.VMEM((bq, rep * 2 * LANES), F32),
        ],
        compiler_params=_cparams(("parallel", "parallel", "parallel", "arbitrary")),
        name="flash_attention",
    )(*args)


def _band_kernel(rep, shared, has_sink, *refs):
    if has_sink:
        q_ref, kp_ref, kx_ref, kn_ref, vp_ref, vx_ref, vn_ref, kc_ref, vc_ref, bias_ref, sink_ref, o_ref = refs
    else:
        q_ref, kp_ref, kx_ref, kn_ref, vp_ref, vx_ref, vn_ref, kc_ref, vc_ref, bias_ref, o_ref = refs
    t = BAND_TILE
    for r in range(rep):
        hcol = slice(r * HEAD_DIM, (r + 1) * HEAD_DIM)
        kcol = slice(0, HEAD_DIM) if shared else hcol
        bias = bias_ref if shared else bias_ref.at[r]
        q = q_ref[:, hcol]
        scores = [_dot_nt(q, kc_ref[:, kcol])]
        for n, k_ref in enumerate((kp_ref, kx_ref, kn_ref)):
            scores.append(_dot_nt(q, k_ref[:, kcol]) + bias[:, n * t:(n + 1) * t])
        chunks = [s[:, c * LANES:(c + 1) * LANES] for s in scores for c in range(s.shape[1] // LANES)]
        m = jnp.max(functools.reduce(jnp.maximum, chunks), axis=-1, keepdims=True)
        if has_sink:
            sink = sink_ref[r, 0:1, 0:1]
            m = jnp.maximum(m, sink)
        probs = [jnp.exp2(s - m) for s in scores]
        pch = [p[:, c * LANES:(c + 1) * LANES] for p in probs for c in range(p.shape[1] // LANES)]
        l = jnp.sum(functools.reduce(jnp.add, pch), axis=-1, keepdims=True)
        if has_sink:
            l = l + jnp.exp2(sink - m)
        acc = _dot(probs[0].astype(BF16), vc_ref[:, kcol])
        for p, v_ref in zip(probs[1:], (vp_ref, vx_ref, vn_ref)):
            acc = acc + _dot(p.astype(BF16), v_ref[:, kcol])
        o_ref[:, hcol] = (acc / l).astype(BF16)


def _band_attention(q, k, v, kc, vc, bias, sink, batch, q_col0, kv_col0, shared):
    t = BAND_TILE
    rep = 4
    seq = q.shape[0] // batch
    nq = seq // t
    lc = kc.shape[0] // batch
    qw = rep * HEAD_DIM
    kw = HEAD_DIM if shared else qw
    n_groups = 2
    has_sink = sink is not None

    def tile_map(off):
        def f(g, b, i):
            return (b * nq + jnp.clip(i + off, 0, nq - 1), kv_col0 // kw + g)
        return f

    def cls(i):
        return jnp.where(i == 0, 0, jnp.where(i == nq - 1, 2, 1))

    kv_spec = [pl.BlockSpec((t, kw), tile_map(off)) for off in (-1, 0, 1)]
    ctx_spec = pl.BlockSpec((lc, kw), lambda g, b, i: (b, kv_col0 // kw + g))
    if shared:
        bias_spec = pl.BlockSpec((None, t, 3 * t), lambda g, b, i: (cls(i), 0, 0))
    else:
        bias_spec = pl.BlockSpec((rep, None, t, 3 * t), lambda g, b, i: (g, cls(i), 0, 0))
    in_specs = [pl.BlockSpec((t, qw), lambda g, b, i: (b * nq + i, q_col0 // qw + g))]
    in_specs += kv_spec + kv_spec + [ctx_spec, ctx_spec, bias_spec]
    args = [q, k, k, k, v, v, v, kc, vc, bias]
    if has_sink:
        in_specs.append(pl.BlockSpec((rep, 8, LANES), lambda g, b, i: (g, 0, 0)))
        args.append(sink)
    return pl.pallas_call(
        functools.partial(_band_kernel, rep, shared, has_sink),
        grid=(n_groups, batch, nq),
        in_specs=in_specs,
        out_specs=pl.BlockSpec((t, qw), lambda g, b, i: (b * nq + i, g)),
        out_shape=jax.ShapeDtypeStruct((batch * seq, n_groups * qw), BF16),
        compiler_params=_cparams(("parallel", "parallel", "parallel")),
        name="band_attention",
    )(*args)


def _outproj_kernel(n_in, *refs):
    a_refs = refs[:n_in]
    w_ref, g_ref, mod_ref, h_ref, o_ref, acc_ref = refs[n_in:]
    k0 = 0
    acc = None
    for a_ref in a_refs:
        kk = a_ref.shape[1]
        part = _dot(a_ref[...], w_ref[k0:k0 + kk, :])
        acc = part if acc is None else acc + part
        k0 += kk
    acc_ref[...] = acc
    _post_residual_to(acc_ref, h_ref, g_ref, mod_ref, 2, o_ref)


def _outproj(a_list, w, g_post, mod, h, rows_per_batch, bm):
    rows, d = h.shape
    tiles_per_batch = rows_per_batch // bm
    nb = mod.shape[0]

    def row_map(i):
        return (i, 0)

    def mod_map(i):
        return (i // tiles_per_batch if nb > 1 else 0, 0, 0)

    return pl.pallas_call(
        functools.partial(_outproj_kernel, len(a_list)),
        grid=(rows // bm,),
        in_specs=[pl.BlockSpec((bm, a.shape[1]), row_map) for a in a_list] + [
            _resident(w.shape),
            _resident((1, d)),
            pl.BlockSpec((None, 8, d), mod_map),
            pl.BlockSpec((bm, d), row_map),
        ],
        out_specs=pl.BlockSpec((bm, d), row_map),
        out_shape=jax.ShapeDtypeStruct((rows, d), F32),
        scratch_shapes=[pltpu.VMEM((bm, d), F32)],
        compiler_params=_cparams(("parallel",)),
        name="outproj",
    )(*a_list, w, g_post, mod, h)


def _ffn_kernel(h_ref, gpre_ref, gpost_ref, mod_ref, wg_ref, wu_ref, wd_ref, o_ref, u_ref):
    f = pl.program_id(1)

    @pl.when(f == 0)
    def _pro():
        _norm_mod_to(h_ref, gpre_ref, mod_ref, 3, 4, u_ref)
        o_ref[...] = jnp.zeros(o_ref.shape, F32)

    u = u_ref[...]
    act = (_silu(_dot(u, wg_ref[...])) * _dot(u, wu_ref[...])).astype(BF16)
    o_ref[...] += _dot(act, wd_ref[...])

    @pl.when(f == pl.num_programs(1) - 1)
    def _epi():
        _post_residual_to(o_ref, h_ref, gpost_ref, mod_ref, 5, o_ref)


def _cast_kernel(x_ref, o_ref):
    o_ref[...] = x_ref[...].astype(BF16)


def _to_bf16(w, br):
    depth, r, c = w.shape
    assert r % br == 0
    spec = pl.BlockSpec((None, br, c), lambda l, i: (l, i, 0))
    return pl.pallas_call(
        _cast_kernel,
        grid=(depth, r // br),
        in_specs=[spec],
        out_specs=spec,
        out_shape=jax.ShapeDtypeStruct(w.shape, BF16),
        compiler_params=_cparams(("parallel", "parallel")),
        name="cast_bf16",
    )(w)


def _ffn(h, g_pre, g_post, mod, wg, wu, wd, layer, rows_per_batch, bm, bf):
    rows, d = h.shape
    dff = wg.shape[2]
    tiles_per_batch = rows_per_batch // bm
    nb = mod.shape[0]

    def row_map(i, f):
        return (i, 0)

    def mod_map(i, f):
        return (i // tiles_per_batch if nb > 1 else 0, 0, 0)

    assert rows % bm == 0 and dff % bf == 0
    return pl.pallas_call(
        _ffn_kernel,
        grid=(rows // bm, dff // bf),
        in_specs=[
            pl.BlockSpec((bm, d), row_map),
            _resident((1, d)), _resident((1, d)),
            pl.BlockSpec((None, 8, d), mod_map),
            pl.BlockSpec((None, d, bf), lambda i, f: (layer, 0, f)),
            pl.BlockSpec((None, d, bf), lambda i, f: (layer, 0, f)),
            pl.BlockSpec((None, bf, d), lambda i, f: (layer, f, 0)),
        ],
        out_specs=pl.BlockSpec((bm, d), row_map),
        out_shape=jax.ShapeDtypeStruct((rows, d), F32),
        scratch_shapes=[pltpu.VMEM((bm, d), BF16)],
        compiler_params=_cparams(("parallel", "arbitrary")),
        name="ffn",
    )(h, g_pre, g_post, mod, wg, wu, wd)


def _rope_table(n_tok, dim):
    t = jnp.arange(n_tok, dtype=jnp.int32)
    row = (t // GRID_W).astype(F32)
    col = (t % GRID_W).astype(F32)
    half = dim // 2
    inv_freq = ROPE_BASE ** (-jnp.arange(0, half, 2, dtype=F32) / half)
    ar = row[:, None] * inv_freq[None, :]
    ac = col[:, None] * inv_freq[None, :]
    ang = jnp.concatenate([ar, ar, ac, ac], axis=-1)
    return jnp.cos(ang), jnp.sin(ang)


def _rpb_cols_kernel(r_ref, e_ref, o_ref):
    x = r_ref[...] * LOG2E
    hi = x.astype(BF16)
    r1 = x - hi.astype(F32)
    mid = r1.astype(BF16)
    lo = (r1 - mid.astype(F32)).astype(BF16)
    e = e_ref[...]
    o_ref[...] = (_dot(hi, e) + _dot(mid, e)) + _dot(lo, e)


def _band_bias(rpb, seq):
    t, g = BAND_TILE, GRID_W
    rpt = t // g
    nq = seq // t
    rows = seq // g
    kh = min(NA_KH, rows)
    n_h, n_di, n_dj = rpb.shape
    tiles = (0, 1, nq - 1)
    neg = NEG * LOG2E

    qpos = np.array(tiles)[:, None, None] * t + np.arange(t)[None, :, None]
    kpos = (np.array(tiles)[:, None, None] - 1) * t + np.arange(3 * t)[None, None, :]
    win = (np.abs(kpos - qpos) <= C_WINDOW) & (kpos >= 0) & (kpos < seq)
    bias_c = jnp.asarray(np.where(win, 0.0, neg).astype(np.float32))

    qc = np.arange(g)[:, None]
    kc = np.arange(g)[None, :]
    cs = np.clip(qc - NA_KW // 2, 0, g - NA_KW)
    col_ok = (kc >= cs) & (kc < cs + NA_KW)
    onehot = np.zeros((LANES, g, g), np.float32)
    qq, kk = np.nonzero(col_ok)
    onehot[(kk - qq + NA_KW - 1), qq, kk] = 1.0
    assert n_h * n_di <= LANES and n_dj <= LANES
    rpb2d = jnp.pad(rpb.reshape(n_h * n_di, n_dj).astype(F32), ((0, LANES - n_h * n_di), (0, LANES - n_dj)))
    cols = pl.pallas_call(
        _rpb_cols_kernel,
        out_shape=jax.ShapeDtypeStruct((LANES, g * g), F32),
        name="rpb_columns",
    )(rpb2d, jnp.asarray(onehot.reshape(LANES, g * g), BF16))
    cols = cols[:n_h * n_di].reshape(n_h, n_di, g, g) + jnp.asarray(np.where(col_ok, 0.0, neg).astype(np.float32))
    masked = jnp.full((n_h, g, g), neg, F32)

    tile_rows = []
    for ti in tiles:
        for a in range(rpt):
            qr = ti * rpt + a
            rs = min(max(qr - kh // 2, 0), rows - kh)
            krs = [(ti - 1) * rpt + b for b in range(3 * rpt)]
            tile_rows.append(jnp.concatenate(
                [cols[:, kr - qr + NA_KH - 1] if rs <= kr < rs + kh else masked for kr in krs], axis=-1))
    bias_d = jnp.stack(tile_rows, axis=1).reshape(n_h, len(tiles), t, 3 * t)
    return bias_c, bias_d


def _pad_mod(m):
    nb = m.shape[0]
    return jnp.pad(m.reshape(nb, 6, D_MODEL), ((0, 0), (0, 2), (0, 0)))


def _row(v):
    return v.reshape(1, -1).astype(F32)


def kernel(x, c, ctx, c_ctx, w_mod, b_mod, g_mix_pre, g_mix_post, g_ffn_pre, g_ffn_post, w_gate, w_up, w_down,
           ab_w_in, ab_w_out, a_q_norm, a_k_norm, b_q_norm, b_kv_norm, b_w_uq, b_w_ukv, cd_w_in, cd_w_out,
           c_q_norm, c_k_norm, c_sink, d_q_norm, d_k_norm, d_rpb):
    batch, seq, d = x.shape
    n_ctx = ctx.shape[1]
    depth = w_mod.shape[0]
    assert d == D_MODEL and seq % 512 == 0 and n_ctx % 256 == 0 and batch + 1 <= 8
    assert depth == 2

    cvec = jnp.zeros((8, d), F32).at[:batch].set(c).at[batch].set(c_ctx)
    mod_all = _modulation(cvec, w_mod, b_mod)

    cos_h, sin_h = _rope_table(seq, HEAD_DIM)
    cos_r, sin_r = _rope_table(seq, B_ROPE)
    pad_r = ((0, 0), (0, LANES - B_ROPE))
    tabs_ab = (cos_h, sin_h, jnp.pad(cos_r, pad_r), jnp.pad(sin_r, pad_r))
    tabs_cd = (cos_h, sin_h)

    bm_lat = 512
    bm_ctx = min(512, batch * n_ctx)
    bm_ffn = 1024 if seq % 1024 == 0 else 512
    bq_a = min(1024, seq)
    bq_b = min(4096, seq)
    bk = min(2048, seq)
    sub = 256

    bf_ffn = 512

    h_lat = x.reshape(batch * seq, d)
    h_ctx = ctx.reshape(batch * n_ctx, d)
    wg, wu, wd = _to_bf16(w_gate, 256), _to_bf16(w_up, 256), _to_bf16(w_down, 512)

    for i in range(depth):
        last = i == depth - 1
        need_ctx = not last
        j = i // 2
        mod_lat = _pad_mod(mod_all[i, :batch])
        mod_ctx = _pad_mod(mod_all[i, batch:batch + 1])
        g1, g2, g3, g4 = _row(g_mix_pre[i]), _row(g_mix_post[i]), _row(g_ffn_pre[i]), _row(g_ffn_post[i])

        if i % 2 == 0:
            w_in = ab_w_in[j]
            kr_cols = jnp.pad(w_in[:, A_IN + B_Q_RANK + B_KV_RANK:], ((0, 0), (0, LANES - B_ROPE)))
            w_ab = jnp.concatenate([w_in[:, :A_IN + B_Q_RANK + B_KV_RANK], kr_cols], axis=1).astype(BF16)
            wuq = jnp.pad(b_w_uq[j].reshape(B_Q_RANK, B_HEADS, B_NOPE + B_ROPE),
                          ((0, 0), (0, 0), (0, B_QK_PAD - B_NOPE - B_ROPE))).reshape(B_Q_RANK, B_HEADS * B_QK_PAD)
            wukv = b_w_ukv[j].reshape(B_KV_RANK, B_HEADS, B_NOPE + B_V)
            wuk = wukv[:, :, :B_NOPE].reshape(B_KV_RANK, B_HEADS * B_NOPE)
            wuv = wukv[:, :, B_NOPE:].reshape(B_KV_RANK, B_HEADS * B_V)
            norms = (_row(a_q_norm[j]), _row(a_k_norm[j]), _row(b_q_norm[j]), _row(b_kv_norm[j]),
                     wuq.astype(BF16), wuk.astype(BF16), wuv.astype(BF16))
            qa, ka, va, qb, kb, vb = _inproj_ab(h_lat, g1, mod_lat, w_ab, *norms, tabs_ab, seq, bm_lat)
            qac, kac, vac, qbc, kbc, vbc = _inproj_ab(h_ctx, g1, mod_ctx, w_ab, *norms, None, batch * n_ctx, bm_ctx)
            rep_a = A_HEADS // A_KV_HEADS
            oa = _flash(qa, ka, va, kac, vac, batch, A_KV_HEADS, rep_a, HEAD_DIM, bq_a, bk, sub)
            ob = _flash(qb, kb, vb, kbc, vbc, batch, B_HEADS, 1, B_QK_PAD, bq_b, bk, sub)
            w_out = ab_w_out[j].astype(BF16)
            attn_lat = [oa, ob]
            if need_ctx:
                oac = _flash(qac, kac, vac, None, None, batch, A_KV_HEADS, rep_a, HEAD_DIM, n_ctx, n_ctx, sub)
                obc = _flash(qbc, kbc, vbc, None, None, batch, B_HEADS, 1, B_QK_PAD, n_ctx, n_ctx, sub)
                attn_ctx = [oac, obc]
        else:
            w_cd = cd_w_in[j].astype(BF16)
            norms = (_row(c_q_norm[j]), _row(c_k_norm[j]), _row(d_q_norm[j]), _row(d_k_norm[j]))
            q_l, k_l, v_l = _inproj_cd(h_lat, g1, mod_lat, w_cd, *norms, tabs_cd, seq, bm_lat)
            q_c, k_c, v_c = _inproj_cd(h_ctx, g1, mod_ctx, w_cd, *norms, None, batch * n_ctx, bm_ctx)
            bias_c, bias_d = _band_bias(d_rpb[j], seq)
            sink = jnp.broadcast_to((c_sink[j].astype(F32) * LOG2E)[:, None, None], (C_HEADS, 8, LANES))
            kv_d = D_HEADS * HEAD_DIM
            attn_lat = [
                _band_attention(q_l, k_l, v_l, k_c, v_c, bias_c, sink, batch, 0, kv_d, True),
                _band_attention(q_l, k_l, v_l, k_c, v_c, bias_d, None, batch, C_HEADS * HEAD_DIM, 0, False),
            ]
            w_out = cd_w_out[j].astype(BF16)

        h_lat = _outproj(attn_lat, w_out, g2, mod_lat, h_lat, seq, bm_lat)
        h_lat = _ffn(h_lat, g3, g4, mod_lat, wg, wu, wd, i, seq, bm_ffn, bf_ffn)
        if need_ctx:
            h_ctx = _outproj(attn_ctx, w_out, g2, mod_ctx, h_ctx, batch * n_ctx, bm_ctx)
            h_ctx = _ffn(h_ctx, g3, g4, mod_ctx, wg, wu, wd, i, batch * n_ctx, bm_ctx, bf_ffn)
    return h_lat.reshape(batch, seq, d)
```

```python
import functools

import jax
import jax.numpy as jnp
import numpy as np
from jax import lax
from jax.experimental import pallas as pl
from jax.experimental.pallas import tpu as pltpu

F32 = jnp.float32
BF16 = jnp.bfloat16

D_MODEL = 2048
GRID_W = 64
HEAD_DIM = 128
ROPE_BASE = 10000.0
EPS = 1e-6
NEG = -1e30
LOG2E = 1.4426950408889634

A_HEADS, A_KV_HEADS = 8, 2
B_HEADS, B_Q_RANK, B_KV_RANK, B_NOPE, B_ROPE, B_V = 8, 512, 512, 128, 64, 128
C_HEADS, C_KV_HEADS, C_WINDOW = 8, 2, 128
D_HEADS, NA_KH, NA_KW = 8, 8, 16
A_IN = (A_HEADS + 2 * A_KV_HEADS) * HEAD_DIM
C_IN = (C_HEADS + 2 * C_KV_HEADS) * HEAD_DIM
D_FF = 5632

LANES = 128
MXU_COLS = 256
B_QK_PAD = 256
BAND_TILE = 256
VMEM_LIMIT = 56 * 1024 * 1024
ROW_CHUNK = 32
CHUNK_UNROLL = 4


def _cparams(sem):
    return pltpu.CompilerParams(dimension_semantics=sem, vmem_limit_bytes=VMEM_LIMIT)


def _resident(shape):
    nd = len(shape)
    return pl.BlockSpec(shape, lambda *_: (0,) * nd, pipeline_mode=pl.Buffered(1))


def _silu(x):
    return x * (1.0 / (1.0 + jnp.exp(-x)))


def _rms(x, g):
    ms = jnp.mean(x * x, axis=-1, keepdims=True)
    return x * lax.rsqrt(ms + EPS) * g


def _dot(a, b):
    return jnp.dot(a, b, preferred_element_type=F32)


def _dot_nt(a, b):
    return lax.dot_general(a, b, (((1,), (1,)), ((), ())), preferred_element_type=F32)


def _mod_kernel(c_ref, w_ref, b_ref, o_ref):
    s = _silu(c_ref[...])
    s_hi = s.astype(BF16)
    s_lo = (s - s_hi.astype(F32)).astype(BF16)
    w = w_ref[...]
    w_hi = w.astype(BF16)
    w_lo = (w - w_hi.astype(F32)).astype(BF16)
    o_ref[...] = _dot(s_hi, w_hi) + (_dot(s_lo, w_hi) + _dot(s_hi, w_lo)) + b_ref[...]


def _modulation(cvec, w_mod, b_mod):
    depth, d, n = w_mod.shape
    bn = 1024
    return pl.pallas_call(
        _mod_kernel,
        grid=(depth, n // bn),
        in_specs=[
            pl.BlockSpec((8, d), lambda i, j: (0, 0)),
            pl.BlockSpec((None, d, bn), lambda i, j: (i, 0, j)),
            pl.BlockSpec((None, 1, bn), lambda i, j: (i, 0, j)),
        ],
        out_specs=pl.BlockSpec((None, 8, bn), lambda i, j: (i, 0, j)),
        out_shape=jax.ShapeDtypeStruct((depth, 8, n), F32),
        compiler_params=_cparams(("parallel", "parallel")),
        name="modulation",
    )(cvec, w_mod, b_mod.reshape(depth, 1, n))


def _norm_mod_to(h_ref, g_ref, mod_ref, shift_row, scale_row, u_ref):
    bm = h_ref.shape[0]
    chunk = min(ROW_CHUNK, bm)
    gs = g_ref[...] * (1.0 + mod_ref[scale_row:scale_row + 1, :])
    sh = mod_ref[shift_row:shift_row + 1, :]

    def body(c, carry):
        r = pl.multiple_of(c * chunk, chunk)
        h = h_ref[pl.ds(r, chunk), :]
        inv = lax.rsqrt(jnp.mean(h * h, axis=-1, keepdims=True) + EPS)
        u_ref[pl.ds(r, chunk), :] = ((h * inv) * gs + sh).astype(BF16)
        return carry

    lax.fori_loop(0, bm // chunk, body, 0, unroll=CHUNK_UNROLL)


def _post_residual_to(acc_ref, h_ref, g_ref, mod_ref, gate_row, o_ref):
    bm = h_ref.shape[0]
    chunk = min(ROW_CHUNK, bm)
    gg = g_ref[...] * mod_ref[gate_row:gate_row + 1, :]

    def body(c, carry):
        r = pl.multiple_of(c * chunk, chunk)
        a = acc_ref[pl.ds(r, chunk), :]
        inv = lax.rsqrt(jnp.mean(a * a, axis=-1, keepdims=True) + EPS)
        o_ref[pl.ds(r, chunk), :] = h_ref[pl.ds(r, chunk), :] + (a * inv) * gg
        return carry

    lax.fori_loop(0, bm // chunk, body, 0, unroll=CHUNK_UNROLL)


def _rope(x, cos, sin, quarter):
    lane = lax.broadcasted_iota(jnp.int32, x.shape, 1)
    first = (lane % (2 * quarter)) < quarter
    rot = jnp.where(first, -pltpu.roll(x, LANES - quarter, 1), pltpu.roll(x, quarter, 1))
    return x * cos + rot * sin


def _emit_heads(u_ref, w_ref, col0, n_heads, gain, scale, rope, o_ref, ocol0):
    h = 0
    while h < n_heads:
        nh = min(MXU_COLS // HEAD_DIM, n_heads - h)
        c = col0 + h * HEAD_DIM
        p = _dot(u_ref[...], w_ref[:, c:c + nh * HEAD_DIM])
        for t in range(nh):
            x = p[:, t * HEAD_DIM:(t + 1) * HEAD_DIM]
            if gain is not None:
                x = _rms(x, gain)
                if rope is not None:
                    x = _rope(x, rope[0], rope[1], HEAD_DIM // 4)
                if scale != 1.0:
                    x = x * scale
            oc = ocol0 + (h + t) * HEAD_DIM
            o_ref[:, oc:oc + HEAD_DIM] = x.astype(BF16)
        h += nh


def _inproj_ab_kernel(use_rope, h_ref, g_ref, mod_ref, w_ref, aqn_ref, akn_ref, bqn_ref, bkvn_ref,
                      wuq_ref, wuk_ref, wuv_ref, cosh_ref, sinh_ref, cosr_ref, sinr_ref,
                      qa_ref, ka_ref, va_ref, qb_ref, kb_ref, vb_ref, u_ref, c_ref):
    _norm_mod_to(h_ref, g_ref, mod_ref, 0, 1, u_ref)
    rope_h = (cosh_ref[...], sinh_ref[...]) if use_rope else None
    a_scale = HEAD_DIM ** -0.5 * LOG2E
    _emit_heads(u_ref, w_ref, 0, A_HEADS, aqn_ref[...], a_scale, rope_h, qa_ref, 0)
    _emit_heads(u_ref, w_ref, A_HEADS * HEAD_DIM, A_KV_HEADS, akn_ref[...], 1.0, rope_h, ka_ref, 0)
    _emit_heads(u_ref, w_ref, (A_HEADS + A_KV_HEADS) * HEAD_DIM, A_KV_HEADS, None, 1.0, None, va_ref, 0)

    b_scale = (B_NOPE + B_ROPE) ** -0.5 * LOG2E
    cq = _dot(u_ref[...], w_ref[:, A_IN:A_IN + B_Q_RANK])
    c_ref[...] = _rms(cq, bqn_ref[...]).astype(BF16)
    for hd in range(B_HEADS):
        q = _dot(c_ref[...], wuq_ref[:, hd * B_QK_PAD:(hd + 1) * B_QK_PAD])
        q_nope = q[:, :B_NOPE]
        q_rope = q[:, B_NOPE:]
        if use_rope:
            q_rope = _rope(q_rope, cosr_ref[...], sinr_ref[...], B_ROPE // 4)
        qb_ref[:, hd * B_QK_PAD:hd * B_QK_PAD + B_NOPE] = (q_nope * b_scale).astype(BF16)
        qb_ref[:, hd * B_QK_PAD + B_NOPE:(hd + 1) * B_QK_PAD] = (q_rope * b_scale).astype(BF16)
    kr = _dot(u_ref[...], w_ref[:, A_IN + B_Q_RANK + B_KV_RANK:A_IN + B_Q_RANK + B_KV_RANK + LANES])
    if use_rope:
        kr = _rope(kr, cosr_ref[...], sinr_ref[...], B_ROPE // 4)
    kr = kr.astype(BF16)
    ckv = _dot(u_ref[...], w_ref[:, A_IN + B_Q_RANK:A_IN + B_Q_RANK + B_KV_RANK])
    c_ref[...] = _rms(ckv, bkvn_ref[...]).astype(BF16)
    for pair in range(B_HEADS * B_NOPE // MXU_COLS):
        kn = _dot(c_ref[...], wuk_ref[:, pair * MXU_COLS:(pair + 1) * MXU_COLS])
        for t in range(MXU_COLS // B_NOPE):
            hd = pair * (MXU_COLS // B_NOPE) + t
            kb_ref[:, hd * B_QK_PAD:hd * B_QK_PAD + B_NOPE] = kn[:, t * B_NOPE:(t + 1) * B_NOPE].astype(BF16)
            kb_ref[:, hd * B_QK_PAD + B_NOPE:(hd + 1) * B_QK_PAD] = kr
        vb_ref[:, pair * MXU_COLS:(pair + 1) * MXU_COLS] = _dot(
            c_ref[...], wuv_ref[:, pair * MXU_COLS:(pair + 1) * MXU_COLS]).astype(BF16)


def _inproj_ab(h, g_pre, mod, w, aqn, akn, bqn, bkvn, wuq, wuk, wuv, rope_tabs, rows_per_batch, bm):
    rows, d = h.shape
    use_rope = rope_tabs is not None
    tiles_per_batch = rows_per_batch // bm
    nb = mod.shape[0]

    def row_map(i):
        return (i, 0)

    def mod_map(i):
        return (i // tiles_per_batch if nb > 1 else 0, 0, 0)

    def tab_map(i):
        return (i % tiles_per_batch, 0)

    if use_rope:
        tabs = list(rope_tabs)
    else:
        tabs = [jnp.zeros((bm, LANES), F32)] * 4

        def tab_map(i):
            return (0, 0)

    outs = [
        (A_HEADS * HEAD_DIM), (A_KV_HEADS * HEAD_DIM), (A_KV_HEADS * HEAD_DIM),
        (B_HEADS * B_QK_PAD), (B_HEADS * B_QK_PAD), (B_HEADS * B_V),
    ]
    return pl.pallas_call(
        functools.partial(_inproj_ab_kernel, use_rope),
        grid=(rows // bm,),
        in_specs=[
            pl.BlockSpec((bm, d), row_map),
            _resident((1, d)),
            pl.BlockSpec((None, 8, d), mod_map),
            _resident(w.shape),
            _resident((1, HEAD_DIM)), _resident((1, HEAD_DIM)),
            _resident((1, B_Q_RANK)), _resident((1, B_KV_RANK)),
            _resident(wuq.shape), _resident(wuk.shape), _resident(wuv.shape),
        ] + [pl.BlockSpec((bm, LANES), tab_map)] * 4,
        out_specs=[pl.BlockSpec((bm, n), row_map) for n in outs],
        out_shape=[jax.ShapeDtypeStruct((rows, n), BF16) for n in outs],
        scratch_shapes=[pltpu.VMEM((bm, d), BF16), pltpu.VMEM((bm, B_Q_RANK), BF16)],
        compiler_params=_cparams(("parallel",)),
        name="inproj_ab",
    )(h, g_pre, mod, w, aqn, akn, bqn, bkvn, wuq, wuk, wuv, *tabs)


def _inproj_cd_kernel(use_rope, h_ref, g_ref, mod_ref, w_ref, cqn_ref, ckn_ref, dqn_ref, dkn_ref,
                      cosh_ref, sinh_ref, q_ref, kv_ref, u_ref):
    _norm_mod_to(h_ref, g_ref, mod_ref, 0, 1, u_ref)
    rope_h = (cosh_ref[...], sinh_ref[...]) if use_rope else None
    scale = HEAD_DIM ** -0.5 * LOG2E
    kv_c = C_KV_HEADS * HEAD_DIM
    kv_d = D_HEADS * HEAD_DIM
    _emit_heads(u_ref, w_ref, 0, C_HEADS, cqn_ref[...], scale, rope_h, q_ref, 0)
    _emit_heads(u_ref, w_ref, C_HEADS * HEAD_DIM, C_KV_HEADS, ckn_ref[...], 1.0, rope_h, kv_ref, 2 * kv_d)
    _emit_heads(u_ref, w_ref, C_HEADS * HEAD_DIM + kv_c, C_KV_HEADS, None, 1.0, None, kv_ref, 2 * kv_d + kv_c)
    _emit_heads(u_ref, w_ref, C_IN, D_HEADS, dqn_ref[...], scale, None, q_ref, C_HEADS * HEAD_DIM)
    _emit_heads(u_ref, w_ref, C_IN + kv_d, D_HEADS, dkn_ref[...], 1.0, None, kv_ref, 0)
    _emit_heads(u_ref, w_ref, C_IN + 2 * kv_d, D_HEADS, None, 1.0, None, kv_ref, kv_d)


def _inproj_cd(h, g_pre, mod, w, cqn, ckn, dqn, dkn, rope_tabs, rows_per_batch, bm):
    rows, d = h.shape
    use_rope = rope_tabs is not None
    tiles_per_batch = rows_per_batch // bm
    nb = mod.shape[0]

    def row_map(i):
        return (i, 0)

    def mod_map(i):
        return (i // tiles_per_batch if nb > 1 else 0, 0, 0)

    if use_rope:
        tabs = list(rope_tabs)

        def tab_map(i):
            return (i % tiles_per_batch, 0)
    else:
        tabs = [jnp.zeros((bm, LANES), F32)] * 2

        def tab_map(i):
            return (0, 0)

    outs = [(C_HEADS + D_HEADS) * HEAD_DIM, 2 * (C_KV_HEADS + D_HEADS) * HEAD_DIM]
    return pl.pallas_call(
        functools.partial(_inproj_cd_kernel, use_rope),
        grid=(rows // bm,),
        in_specs=[
            pl.BlockSpec((bm, d), row_map),
            _resident((1, d)),
            pl.BlockSpec((None, 8, d), mod_map),
            _resident(w.shape),
            _resident((1, HEAD_DIM)), _resident((1, HEAD_DIM)), _resident((1, HEAD_DIM)), _resident((1, HEAD_DIM)),
        ] + [pl.BlockSpec((bm, LANES), tab_map)] * 2,
        out_specs=[pl.BlockSpec((bm, n), row_map) for n in outs],
        out_shape=[jax.ShapeDtypeStruct((rows, n), BF16) for n in outs],
        scratch_shapes=[pltpu.VMEM((bm, d), BF16)],
        compiler_params=_cparams(("parallel",)),
        name="inproj_cd",
    )(h, g_pre, mod, w, cqn, ckn, dqn, dkn, *tabs)


def _flash_step(q_ref, kt, vt, m_ref, acc_ref, rep, dq, sub):
    bq = q_ref.shape[0]
    n_chunks = kt.shape[0] // LANES
    for r in range(rep):
        for sb in range(bq // sub):
            rows = slice(sb * sub, (sb + 1) * sub)
            mcol = slice(r * LANES, (r + 1) * LANES)
            acol = slice(r * 2 * LANES, (r + 1) * 2 * LANES)
            s = _dot_nt(q_ref[rows, r * dq:(r + 1) * dq], kt)
            chunks = [s[:, c * LANES:(c + 1) * LANES] for c in range(n_chunks)]
            lane_max = functools.reduce(jnp.maximum, chunks)
            m_prev = m_ref[rows, mcol]
            m_new = jnp.maximum(m_prev, jnp.max(lane_max, axis=-1, keepdims=True))
            alpha = jnp.exp2(m_prev - m_new)
            p = jnp.concatenate([jnp.exp2(ch - m_new) for ch in chunks], axis=1).astype(BF16)
            acc_ref[rows, acol] = jnp.concatenate([alpha, alpha], axis=1) * acc_ref[rows, acol] + _dot(p, vt)
            m_ref[rows, mcol] = m_new


def _flash_kernel(rep, dq, has_ctx, sub, *refs):
    if has_ctx:
        q_ref, k_ref, v_ref, kc_ref, vc_ref, o_ref, m_ref, acc_ref = refs
    else:
        q_ref, k_ref, v_ref, o_ref, m_ref, acc_ref = refs
    j = pl.program_id(3)

    def with_ones(v):
        return jnp.concatenate([v, jnp.ones(v.shape, BF16)], axis=1)

    @pl.when(j == 0)
    def _init():
        m_ref[...] = jnp.full(m_ref.shape, NEG, F32)
        acc_ref[...] = jnp.zeros(acc_ref.shape, F32)
        if has_ctx:
            _flash_step(q_ref, kc_ref[...], with_ones(vc_ref[...]), m_ref, acc_ref, rep, dq, sub)

    _flash_step(q_ref, k_ref[...], with_ones(v_ref[...]), m_ref, acc_ref, rep, dq, sub)

    @pl.when(j == pl.num_programs(3) - 1)
    def _fin():
        for r in range(rep):
            c0 = r * 2 * LANES
            o_ref[:, r * HEAD_DIM:(r + 1) * HEAD_DIM] = (
                acc_ref[:, c0:c0 + LANES] / acc_ref[:, c0 + LANES:c0 + 2 * LANES]).astype(BF16)


def _flash(q, k, v, kc, vc, batch, n_groups, rep, dq, bq, bk, sub):
    lq = q.shape[0] // batch
    lk = k.shape[0] // batch
    nq, nk = lq // bq, lk // bk
    has_ctx = kc is not None
    dv = HEAD_DIM
    assert dv == LANES
    in_specs = [
        pl.BlockSpec((bq, rep * dq), lambda b, g, i, j: (b * nq + i, g)),
        pl.BlockSpec((bk, dq), lambda b, g, i, j: (b * nk + j, g)),
        pl.BlockSpec((bk, dv), lambda b, g, i, j: (b * nk + j, g)),
    ]
    args = [q, k, v]
    if has_ctx:
        lc = kc.shape[0] // batch
        in_specs += [
            pl.BlockSpec((lc, dq), lambda b, g, i, j: (b, g)),
            pl.BlockSpec((lc, dv), lambda b, g, i, j: (b, g)),
        ]
        args += [kc, vc]
    return pl.pallas_call(
        functools.partial(_flash_kernel, rep, dq, has_ctx, min(sub, bq)),
        grid=(batch, n_groups, nq, nk),
        in_specs=in_specs,
        out_specs=pl.BlockSpec((bq, rep * dv), lambda b, g, i, j: (b * nq + i, g)),
        out_shape=jax.ShapeDtypeStruct((batch * lq, n_groups * rep * dv), BF16),
        scratch_shapes=[
            pltpu.VMEM((bq, rep * LANES), F32),
            pltpu.VMEM((bq, rep * 2 * LANES), F32),
        ],
        compiler_params=_cparams(("parallel", "parallel", "parallel", "arbitrary")),
        name="flash_attention",
    )(*args)


def _band_kernel(n_q, n_kv, per_head_bias, has_sink, *refs):
    if has_sink:
        q_ref, kvp_ref, kvx_ref, kvn_ref, kvc_ref, bias_ref, sink_ref, o_ref = refs
    else:
        q_ref, kvp_ref, kvx_ref, kvn_ref, kvc_ref, bias_ref, o_ref = refs
    t = BAND_TILE
    for r in range(n_q):
        hcol = slice(r * HEAD_DIM, (r + 1) * HEAD_DIM)
        kv_head = r * n_kv // n_q
        kcol = slice(kv_head * HEAD_DIM, (kv_head + 1) * HEAD_DIM)
        vcol = slice((n_kv + kv_head) * HEAD_DIM, (n_kv + kv_head + 1) * HEAD_DIM)
        bias = bias_ref.at[r] if per_head_bias else bias_ref
        q = q_ref[:, hcol]
        scores = [_dot_nt(q, kvc_ref[:, kcol])]
        for n, kv_ref in enumerate((kvp_ref, kvx_ref, kvn_ref)):
            scores.append(_dot_nt(q, kv_ref[:, kcol]) + bias[:, n * t:(n + 1) * t])
        chunks = [s[:, c * LANES:(c + 1) * LANES] for s in scores for c in range(s.shape[1] // LANES)]
        m = jnp.max(functools.reduce(jnp.maximum, chunks), axis=-1, keepdims=True)
        if has_sink:
            sink = sink_ref[r, 0:1, 0:1]
            m = jnp.maximum(m, sink)
        probs = [jnp.exp2(s - m) for s in scores]
        pch = [p[:, c * LANES:(c + 1) * LANES] for p in probs for c in range(p.shape[1] // LANES)]
        l = jnp.sum(functools.reduce(jnp.add, pch), axis=-1, keepdims=True)
        if has_sink:
            l = l + jnp.exp2(sink - m)
        acc = _dot(probs[0].astype(BF16), kvc_ref[:, vcol])
        for p, kv_ref in zip(probs[1:], (kvp_ref, kvx_ref, kvn_ref)):
            acc = acc + _dot(p.astype(BF16), kv_ref[:, vcol])
        o_ref[:, hcol] = (acc / l).astype(BF16)


def _band_attention(q, kv, kvc, bias, sink, batch, q_col0, kv_col0, n_kv):
    t = BAND_TILE
    n_q = 8
    seq = q.shape[0] // batch
    nq = seq // t
    lc = kvc.shape[0] // batch
    qw = n_q * HEAD_DIM
    kw = 2 * n_kv * HEAD_DIM
    has_sink = sink is not None
    per_head_bias = bias.ndim == 4

    def tile_map(off):
        def f(b, i):
            return (b * nq + jnp.clip(i + off, 0, nq - 1), kv_col0 // kw)
        return f

    def cls(i):
        return jnp.where(i == 0, 0, jnp.where(i == nq - 1, 2, 1))

    if per_head_bias:
        bias_spec = pl.BlockSpec((n_q, None, t, 3 * t), lambda b, i: (0, cls(i), 0, 0))
    else:
        bias_spec = pl.BlockSpec((None, t, 3 * t), lambda b, i: (cls(i), 0, 0))
    in_specs = [pl.BlockSpec((t, qw), lambda b, i: (b * nq + i, q_col0 // qw))]
    in_specs += [pl.BlockSpec((t, kw), tile_map(off)) for off in (-1, 0, 1)]
    in_specs += [pl.BlockSpec((lc, kw), lambda b, i: (b, kv_col0 // kw)), bias_spec]
    args = [q, kv, kv, kv, kvc, bias]
    if has_sink:
        in_specs.append(pl.BlockSpec((n_q, 8, LANES), lambda b, i: (0, 0, 0)))
        args.append(sink)
    return pl.pallas_call(
        functools.partial(_band_kernel, n_q, n_kv, per_head_bias, has_sink),
        grid=(batch, nq),
        in_specs=in_specs,
        out_specs=pl.BlockSpec((t, qw), lambda b, i: (b * nq + i, 0)),
        out_shape=jax.ShapeDtypeStruct((batch * seq, qw), BF16),
        compiler_params=_cparams(("parallel", "parallel")),
        name="band_attention",
    )(*args)


def _outproj_kernel(n_in, *refs):
    a_refs = refs[:n_in]
    w_ref, g_ref, mod_ref, h_ref, o_ref, acc_ref = refs[n_in:]
    k0 = 0
    acc = None
    for a_ref in a_refs:
        kk = a_ref.shape[1]
        part = _dot(a_ref[...], w_ref[k0:k0 + kk, :])
        acc = part if acc is None else acc + part
        k0 += kk
    acc_ref[...] = acc
    _post_residual_to(acc_ref, h_ref, g_ref, mod_ref, 2, o_ref)


def _outproj(a_list, w, g_post, mod, h, rows_per_batch, bm):
    rows, d = h.shape
    tiles_per_batch = rows_per_batch // bm
    nb = mod.shape[0]

    def row_map(i):
        return (i, 0)

    def mod_map(i):
        return (i // tiles_per_batch if nb > 1 else 0, 0, 0)

    return pl.pallas_call(
        functools.partial(_outproj_kernel, len(a_list)),
        grid=(rows // bm,),
        in_specs=[pl.BlockSpec((bm, a.shape[1]), row_map) for a in a_list] + [
            _resident(w.shape),
            _resident((1, d)),
            pl.BlockSpec((None, 8, d), mod_map),
            pl.BlockSpec((bm, d), row_map),
        ],
        out_specs=pl.BlockSpec((bm, d), row_map),
        out_shape=jax.ShapeDtypeStruct((rows, d), F32),
        scratch_shapes=[pltpu.VMEM((bm, d), F32)],
        compiler_params=_cparams(("parallel",)),
        name="outproj",
    )(*a_list, w, g_post, mod, h)


def _ffn_kernel(h_ref, gpre_ref, gpost_ref, mod_ref, wg_ref, wu_ref, wd_ref, o_ref, u_ref):
    f = pl.program_id(1)

    @pl.when(f == 0)
    def _pro():
        _norm_mod_to(h_ref, gpre_ref, mod_ref, 3, 4, u_ref)
        o_ref[...] = jnp.zeros(o_ref.shape, F32)

    u = u_ref[...]
    act = (_silu(_dot(u, wg_ref[...])) * _dot(u, wu_ref[...])).astype(BF16)
    o_ref[...] += _dot(act, wd_ref[...])

    @pl.when(f == pl.num_programs(1) - 1)
    def _epi():
        _post_residual_to(o_ref, h_ref, gpost_ref, mod_ref, 5, o_ref)


def _cast_kernel(x_ref, o_ref):
    o_ref[...] = x_ref[...].astype(BF16)


def _to_bf16(w, br):
    depth, r, c = w.shape
    assert r % br == 0
    spec = pl.BlockSpec((None, br, c), lambda l, i: (l, i, 0))
    return pl.pallas_call(
        _cast_kernel,
        grid=(depth, r // br),
        in_specs=[spec],
        out_specs=spec,
        out_shape=jax.ShapeDtypeStruct(w.shape, BF16),
        compiler_params=_cparams(("parallel", "parallel")),
        name="cast_bf16",
    )(w)


def _ffn(h, g_pre, g_post, mod, wg, wu, wd, layer, rows_per_batch, bm, bf):
    rows, d = h.shape
    dff = wg.shape[2]
    tiles_per_batch = rows_per_batch // bm
    nb = mod.shape[0]

    def row_map(i, f):
        return (i, 0)

    def mod_map(i, f):
        return (i // tiles_per_batch if nb > 1 else 0, 0, 0)

    assert rows % bm == 0 and dff % bf == 0
    return pl.pallas_call(
        _ffn_kernel,
        grid=(rows // bm, dff // bf),
        in_specs=[
            pl.BlockSpec((bm, d), row_map),
            _resident((1, d)), _resident((1, d)),
            pl.BlockSpec((None, 8, d), mod_map),
            pl.BlockSpec((None, d, bf), lambda i, f: (layer, 0, f)),
            pl.BlockSpec((None, d, bf), lambda i, f: (layer, 0, f)),
            pl.BlockSpec((None, bf, d), lambda i, f: (layer, f, 0)),
        ],
        out_specs=pl.BlockSpec((bm, d), row_map),
        out_shape=jax.ShapeDtypeStruct((rows, d), F32),
        scratch_shapes=[pltpu.VMEM((bm, d), BF16)],
        compiler_params=_cparams(("parallel", "arbitrary")),
        name="ffn",
    )(h, g_pre, g_post, mod, wg, wu, wd)


def _rope_table(n_tok, dim):
    t = jnp.arange(n_tok, dtype=jnp.int32)
    row = (t // GRID_W).astype(F32)
    col = (t % GRID_W).astype(F32)
    half = dim // 2
    inv_freq = ROPE_BASE ** (-jnp.arange(0, half, 2, dtype=F32) / half)
    ar = row[:, None] * inv_freq[None, :]
    ac = col[:, None] * inv_freq[None, :]
    ang = jnp.concatenate([ar, ar, ac, ac], axis=-1)
    return jnp.cos(ang), jnp.sin(ang)


def _rpb_cols_kernel(r_ref, e_ref, o_ref):
    x = r_ref[...] * LOG2E
    hi = x.astype(BF16)
    r1 = x - hi.astype(F32)
    mid = r1.astype(BF16)
    lo = (r1 - mid.astype(F32)).astype(BF16)
    e = e_ref[...]
    o_ref[...] = (_dot(hi, e) + _dot(mid, e)) + _dot(lo, e)


def _band_bias(rpb, seq):
    t, g = BAND_TILE, GRID_W
    rpt = t // g
    nq = seq // t
    rows = seq // g
    kh = min(NA_KH, rows)
    n_h, n_di, n_dj = rpb.shape
    tiles = (0, 1, nq - 1)
    neg = NEG * LOG2E

    qpos = np.array(tiles)[:, None, None] * t + np.arange(t)[None, :, None]
    kpos = (np.array(tiles)[:, None, None] - 1) * t + np.arange(3 * t)[None, None, :]
    win = (np.abs(kpos - qpos) <= C_WINDOW) & (kpos >= 0) & (kpos < seq)
    bias_c = jnp.asarray(np.where(win, 0.0, neg).astype(np.float32))

    qc = np.arange(g)[:, None]
    kc = np.arange(g)[None, :]
    cs = np.clip(qc - NA_KW // 2, 0, g - NA_KW)
    col_ok = (kc >= cs) & (kc < cs + NA_KW)
    onehot = np.zeros((LANES, g, g), np.float32)
    qq, kk = np.nonzero(col_ok)
    onehot[(kk - qq + NA_KW - 1), qq, kk] = 1.0
    assert n_h * n_di <= LANES and n_dj <= LANES
    rpb2d = jnp.pad(rpb.reshape(n_h * n_di, n_dj).astype(F32), ((0, LANES - n_h * n_di), (0, LANES - n_dj)))
    cols = pl.pallas_call(
        _rpb_cols_kernel,
        out_shape=jax.ShapeDtypeStruct((LANES, g * g), F32),
        name="rpb_columns",
    )(rpb2d, jnp.asarray(onehot.reshape(LANES, g * g), BF16))
    cols = cols[:n_h * n_di].reshape(n_h, n_di, g, g) + jnp.asarray(np.where(col_ok, 0.0, neg).astype(np.float32))
    masked = jnp.full((n_h, g, g), neg, F32)

    tile_rows = []
    for ti in tiles:
        for a in range(rpt):
            qr = ti * rpt + a
            rs = min(max(qr - kh // 2, 0), rows - kh)
            krs = [(ti - 1) * rpt + b for b in range(3 * rpt)]
            tile_rows.append(jnp.concatenate(
                [cols[:, kr - qr + NA_KH - 1] if rs <= kr < rs + kh else masked for kr in krs], axis=-1))
    bias_d = jnp.stack(tile_rows, axis=1).reshape(n_h, len(tiles), t, 3 * t)
    return bias_c, bias_d


def _pad_mod(m):
    nb = m.shape[0]
    return jnp.pad(m.reshape(nb, 6, D_MODEL), ((0, 0), (0, 2), (0, 0)))


def _row(v):
    return v.reshape(1, -1).astype(F32)


def kernel(x, c, ctx, c_ctx, w_mod, b_mod, g_mix_pre, g_mix_post, g_ffn_pre, g_ffn_post, w_gate, w_up, w_down,
           ab_w_in, ab_w_out, a_q_norm, a_k_norm, b_q_norm, b_kv_norm, b_w_uq, b_w_ukv, cd_w_in, cd_w_out,
           c_q_norm, c_k_norm, c_sink, d_q_norm, d_k_norm, d_rpb):
    batch, seq, d = x.shape
    n_ctx = ctx.shape[1]
    depth = w_mod.shape[0]
    assert d == D_MODEL and seq % 512 == 0 and n_ctx % 256 == 0 and batch + 1 <= 8
    assert depth == 2

    cvec = jnp.zeros((8, d), F32).at[:batch].set(c).at[batch].set(c_ctx)
    mod_all = _modulation(cvec, w_mod, b_mod)

    cos_h, sin_h = _rope_table(seq, HEAD_DIM)
    cos_r, sin_r = _rope_table(seq, B_ROPE)
    pad_r = ((0, 0), (0, LANES - B_ROPE))
    tabs_ab = (cos_h, sin_h, jnp.pad(cos_r, pad_r), jnp.pad(sin_r, pad_r))
    tabs_cd = (cos_h, sin_h)

    bm_lat = 512
    bm_ctx = min(512, batch * n_ctx)
    bm_ffn = 1024 if seq % 1024 == 0 else 512
    bq_a = min(1024, seq)
    bq_b = min(4096, seq)
    bk = min(2048, seq)
    sub = 256

    bf_ffn = 512

    h_lat = x.reshape(batch * seq, d)
    h_ctx = ctx.reshape(batch * n_ctx, d)
    wg, wu, wd = _to_bf16(w_gate, 256), _to_bf16(w_up, 256), _to_bf16(w_down, 512)

    for i in range(depth):
        last = i == depth - 1
        need_ctx = not last
        j = i // 2
        mod_lat = _pad_mod(mod_all[i, :batch])
        mod_ctx = _pad_mod(mod_all[i, batch:batch + 1])
        g1, g2, g3, g4 = _row(g_mix_pre[i]), _row(g_mix_post[i]), _row(g_ffn_pre[i]), _row(g_ffn_post[i])

        if i % 2 == 0:
            w_in = ab_w_in[j]
            kr_cols = jnp.pad(w_in[:, A_IN + B_Q_RANK + B_KV_RANK:], ((0, 0), (0, LANES - B_ROPE)))
            w_ab = jnp.concatenate([w_in[:, :A_IN + B_Q_RANK + B_KV_RANK], kr_cols], axis=1).astype(BF16)
            wuq = jnp.pad(b_w_uq[j].reshape(B_Q_RANK, B_HEADS, B_NOPE + B_ROPE),
                          ((0, 0), (0, 0), (0, B_QK_PAD - B_NOPE - B_ROPE))).reshape(B_Q_RANK, B_HEADS * B_QK_PAD)
            wukv = b_w_ukv[j].reshape(B_KV_RANK, B_HEADS, B_NOPE + B_V)
            wuk = wukv[:, :, :B_NOPE].reshape(B_KV_RANK, B_HEADS * B_NOPE)
            wuv = wukv[:, :, B_NOPE:].reshape(B_KV_RANK, B_HEADS * B_V)
            norms = (_row(a_q_norm[j]), _row(a_k_norm[j]), _row(b_q_norm[j]), _row(b_kv_norm[j]),
                     wuq.astype(BF16), wuk.astype(BF16), wuv.astype(BF16))
            qa, ka, va, qb, kb, vb = _inproj_ab(h_lat, g1, mod_lat, w_ab, *norms, tabs_ab, seq, bm_lat)
            qac, kac, vac, qbc, kbc, vbc = _inproj_ab(h_ctx, g1, mod_ctx, w_ab, *norms, None, batch * n_ctx, bm_ctx)
            rep_a = A_HEADS // A_KV_HEADS
            oa = _flash(qa, ka, va, kac, vac, batch, A_KV_HEADS, rep_a, HEAD_DIM, bq_a, bk, sub)
            ob = _flash(qb, kb, vb, kbc, vbc, batch, B_HEADS, 1, B_QK_PAD, bq_b, bk, sub)
            w_out = ab_w_out[j].astype(BF16)
            attn_lat = [oa, ob]
            if need_ctx:
                oac = _flash(qac, kac, vac, None, None, batch, A_KV_HEADS, rep_a, HEAD_DIM, n_ctx, n_ctx, sub)
                obc = _flash(qbc, kbc, vbc, None, None, batch, B_HEADS, 1, B_QK_PAD, n_ctx, n_ctx, sub)
                attn_ctx = [oac, obc]
        else:
            w_cd = cd_w_in[j].astype(BF16)
            norms = (_row(c_q_norm[j]), _row(c_k_norm[j]), _row(d_q_norm[j]), _row(d_k_norm[j]))
            q_l, kv_l = _inproj_cd(h_lat, g1, mod_lat, w_cd, *norms, tabs_cd, seq, bm_lat)
            _, kv_c = _inproj_cd(h_ctx, g1, mod_ctx, w_cd, *norms, None, batch * n_ctx, bm_ctx)
            bias_c, bias_d = _band_bias(d_rpb[j], seq)
            sink = jnp.broadcast_to((c_sink[j].astype(F32) * LOG2E)[:, None, None], (C_HEADS, 8, LANES))
            kv_d = 2 * D_HEADS * HEAD_DIM
            attn_lat = [
                _band_attention(q_l, kv_l, kv_c, bias_c, sink, batch, 0, kv_d, C_KV_HEADS),
                _band_attention(q_l, kv_l, kv_c, bias_d, None, batch, C_HEADS * HEAD_DIM, 0, D_HEADS),
            ]
            w_out = cd_w_out[j].astype(BF16)

        h_lat = _outproj(attn_lat, w_out, g2, mod_lat, h_lat, seq, bm_lat)
        h_lat = _ffn(h_lat, g3, g4, mod_lat, wg, wu, wd, i, seq, bm_ffn, bf_ffn)
        if need_ctx:
            h_ctx = _outproj(attn_ctx, w_out, g2, mod_ctx, h_ctx, batch * n_ctx, bm_ctx)
            h_ctx = _ffn(h_ctx, g3, g4, mod_ctx, wg, wu, wd, i, batch * n_ctx, bm_ctx, bf_ffn)
    return h_lat.reshape(batch, seq, d)
```

```python
import functools

import jax
import jax.numpy as jnp
import numpy as np
from jax import lax
from jax.experimental import pallas as pl
from jax.experimental.pallas import tpu as pltpu

F32 = jnp.float32
BF16 = jnp.bfloat16

D_MODEL = 2048
GRID_W = 64
HEAD_DIM = 128
ROPE_BASE = 10000.0
EPS = 1e-6
NEG = -1e30
LOG2E = 1.4426950408889634

A_HEADS, A_KV_HEADS = 8, 2
B_HEADS, B_Q_RANK, B_KV_RANK, B_NOPE, B_ROPE, B_V = 8, 512, 512, 128, 64, 128
C_HEADS, C_KV_HEADS, C_WINDOW = 8, 2, 128
D_HEADS, NA_KH, NA_KW = 8, 8, 16
A_IN = (A_HEADS + 2 * A_KV_HEADS) * HEAD_DIM
C_IN = (C_HEADS + 2 * C_KV_HEADS) * HEAD_DIM
D_FF = 5632

LANES = 128
MXU_COLS = 256
B_QK_PAD = 256
BAND_TILE = 256
VMEM_LIMIT = 56 * 1024 * 1024
ROW_CHUNK = 32
PROJ_ROW_BLOCK = 256
FFN_ROW_BLOCK = 256
CHUNK_UNROLL = 4


def _cparams(sem):
    return pltpu.CompilerParams(dimension_semantics=sem, vmem_limit_bytes=VMEM_LIMIT)


def _resident(shape):
    nd = len(shape)
    return pl.BlockSpec(shape, lambda *_: (0,) * nd, pipeline_mode=pl.Buffered(1))


def _silu(x):
    return x * (1.0 / (1.0 + jnp.exp(-x)))


def _rms(x, g):
    ms = jnp.mean(x * x, axis=-1, keepdims=True)
    return x * lax.rsqrt(ms + EPS) * g


def _dot(a, b):
    return jnp.dot(a, b, preferred_element_type=F32)


def _dot_nt(a, b):
    return lax.dot_general(a, b, (((1,), (1,)), ((), ())), preferred_element_type=F32)


def _mod_kernel(c_ref, w_ref, b_ref, o_ref):
    s = _silu(c_ref[...])
    s_hi = s.astype(BF16)
    s_lo = (s - s_hi.astype(F32)).astype(BF16)
    w = w_ref[...]
    w_hi = w.astype(BF16)
    w_lo = (w - w_hi.astype(F32)).astype(BF16)
    o_ref[...] = _dot(s_hi, w_hi) + (_dot(s_lo, w_hi) + _dot(s_hi, w_lo)) + b_ref[...]


def _modulation(cvec, w_mod, b_mod):
    depth, d, n = w_mod.shape
    bn = 1024
    return pl.pallas_call(
        _mod_kernel,
        grid=(depth, n // bn),
        in_specs=[
            pl.BlockSpec((8, d), lambda i, j: (0, 0)),
            pl.BlockSpec((None, d, bn), lambda i, j: (i, 0, j)),
            pl.BlockSpec((None, 1, bn), lambda i, j: (i, 0, j)),
        ],
        out_specs=pl.BlockSpec((None, 8, bn), lambda i, j: (i, 0, j)),
        out_shape=jax.ShapeDtypeStruct((depth, 8, n), F32),
        compiler_params=_cparams(("parallel", "parallel")),
        name="modulation",
    )(cvec, w_mod, b_mod.reshape(depth, 1, n))


def _chunked(fn, n_rows, row0, inline):
    chunk = min(ROW_CHUNK, n_rows)
    if inline:
        for c in range(n_rows // chunk):
            fn(row0 + c * chunk)
        return

    def body(c, carry):
        fn(pl.multiple_of(row0 + c * chunk, chunk))
        return carry

    lax.fori_loop(0, n_rows // chunk, body, 0, unroll=CHUNK_UNROLL)


def _norm_mod_to(h_ref, g_ref, mod_ref, shift_row, scale_row, u_ref, row0=0, n_rows=None, inline=False):
    n_rows = h_ref.shape[0] if n_rows is None else n_rows
    chunk = min(ROW_CHUNK, n_rows)
    gs = g_ref[...] * (1.0 + mod_ref[scale_row:scale_row + 1, :])
    sh = mod_ref[shift_row:shift_row + 1, :]

    def one(r):
        h = h_ref[pl.ds(r, chunk), :]
        inv = lax.rsqrt(jnp.mean(h * h, axis=-1, keepdims=True) + EPS)
        u_ref[pl.ds(r, chunk), :] = ((h * inv) * gs + sh).astype(BF16)

    _chunked(one, n_rows, row0, inline)


def _post_residual_to(acc_ref, h_ref, g_ref, mod_ref, gate_row, o_ref, row0=0, n_rows=None, inline=False):
    n_rows = h_ref.shape[0] if n_rows is None else n_rows
    chunk = min(ROW_CHUNK, n_rows)
    gg = g_ref[...] * mod_ref[gate_row:gate_row + 1, :]

    def one(r):
        a = acc_ref[pl.ds(r, chunk), :]
        inv = lax.rsqrt(jnp.mean(a * a, axis=-1, keepdims=True) + EPS)
        o_ref[pl.ds(r, chunk), :] = h_ref[pl.ds(r, chunk), :] + (a * inv) * gg

    _chunked(one, n_rows, row0, inline)


def _rope(x, cos, sin, quarter):
    lane = lax.broadcasted_iota(jnp.int32, x.shape, 1)
    first = (lane % (2 * quarter)) < quarter
    rot = jnp.where(first, -pltpu.roll(x, LANES - quarter, 1), pltpu.roll(x, quarter, 1))
    return x * cos + rot * sin


def _emit_heads(u_ref, rows, w_ref, col0, n_heads, gain, scale, rope, o_ref, ocol0):
    h = 0
    while h < n_heads:
        nh = min(MXU_COLS // HEAD_DIM, n_heads - h)
        c = col0 + h * HEAD_DIM
        p = _dot(u_ref[rows, :], w_ref[:, c:c + nh * HEAD_DIM])
        for t in range(nh):
            x = p[:, t * HEAD_DIM:(t + 1) * HEAD_DIM]
            if gain is not None:
                x = _rms(x, gain)
                if rope is not None:
                    x = _rope(x, rope[0], rope[1], HEAD_DIM // 4)
                if scale != 1.0:
                    x = x * scale
            oc = ocol0 + (h + t) * HEAD_DIM
            o_ref[rows, oc:oc + HEAD_DIM] = x.astype(BF16)
        h += nh


def _inproj_ab_kernel(use_rope, h_ref, g_ref, mod_ref, w_ref, aqn_ref, akn_ref, bqn_ref, bkvn_ref,
                      wuq_ref, wuk_ref, wuv_ref, cosh_ref, sinh_ref, cosr_ref, sinr_ref,
                      qa_ref, ka_ref, va_ref, qb_ref, kb_ref, vb_ref, u_ref, c_ref):
    bm = h_ref.shape[0]
    rb = min(PROJ_ROW_BLOCK, bm)
    a_scale = HEAD_DIM ** -0.5 * LOG2E
    b_scale = (B_NOPE + B_ROPE) ** -0.5 * LOG2E
    for r0 in range(0, bm, rb):
        rows = slice(r0, r0 + rb)
        _norm_mod_to(h_ref, g_ref, mod_ref, 0, 1, u_ref, r0, rb, inline=True)
        rope_h = (cosh_ref[rows, :], sinh_ref[rows, :]) if use_rope else None
        _emit_heads(u_ref, rows, w_ref, 0, A_HEADS, aqn_ref[...], a_scale, rope_h, qa_ref, 0)
        _emit_heads(u_ref, rows, w_ref, A_HEADS * HEAD_DIM, A_KV_HEADS, akn_ref[...], 1.0, rope_h, ka_ref, 0)
        _emit_heads(u_ref, rows, w_ref, (A_HEADS + A_KV_HEADS) * HEAD_DIM, A_KV_HEADS, None, 1.0, None, va_ref, 0)

        cq = _dot(u_ref[rows, :], w_ref[:, A_IN:A_IN + B_Q_RANK])
        c_ref[rows, :] = _rms(cq, bqn_ref[...]).astype(BF16)
        for hd in range(B_HEADS):
            q = _dot(c_ref[rows, :], wuq_ref[:, hd * B_QK_PAD:(hd + 1) * B_QK_PAD])
            q_nope = q[:, :B_NOPE]
            q_rope = q[:, B_NOPE:]
            if use_rope:
                q_rope = _rope(q_rope, cosr_ref[rows, :], sinr_ref[rows, :], B_ROPE // 4)
            qb_ref[rows, hd * B_QK_PAD:hd * B_QK_PAD + B_NOPE] = (q_nope * b_scale).astype(BF16)
            qb_ref[rows, hd * B_QK_PAD + B_NOPE:(hd + 1) * B_QK_PAD] = (q_rope * b_scale).astype(BF16)
        kr0 = A_IN + B_Q_RANK + B_KV_RANK
        kr = _dot(u_ref[rows, :], w_ref[:, kr0:kr0 + LANES])
        if use_rope:
            kr = _rope(kr, cosr_ref[rows, :], sinr_ref[rows, :], B_ROPE // 4)
        kr = kr.astype(BF16)
        ckv = _dot(u_ref[rows, :], w_ref[:, A_IN + B_Q_RANK:A_IN + B_Q_RANK + B_KV_RANK])
        c_ref[rows, :] = _rms(ckv, bkvn_ref[...]).astype(BF16)
        for pair in range(B_HEADS * B_NOPE // MXU_COLS):
            kn = _dot(c_ref[rows, :], wuk_ref[:, pair * MXU_COLS:(pair + 1) * MXU_COLS])
            for t in range(MXU_COLS // B_NOPE):
                hd = pair * (MXU_COLS // B_NOPE) + t
                kb_ref[rows, hd * B_QK_PAD:hd * B_QK_PAD + B_NOPE] = kn[:, t * B_NOPE:(t + 1) * B_NOPE].astype(BF16)
                kb_ref[rows, hd * B_QK_PAD + B_NOPE:(hd + 1) * B_QK_PAD] = kr
            vb_ref[rows, pair * MXU_COLS:(pair + 1) * MXU_COLS] = _dot(
                c_ref[rows, :], wuv_ref[:, pair * MXU_COLS:(pair + 1) * MXU_COLS]).astype(BF16)


def _rope_specs(rope_tabs, n_tabs, bm, tiles_per_batch):
    if rope_tabs is None:
        return [jnp.zeros((bm, LANES), F32)] * n_tabs, [pl.BlockSpec((bm, LANES), lambda i: (0, 0))] * n_tabs
    return list(rope_tabs), [pl.BlockSpec((bm, LANES), lambda i: (i % tiles_per_batch, 0))] * n_tabs


def _inproj_ab(h, g_pre, mod, w, aqn, akn, bqn, bkvn, wuq, wuk, wuv, rope_tabs, rows_per_batch, bm):
    rows, d = h.shape
    tiles_per_batch = rows_per_batch // bm
    nb = mod.shape[0]

    def row_map(i):
        return (i, 0)

    def mod_map(i):
        return (i // tiles_per_batch if nb > 1 else 0, 0, 0)

    tabs, tab_specs = _rope_specs(rope_tabs, 4, bm, tiles_per_batch)
    outs = [
        (A_HEADS * HEAD_DIM), (A_KV_HEADS * HEAD_DIM), (A_KV_HEADS * HEAD_DIM),
        (B_HEADS * B_QK_PAD), (B_HEADS * B_QK_PAD), (B_HEADS * B_V),
    ]
    return pl.pallas_call(
        functools.partial(_inproj_ab_kernel, rope_tabs is not None),
        grid=(rows // bm,),
        in_specs=[
            pl.BlockSpec((bm, d), row_map),
            _resident((1, d)),
            pl.BlockSpec((None, 8, d), mod_map),
            _resident(w.shape),
            _resident((1, HEAD_DIM)), _resident((1, HEAD_DIM)),
            _resident((1, B_Q_RANK)), _resident((1, B_KV_RANK)),
            _resident(wuq.shape), _resident(wuk.shape), _resident(wuv.shape),
        ] + tab_specs,
        out_specs=[pl.BlockSpec((bm, n), row_map) for n in outs],
        out_shape=[jax.ShapeDtypeStruct((rows, n), BF16) for n in outs],
        scratch_shapes=[pltpu.VMEM((bm, d), BF16), pltpu.VMEM((bm, B_Q_RANK), BF16)],
        compiler_params=_cparams(("parallel",)),
        name="inproj_ab",
    )(h, g_pre, mod, w, aqn, akn, bqn, bkvn, wuq, wuk, wuv, *tabs)


def _inproj_cd_kernel(use_rope, h_ref, g_ref, mod_ref, w_ref, cqn_ref, ckn_ref, dqn_ref, dkn_ref,
                      cosh_ref, sinh_ref, q_ref, kv_ref, u_ref):
    bm = h_ref.shape[0]
    rb = min(PROJ_ROW_BLOCK, bm)
    scale = HEAD_DIM ** -0.5 * LOG2E
    kv_c = C_KV_HEADS * HEAD_DIM
    kv_d = D_HEADS * HEAD_DIM
    for r0 in range(0, bm, rb):
        rows = slice(r0, r0 + rb)
        _norm_mod_to(h_ref, g_ref, mod_ref, 0, 1, u_ref, r0, rb, inline=True)
        rope_h = (cosh_ref[rows, :], sinh_ref[rows, :]) if use_rope else None
        _emit_heads(u_ref, rows, w_ref, 0, C_HEADS, cqn_ref[...], scale, rope_h, q_ref, 0)
        _emit_heads(u_ref, rows, w_ref, C_HEADS * HEAD_DIM, C_KV_HEADS, ckn_ref[...], 1.0, rope_h, kv_ref, 2 * kv_d)
        _emit_heads(u_ref, rows, w_ref, C_HEADS * HEAD_DIM + kv_c, C_KV_HEADS, None, 1.0, None, kv_ref, 2 * kv_d + kv_c)
        _emit_heads(u_ref, rows, w_ref, C_IN, D_HEADS, dqn_ref[...], scale, None, q_ref, C_HEADS * HEAD_DIM)
        _emit_heads(u_ref, rows, w_ref, C_IN + kv_d, D_HEADS, dkn_ref[...], 1.0, None, kv_ref, 0)
        _emit_heads(u_ref, rows, w_ref, C_IN + 2 * kv_d, D_HEADS, None, 1.0, None, kv_ref, kv_d)


def _inproj_cd(h, g_pre, mod, w, cqn, ckn, dqn, dkn, rope_tabs, rows_per_batch, bm):
    rows, d = h.shape
    tiles_per_batch = rows_per_batch // bm
    nb = mod.shape[0]

    def row_map(i):
        return (i, 0)

    def mod_map(i):
        return (i // tiles_per_batch if nb > 1 else 0, 0, 0)

    tabs, tab_specs = _rope_specs(rope_tabs, 2, bm, tiles_per_batch)
    outs = [(C_HEADS + D_HEADS) * HEAD_DIM, 2 * (C_KV_HEADS + D_HEADS) * HEAD_DIM]
    return pl.pallas_call(
        functools.partial(_inproj_cd_kernel, rope_tabs is not None),
        grid=(rows // bm,),
        in_specs=[
            pl.BlockSpec((bm, d), row_map),
            _resident((1, d)),
            pl.BlockSpec((None, 8, d), mod_map),
            _resident(w.shape),
            _resident((1, HEAD_DIM)), _resident((1, HEAD_DIM)), _resident((1, HEAD_DIM)), _resident((1, HEAD_DIM)),
        ] + tab_specs,
        out_specs=[pl.BlockSpec((bm, n), row_map) for n in outs],
        out_shape=[jax.ShapeDtypeStruct((rows, n), BF16) for n in outs],
        scratch_shapes=[pltpu.VMEM((bm, d), BF16)],
        compiler_params=_cparams(("parallel",)),
        name="inproj_cd",
    )(h, g_pre, mod, w, cqn, ckn, dqn, dkn, *tabs)


def _flash_step(q_ref, kt, vt, m_ref, acc_ref, rep, dq, sub):
    bq = q_ref.shape[0]
    n_chunks = kt.shape[0] // LANES
    for r in range(rep):
        for sb in range(bq // sub):
            rows = slice(sb * sub, (sb + 1) * sub)
            mcol = slice(r * LANES, (r + 1) * LANES)
            acol = slice(r * 2 * LANES, (r + 1) * 2 * LANES)
            s = _dot_nt(q_ref[rows, r * dq:(r + 1) * dq], kt)
            chunks = [s[:, c * LANES:(c + 1) * LANES] for c in range(n_chunks)]
            lane_max = functools.reduce(jnp.maximum, chunks)
            m_prev = m_ref[rows, mcol]
            m_new = jnp.maximum(m_prev, jnp.max(lane_max, axis=-1, keepdims=True))
            alpha = jnp.exp2(m_prev - m_new)
            p = jnp.concatenate([jnp.exp2(ch - m_new) for ch in chunks], axis=1).astype(BF16)
            acc_ref[rows, acol] = jnp.concatenate([alpha, alpha], axis=1) * acc_ref[rows, acol] + _dot(p, vt)
            m_ref[rows, mcol] = m_new


def _flash_kernel(rep, dq, has_ctx, sub, *refs):
    if has_ctx:
        q_ref, k_ref, v_ref, kc_ref, vc_ref, o_ref, m_ref, acc_ref = refs
    else:
        q_ref, k_ref, v_ref, o_ref, m_ref, acc_ref = refs
    j = pl.program_id(3)

    def with_ones(v):
        return jnp.concatenate([v, jnp.ones(v.shape, BF16)], axis=1)

    @pl.when(j == 0)
    def _init():
        m_ref[...] = jnp.full(m_ref.shape, NEG, F32)
        acc_ref[...] = jnp.zeros(acc_ref.shape, F32)
        if has_ctx:
            _flash_step(q_ref, kc_ref[...], with_ones(vc_ref[...]), m_ref, acc_ref, rep, dq, sub)

    _flash_step(q_ref, k_ref[...], with_ones(v_ref[...]), m_ref, acc_ref, rep, dq, sub)

    @pl.when(j == pl.num_programs(3) - 1)
    def _fin():
        for r in range(rep):
            c0 = r * 2 * LANES
            o_ref[:, r * HEAD_DIM:(r + 1) * HEAD_DIM] = (
                acc_ref[:, c0:c0 + LANES] / acc_ref[:, c0 + LANES:c0 + 2 * LANES]).astype(BF16)


def _flash(q, k, v, kc, vc, batch, n_groups, rep, dq, bq, bk, sub):
    lq = q.shape[0] // batch
    lk = k.shape[0] // batch
    nq, nk = lq // bq, lk // bk
    has_ctx = kc is not None
    dv = HEAD_DIM
    assert dv == LANES
    in_specs = [
        pl.BlockSpec((bq, rep * dq), lambda b, g, i, j: (b * nq + i, g)),
        pl.BlockSpec((bk, dq), lambda b, g, i, j: (b * nk + j, g)),
        pl.BlockSpec((bk, dv), lambda b, g, i, j: (b * nk + j, g)),
    ]
    args = [q, k, v]
    if has_ctx:
        lc = kc.shape[0] // batch
        in_specs += [
            pl.BlockSpec((lc, dq), lambda b, g, i, j: (b, g)),
            pl.BlockSpec((lc, dv), lambda b, g, i, j: (b, g)),
        ]
        args += [kc, vc]
    return pl.pallas_call(
        functools.partial(_flash_kernel, rep, dq, has_ctx, min(sub, bq)),
        grid=(batch, n_groups, nq, nk),
        in_specs=in_specs,
        out_specs=pl.BlockSpec((bq, rep * dv), lambda b, g, i, j: (b * nq + i, g)),
        out_shape=jax.ShapeDtypeStruct((batch * lq, n_groups * rep * dv), BF16),
        scratch_shapes=[
            pltpu.VMEM((bq, rep * LANES), F32),
            pltpu.VMEM((bq, rep * 2 * LANES), F32),
        ],
        compiler_params=_cparams(("parallel", "parallel", "parallel", "arbitrary")),
        name="flash_attention",
    )(*args)


def _band_kernel(n_q, n_kv, per_head_bias, has_sink, *refs):
    if has_sink:
        q_ref, kvp_ref, kvx_ref, kvn_ref, kvc_ref, bias_ref, sink_ref, o_ref = refs
    else:
        q_ref, kvp_ref, kvx_ref, kvn_ref, kvc_ref, bias_ref, o_ref = refs
    t = BAND_TILE
    for r in range(n_q):
        hcol = slice(r * HEAD_DIM, (r + 1) * HEAD_DIM)
        kv_head = r * n_kv // n_q
        kcol = slice(kv_head * HEAD_DIM, (kv_head + 1) * HEAD_DIM)
        vcol = slice((n_kv + kv_head) * HEAD_DIM, (n_kv + kv_head + 1) * HEAD_DIM)
        bias = bias_ref.at[r] if per_head_bias else bias_ref
        q = q_ref[:, hcol]
        scores = [_dot_nt(q, kvc_ref[:, kcol])]
        for n, kv_ref in enumerate((kvp_ref, kvx_ref, kvn_ref)):
            scores.append(_dot_nt(q, kv_ref[:, kcol]) + bias[:, n * t:(n + 1) * t])
        chunks = [s[:, c * LANES:(c + 1) * LANES] for s in scores for c in range(s.shape[1] // LANES)]
        m = jnp.max(functools.reduce(jnp.maximum, chunks), axis=-1, keepdims=True)
        if has_sink:
            sink = sink_ref[r, 0:1, 0:1]
            m = jnp.maximum(m, sink)
        probs = [jnp.exp2(s - m) for s in scores]
        pch = [p[:, c * LANES:(c + 1) * LANES] for p in probs for c in range(p.shape[1] // LANES)]
        l = jnp.sum(functools.reduce(jnp.add, pch), axis=-1, keepdims=True)
        if has_sink:
            l = l + jnp.exp2(sink - m)
        acc = _dot(probs[0].astype(BF16), kvc_ref[:, vcol])
        for p, kv_ref in zip(probs[1:], (kvp_ref, kvx_ref, kvn_ref)):
            acc = acc + _dot(p.astype(BF16), kv_ref[:, vcol])
        o_ref[:, hcol] = (acc / l).astype(BF16)


def _band_attention(q, kv, kvc, bias, sink, batch, q_col0, kv_col0, n_kv):
    t = BAND_TILE
    n_q = 8
    seq = q.shape[0] // batch
    nq = seq // t
    lc = kvc.shape[0] // batch
    qw = n_q * HEAD_DIM
    kw = 2 * n_kv * HEAD_DIM
    has_sink = sink is not None
    per_head_bias = bias.ndim == 4

    def tile_map(off):
        def f(b, i):
            return (b * nq + jnp.clip(i + off, 0, nq - 1), kv_col0 // kw)
        return f

    def cls(i):
        return jnp.where(i == 0, 0, jnp.where(i == nq - 1, 2, 1))

    if per_head_bias:
        bias_spec = pl.BlockSpec((n_q, None, t, 3 * t), lambda b, i: (0, cls(i), 0, 0))
    else:
        bias_spec = pl.BlockSpec((None, t, 3 * t), lambda b, i: (cls(i), 0, 0))
    in_specs = [pl.BlockSpec((t, qw), lambda b, i: (b * nq + i, q_col0 // qw))]
    in_specs += [pl.BlockSpec((t, kw), tile_map(off)) for off in (-1, 0, 1)]
    in_specs += [pl.BlockSpec((lc, kw), lambda b, i: (b, kv_col0 // kw)), bias_spec]
    args = [q, kv, kv, kv, kvc, bias]
    if has_sink:
        in_specs.append(pl.BlockSpec((n_q, 8, LANES), lambda b, i: (0, 0, 0)))
        args.append(sink)
    return pl.pallas_call(
        functools.partial(_band_kernel, n_q, n_kv, per_head_bias, has_sink),
        grid=(batch, nq),
        in_specs=in_specs,
        out_specs=pl.BlockSpec((t, qw), lambda b, i: (b * nq + i, 0)),
        out_shape=jax.ShapeDtypeStruct((batch * seq, qw), BF16),
        compiler_params=_cparams(("parallel", "parallel")),
        name="band_attention",
    )(*args)


def _outproj_kernel(n_in, *refs):
    a_refs = refs[:n_in]
    w_ref, g_ref, mod_ref, h_ref, o_ref, acc_ref = refs[n_in:]
    bm = h_ref.shape[0]
    rb = min(PROJ_ROW_BLOCK, bm)
    for r0 in range(0, bm, rb):
        k0 = 0
        acc = None
        for a_ref in a_refs:
            kk = a_ref.shape[1]
            part = _dot(a_ref[r0:r0 + rb, :], w_ref[k0:k0 + kk, :])
            acc = part if acc is None else acc + part
            k0 += kk
        acc_ref[r0:r0 + rb, :] = acc
        _post_residual_to(acc_ref, h_ref, g_ref, mod_ref, 2, o_ref, r0, rb, inline=True)


def _outproj(a_list, w, g_post, mod, h, rows_per_batch, bm):
    rows, d = h.shape
    tiles_per_batch = rows_per_batch // bm
    nb = mod.shape[0]

    def row_map(i):
        return (i, 0)

    def mod_map(i):
        return (i // tiles_per_batch if nb > 1 else 0, 0, 0)

    return pl.pallas_call(
        functools.partial(_outproj_kernel, len(a_list)),
        grid=(rows // bm,),
        in_specs=[pl.BlockSpec((bm, a.shape[1]), row_map) for a in a_list] + [
            _resident(w.shape),
            _resident((1, d)),
            pl.BlockSpec((None, 8, d), mod_map),
            pl.BlockSpec((bm, d), row_map),
        ],
        out_specs=pl.BlockSpec((bm, d), row_map),
        out_shape=jax.ShapeDtypeStruct((rows, d), F32),
        scratch_shapes=[pltpu.VMEM((bm, d), F32)],
        compiler_params=_cparams(("parallel",)),
        name="outproj",
    )(*a_list, w, g_post, mod, h)


def _ffn_kernel(h_ref, gpre_ref, gpost_ref, mod_ref, wg_ref, wu_ref, wd_ref, o_ref, u_ref):
    f = pl.program_id(1)
    last = pl.num_programs(1) - 1
    bm = h_ref.shape[0]
    rb = min(FFN_ROW_BLOCK, bm)

    def partial_out(rows):
        u = u_ref[rows, :]
        act = (_silu(_dot(u, wg_ref[...])) * _dot(u, wu_ref[...])).astype(BF16)
        return _dot(act, wd_ref[...])

    @pl.when(f == 0)
    def _first():
        for r0 in range(0, bm, rb):
            _norm_mod_to(h_ref, gpre_ref, mod_ref, 3, 4, u_ref, r0, rb, inline=True)
            o_ref[r0:r0 + rb, :] = partial_out(slice(r0, r0 + rb))

    @pl.when(jnp.logical_and(f > 0, f < last))
    def _mid():
        o_ref[...] += partial_out(slice(None))

    @pl.when(f == last)
    def _last():
        for r0 in range(0, bm, rb):
            o_ref[r0:r0 + rb, :] += partial_out(slice(r0, r0 + rb))
            _post_residual_to(o_ref, h_ref, gpost_ref, mod_ref, 5, o_ref, r0, rb, inline=True)


def _cast_kernel(x_ref, o_ref):
    o_ref[...] = x_ref[...].astype(BF16)


def _to_bf16(w, br):
    depth, r, c = w.shape
    assert r % br == 0
    spec = pl.BlockSpec((None, br, c), lambda l, i: (l, i, 0))
    return pl.pallas_call(
        _cast_kernel,
        grid=(depth, r // br),
        in_specs=[spec],
        out_specs=spec,
        out_shape=jax.ShapeDtypeStruct(w.shape, BF16),
        compiler_params=_cparams(("parallel", "parallel")),
        name="cast_bf16",
    )(w)


def _ffn(h, g_pre, g_post, mod, wg, wu, wd, layer, rows_per_batch, bm, bf):
    rows, d = h.shape
    dff = wg.shape[2]
    tiles_per_batch = rows_per_batch // bm
    nb = mod.shape[0]

    def row_map(i, f):
        return (i, 0)

    def mod_map(i, f):
        return (i // tiles_per_batch if nb > 1 else 0, 0, 0)

    assert rows % bm == 0 and dff % bf == 0 and dff // bf >= 2
    return pl.pallas_call(
        _ffn_kernel,
        grid=(rows // bm, dff // bf),
        in_specs=[
            pl.BlockSpec((bm, d), row_map),
            _resident((1, d)), _resident((1, d)),
            pl.BlockSpec((None, 8, d), mod_map),
            pl.BlockSpec((None, d, bf), lambda i, f: (layer, 0, f)),
            pl.BlockSpec((None, d, bf), lambda i, f: (layer, 0, f)),
            pl.BlockSpec((None, bf, d), lambda i, f: (layer, f, 0)),
        ],
        out_specs=pl.BlockSpec((bm, d), row_map),
        out_shape=jax.ShapeDtypeStruct((rows, d), F32),
        scratch_shapes=[pltpu.VMEM((bm, d), BF16)],
        compiler_params=_cparams(("parallel", "arbitrary")),
        name="ffn",
    )(h, g_pre, g_post, mod, wg, wu, wd)


def _rope_table(n_tok, dim):
    t = jnp.arange(n_tok, dtype=jnp.int32)
    row = (t // GRID_W).astype(F32)
    col = (t % GRID_W).astype(F32)
    half = dim // 2
    inv_freq = ROPE_BASE ** (-jnp.arange(0, half, 2, dtype=F32) / half)
    ar = row[:, None] * inv_freq[None, :]
    ac = col[:, None] * inv_freq[None, :]
    ang = jnp.concatenate([ar, ar, ac, ac], axis=-1)
    return jnp.cos(ang), jnp.sin(ang)


def _rpb_cols_kernel(r_ref, e_ref, o_ref):
    x = r_ref[...] * LOG2E
    hi = x.astype(BF16)
    r1 = x - hi.astype(F32)
    mid = r1.astype(BF16)
    lo = (r1 - mid.astype(F32)).astype(BF16)
    e = e_ref[...]
    o_ref[...] = (_dot(hi, e) + _dot(mid, e)) + _dot(lo, e)


def _band_bias(rpb, seq):
    t, g = BAND_TILE, GRID_W
    rpt = t // g
    nq = seq // t
    rows = seq // g
    kh = min(NA_KH, rows)
    n_h, n_di, n_dj = rpb.shape
    tiles = (0, 1, nq - 1)
    neg = NEG * LOG2E

    qpos = np.array(tiles)[:, None, None] * t + np.arange(t)[None, :, None]
    kpos = (np.array(tiles)[:, None, None] - 1) * t + np.arange(3 * t)[None, None, :]
    win = (np.abs(kpos - qpos) <= C_WINDOW) & (kpos >= 0) & (kpos < seq)
    bias_c = jnp.asarray(np.where(win, 0.0, neg).astype(np.float32))

    qc = np.arange(g)[:, None]
    kc = np.arange(g)[None, :]
    cs = np.clip(qc - NA_KW // 2, 0, g - NA_KW)
    col_ok = (kc >= cs) & (kc < cs + NA_KW)
    onehot = np.zeros((LANES, g, g), np.float32)
    qq, kk = np.nonzero(col_ok)
    onehot[(kk - qq + NA_KW - 1), qq, kk] = 1.0
    assert n_h * n_di <= LANES and n_dj <= LANES
    rpb2d = jnp.pad(rpb.reshape(n_h * n_di, n_dj).astype(F32), ((0, LANES - n_h * n_di), (0, LANES - n_dj)))
    cols = pl.pallas_call(
        _rpb_cols_kernel,
        out_shape=jax.ShapeDtypeStruct((LANES, g * g), F32),
        name="rpb_columns",
    )(rpb2d, jnp.asarray(onehot.reshape(LANES, g * g), BF16))
    cols = cols[:n_h * n_di].reshape(n_h, n_di, g, g) + jnp.asarray(np.where(col_ok, 0.0, neg).astype(np.float32))
    masked = jnp.full((n_h, g, g), neg, F32)

    tile_rows = []
    for ti in tiles:
        for a in range(rpt):
            qr = ti * rpt + a
            rs = min(max(qr - kh // 2, 0), rows - kh)
            krs = [(ti - 1) * rpt + b for b in range(3 * rpt)]
            tile_rows.append(jnp.concatenate(
                [cols[:, kr - qr + NA_KH - 1] if rs <= kr < rs + kh else masked for kr in krs], axis=-1))
    bias_d = jnp.stack(tile_rows, axis=1).reshape(n_h, len(tiles), t, 3 * t)
    return bias_c, bias_d


def _pad_mod(m):
    nb = m.shape[0]
    return jnp.pad(m.reshape(nb, 6, D_MODEL), ((0, 0), (0, 2), (0, 0)))


def _row(v):
    return v.reshape(1, -1).astype(F32)


def _tiles(batch, seq, n_ctx):
    return dict(
        bm_lat=512,
        bm_ctx=min(512, batch * n_ctx),
        bm_ffn=1024 if seq % 1024 == 0 else 512,
        bf_ffn=512,
        bq_a=min(1024, seq),
        bq_b=min(4096, seq),
        bk=min(2048, seq),
        sub=256,
    )


def kernel(x, c, ctx, c_ctx, w_mod, b_mod, g_mix_pre, g_mix_post, g_ffn_pre, g_ffn_post, w_gate, w_up, w_down,
           ab_w_in, ab_w_out, a_q_norm, a_k_norm, b_q_norm, b_kv_norm, b_w_uq, b_w_ukv, cd_w_in, cd_w_out,
           c_q_norm, c_k_norm, c_sink, d_q_norm, d_k_norm, d_rpb):
    batch, seq, d = x.shape
    n_ctx = ctx.shape[1]
    depth = w_mod.shape[0]
    assert d == D_MODEL and seq % 512 == 0 and n_ctx % 256 == 0 and batch + 1 <= 8
    assert depth == 2

    cvec = jnp.zeros((8, d), F32).at[:batch].set(c).at[batch].set(c_ctx)
    mod_all = _modulation(cvec, w_mod, b_mod)

    cos_h, sin_h = _rope_table(seq, HEAD_DIM)
    cos_r, sin_r = _rope_table(seq, B_ROPE)
    pad_r = ((0, 0), (0, LANES - B_ROPE))
    tabs_ab = (cos_h, sin_h, jnp.pad(cos_r, pad_r), jnp.pad(sin_r, pad_r))
    tabs_cd = (cos_h, sin_h)

    tl = _tiles(batch, seq, n_ctx)
    bm_lat, bm_ctx, bm_ffn, bf_ffn = tl["bm_lat"], tl["bm_ctx"], tl["bm_ffn"], tl["bf_ffn"]
    bq_a, bq_b, bk, sub = tl["bq_a"], tl["bq_b"], tl["bk"], tl["sub"]

    h_lat = x.reshape(batch * seq, d)
    h_ctx = ctx.reshape(batch * n_ctx, d)
    wg, wu, wd = _to_bf16(w_gate, 256), _to_bf16(w_up, 256), _to_bf16(w_down, 512)

    for i in range(depth):
        last = i == depth - 1
        need_ctx = not last
        j = i // 2
        mod_lat = _pad_mod(mod_all[i, :batch])
        mod_ctx = _pad_mod(mod_all[i, batch:batch + 1])
        g1, g2, g3, g4 = _row(g_mix_pre[i]), _row(g_mix_post[i]), _row(g_ffn_pre[i]), _row(g_ffn_post[i])

        if i % 2 == 0:
            w_in = ab_w_in[j]
            kr_cols = jnp.pad(w_in[:, A_IN + B_Q_RANK + B_KV_RANK:], ((0, 0), (0, LANES - B_ROPE)))
            w_ab = jnp.concatenate([w_in[:, :A_IN + B_Q_RANK + B_KV_RANK], kr_cols], axis=1).astype(BF16)
            wuq = jnp.pad(b_w_uq[j].reshape(B_Q_RANK, B_HEADS, B_NOPE + B_ROPE),
                          ((0, 0), (0, 0), (0, B_QK_PAD - B_NOPE - B_ROPE))).reshape(B_Q_RANK, B_HEADS * B_QK_PAD)
            wukv = b_w_ukv[j].reshape(B_KV_RANK, B_HEADS, B_NOPE + B_V)
            wuk = wukv[:, :, :B_NOPE].reshape(B_KV_RANK, B_HEADS * B_NOPE)
            wuv = wukv[:, :, B_NOPE:].reshape(B_KV_RANK, B_HEADS * B_V)
            norms = (_row(a_q_norm[j]), _row(a_k_norm[j]), _row(b_q_norm[j]), _row(b_kv_norm[j]),
                     wuq.astype(BF16), wuk.astype(BF16), wuv.astype(BF16))
            qa, ka, va, qb, kb, vb = _inproj_ab(h_lat, g1, mod_lat, w_ab, *norms, tabs_ab, seq, bm_lat)
            qac, kac, vac, qbc, kbc, vbc = _inproj_ab(h_ctx, g1, mod_ctx, w_ab, *norms, None, batch * n_ctx, bm_ctx)
            rep_a = A_HEADS // A_KV_HEADS
            oa = _flash(qa, ka, va, kac, vac, batch, A_KV_HEADS, rep_a, HEAD_DIM, bq_a, bk, sub)
            ob = _flash(qb, kb, vb, kbc, vbc, batch, B_HEADS, 1, B_QK_PAD, bq_b, bk, sub)
            w_out = ab_w_out[j].astype(BF16)
            attn_lat = [oa, ob]
            if need_ctx:
                oac = _flash(qac, kac, vac, None, None, batch, A_KV_HEADS, rep_a, HEAD_DIM, n_ctx, n_ctx, sub)
                obc = _flash(qbc, kbc, vbc, None, None, batch, B_HEADS, 1, B_QK_PAD, n_ctx, n_ctx, sub)
                attn_ctx = [oac, obc]
        else:
            w_cd = cd_w_in[j].astype(BF16)
            norms = (_row(c_q_norm[j]), _row(c_k_norm[j]), _row(d_q_norm[j]), _row(d_k_norm[j]))
            q_l, kv_l = _inproj_cd(h_lat, g1, mod_lat, w_cd, *norms, tabs_cd, seq, bm_lat)
            _, kv_c = _inproj_cd(h_ctx, g1, mod_ctx, w_cd, *norms, None, batch * n_ctx, bm_ctx)
            bias_c, bias_d = _band_bias(d_rpb[j], seq)
            sink = jnp.broadcast_to((c_sink[j].astype(F32) * LOG2E)[:, None, None], (C_HEADS, 8, LANES))
            kv_d = 2 * D_HEADS * HEAD_DIM
            attn_lat = [
                _band_attention(q_l, kv_l, kv_c, bias_c, sink, batch, 0, kv_d, C_KV_HEADS),
                _band_attention(q_l, kv_l, kv_c, bias_d, None, batch, C_HEADS * HEAD_DIM, 0, D_HEADS),
            ]
            w_out = cd_w_out[j].astype(BF16)

        h_lat = _outproj(attn_lat, w_out, g2, mod_lat, h_lat, seq, bm_lat)
        h_lat = _ffn(h_lat, g3, g4, mod_lat, wg, wu, wd, i, seq, bm_ffn, bf_ffn)
        if need_ctx:
            h_ctx = _outproj(attn_ctx, w_out, g2, mod_ctx, h_ctx, batch * n_ctx, bm_ctx)
            h_ctx = _ffn(h_ctx, g3, g4, mod_ctx, wg, wu, wd, i, batch * n_ctx, bm_ctx, bf_ffn)
    return h_lat.reshape(batch, seq, d)
```

```python
import functools

import jax
import jax.numpy as jnp
import numpy as np
from jax import lax
from jax.experimental import pallas as pl
from jax.experimental.pallas import tpu as pltpu

F32 = jnp.float32
BF16 = jnp.bfloat16

D_MODEL = 2048
GRID_W = 64
HEAD_DIM = 128
ROPE_BASE = 10000.0
EPS = 1e-6
NEG = -1e30
LOG2E = 1.4426950408889634

A_HEADS, A_KV_HEADS = 8, 2
B_HEADS, B_Q_RANK, B_KV_RANK, B_NOPE, B_ROPE, B_V = 8, 512, 512, 128, 64, 128
C_HEADS, C_KV_HEADS, C_WINDOW = 8, 2, 128
D_HEADS, NA_KH, NA_KW = 8, 8, 16
A_IN = (A_HEADS + 2 * A_KV_HEADS) * HEAD_DIM
C_IN = (C_HEADS + 2 * C_KV_HEADS) * HEAD_DIM
D_FF = 5632

LANES = 128
MXU_COLS = 256
B_QK_PAD = 256
BAND_TILE = 256
VMEM_LIMIT = 56 * 1024 * 1024
ROW_CHUNK = 32
PROJ_ROW_BLOCK = 256
FFN_ROW_BLOCK = 256
CHUNK_UNROLL = 4


def _cparams(sem):
    return pltpu.CompilerParams(dimension_semantics=sem, vmem_limit_bytes=VMEM_LIMIT)


def _resident(shape):
    nd = len(shape)
    return pl.BlockSpec(shape, lambda *_: (0,) * nd, pipeline_mode=pl.Buffered(1))


def _silu(x):
    return x * (1.0 / (1.0 + jnp.exp(-x)))


def _rms(x, g):
    ms = jnp.mean(x * x, axis=-1, keepdims=True)
    return x * lax.rsqrt(ms + EPS) * g


def _dot(a, b):
    return jnp.dot(a, b, preferred_element_type=F32)


def _dot_nt(a, b):
    return lax.dot_general(a, b, (((1,), (1,)), ((), ())), preferred_element_type=F32)


def _mod_kernel(c_ref, w_ref, b_ref, o_ref):
    s = _silu(c_ref[...])
    s_hi = s.astype(BF16)
    s_lo = (s - s_hi.astype(F32)).astype(BF16)
    w = w_ref[...]
    w_hi = w.astype(BF16)
    w_lo = (w - w_hi.astype(F32)).astype(BF16)
    o_ref[...] = _dot(s_hi, w_hi) + (_dot(s_lo, w_hi) + _dot(s_hi, w_lo)) + b_ref[...]


def _modulation(cvec, w_mod, b_mod):
    depth, d, n = w_mod.shape
    bn = 1024
    return pl.pallas_call(
        _mod_kernel,
        grid=(depth, n // bn),
        in_specs=[
            pl.BlockSpec((8, d), lambda i, j: (0, 0)),
            pl.BlockSpec((None, d, bn), lambda i, j: (i, 0, j)),
            pl.BlockSpec((None, 1, bn), lambda i, j: (i, 0, j)),
        ],
        out_specs=pl.BlockSpec((None, 8, bn), lambda i, j: (i, 0, j)),
        out_shape=jax.ShapeDtypeStruct((depth, 8, n), F32),
        compiler_params=_cparams(("parallel", "parallel")),
        name="modulation",
    )(cvec, w_mod, b_mod.reshape(depth, 1, n))


def _chunked(fn, n_rows, row0, inline):
    chunk = min(ROW_CHUNK, n_rows)
    if inline:
        for c in range(n_rows // chunk):
            fn(row0 + c * chunk)
        return

    def body(c, carry):
        fn(pl.multiple_of(row0 + c * chunk, chunk))
        return carry

    lax.fori_loop(0, n_rows // chunk, body, 0, unroll=CHUNK_UNROLL)


def _norm_mod_to(h_ref, g_ref, mod_ref, shift_row, scale_row, u_ref, row0=0, n_rows=None, inline=False):
    n_rows = h_ref.shape[0] if n_rows is None else n_rows
    chunk = min(ROW_CHUNK, n_rows)
    gs = g_ref[...] * (1.0 + mod_ref[scale_row:scale_row + 1, :])
    sh = mod_ref[shift_row:shift_row + 1, :]

    def one(r):
        h = h_ref[pl.ds(r, chunk), :]
        inv = lax.rsqrt(jnp.mean(h * h, axis=-1, keepdims=True) + EPS)
        u_ref[pl.ds(r, chunk), :] = ((h * inv) * gs + sh).astype(BF16)

    _chunked(one, n_rows, row0, inline)


def _post_residual_to(acc_ref, h_ref, g_ref, mod_ref, gate_row, o_ref, row0=0, n_rows=None, inline=False):
    n_rows = h_ref.shape[0] if n_rows is None else n_rows
    chunk = min(ROW_CHUNK, n_rows)
    gg = g_ref[...] * mod_ref[gate_row:gate_row + 1, :]

    def one(r):
        a = acc_ref[pl.ds(r, chunk), :]
        inv = lax.rsqrt(jnp.mean(a * a, axis=-1, keepdims=True) + EPS)
        o_ref[pl.ds(r, chunk), :] = h_ref[pl.ds(r, chunk), :] + (a * inv) * gg

    _chunked(one, n_rows, row0, inline)


def _rope(x, cos, sin, quarter):
    lane = lax.broadcasted_iota(jnp.int32, x.shape, 1)
    first = (lane % (2 * quarter)) < quarter
    rot = jnp.where(first, -pltpu.roll(x, LANES - quarter, 1), pltpu.roll(x, quarter, 1))
    return x * cos + rot * sin


def _emit_heads(u_ref, rows, w_ref, col0, n_heads, gain, scale, rope, o_ref, ocol0):
    h = 0
    while h < n_heads:
        nh = min(MXU_COLS // HEAD_DIM, n_heads - h)
        c = col0 + h * HEAD_DIM
        p = _dot(u_ref[rows, :], w_ref[:, c:c + nh * HEAD_DIM])
        for t in range(nh):
            x = p[:, t * HEAD_DIM:(t + 1) * HEAD_DIM]
            if gain is not None:
                x = _rms(x, gain)
                if rope is not None:
                    x = _rope(x, rope[0], rope[1], HEAD_DIM // 4)
                if scale != 1.0:
                    x = x * scale
            oc = ocol0 + (h + t) * HEAD_DIM
            o_ref[rows, oc:oc + HEAD_DIM] = x.astype(BF16)
        h += nh


def _inproj_ab_kernel(use_rope, h_ref, g_ref, mod_ref, w_ref, aqn_ref, akn_ref, bqn_ref, bkvn_ref,
                      wuq_ref, wuk_ref, wuv_ref, cosh_ref, sinh_ref, cosr_ref, sinr_ref,
                      qa_ref, ka_ref, va_ref, qb_ref, kb_ref, vb_ref, u_ref, c_ref):
    bm = h_ref.shape[0]
    rb = min(PROJ_ROW_BLOCK, bm)
    a_scale = HEAD_DIM ** -0.5 * LOG2E
    b_scale = (B_NOPE + B_ROPE) ** -0.5 * LOG2E
    for r0 in range(0, bm, rb):
        rows = slice(r0, r0 + rb)
        _norm_mod_to(h_ref, g_ref, mod_ref, 0, 1, u_ref, r0, rb, inline=True)
        rope_h = (cosh_ref[rows, :], sinh_ref[rows, :]) if use_rope else None
        _emit_heads(u_ref, rows, w_ref, 0, A_HEADS, aqn_ref[...], a_scale, rope_h, qa_ref, 0)
        _emit_heads(u_ref, rows, w_ref, A_HEADS * HEAD_DIM, A_KV_HEADS, akn_ref[...], 1.0, rope_h, ka_ref, 0)
        _emit_heads(u_ref, rows, w_ref, (A_HEADS + A_KV_HEADS) * HEAD_DIM, A_KV_HEADS, None, 1.0, None, va_ref, 0)

        cq = _dot(u_ref[rows, :], w_ref[:, A_IN:A_IN + B_Q_RANK])
        c_ref[rows, :] = _rms(cq, bqn_ref[...]).astype(BF16)
        for hd in range(B_HEADS):
            q = _dot(c_ref[rows, :], wuq_ref[:, hd * B_QK_PAD:(hd + 1) * B_QK_PAD])
            q_nope = q[:, :B_NOPE]
            q_rope = q[:, B_NOPE:]
            if use_rope:
                q_rope = _rope(q_rope, cosr_ref[rows, :], sinr_ref[rows, :], B_ROPE // 4)
            qb_ref[rows, hd * B_QK_PAD:hd * B_QK_PAD + B_NOPE] = (q_nope * b_scale).astype(BF16)
            qb_ref[rows, hd * B_QK_PAD + B_NOPE:(hd + 1) * B_QK_PAD] = (q_rope * b_scale).astype(BF16)
        kr0 = A_IN + B_Q_RANK + B_KV_RANK
        kr = _dot(u_ref[rows, :], w_ref[:, kr0:kr0 + LANES])
        if use_rope:
            kr = _rope(kr, cosr_ref[rows, :], sinr_ref[rows, :], B_ROPE // 4)
        kr = kr.astype(BF16)
        ckv = _dot(u_ref[rows, :], w_ref[:, A_IN + B_Q_RANK:A_IN + B_Q_RANK + B_KV_RANK])
        c_ref[rows, :] = _rms(ckv, bkvn_ref[...]).astype(BF16)
        for pair in range(B_HEADS * B_NOPE // MXU_COLS):
            kn = _dot(c_ref[rows, :], wuk_ref[:, pair * MXU_COLS:(pair + 1) * MXU_COLS])
            for t in range(MXU_COLS // B_NOPE):
                hd = pair * (MXU_COLS // B_NOPE) + t
                kb_ref[rows, hd * B_QK_PAD:hd * B_QK_PAD + B_NOPE] = kn[:, t * B_NOPE:(t + 1) * B_NOPE].astype(BF16)
                kb_ref[rows, hd * B_QK_PAD + B_NOPE:(hd + 1) * B_QK_PAD] = kr
            vb_ref[rows, pair * MXU_COLS:(pair + 1) * MXU_COLS] = _dot(
                c_ref[rows, :], wuv_ref[:, pair * MXU_COLS:(pair + 1) * MXU_COLS]).astype(BF16)


def _rope_specs(rope_tabs, n_tabs, bm, tiles_per_batch):
    if rope_tabs is None:
        return [jnp.zeros((bm, LANES), F32)] * n_tabs, [pl.BlockSpec((bm, LANES), lambda i: (0, 0))] * n_tabs
    return list(rope_tabs), [pl.BlockSpec((bm, LANES), lambda i: (i % tiles_per_batch, 0))] * n_tabs


def _inproj_ab(h, g_pre, mod, w, aqn, akn, bqn, bkvn, wuq, wuk, wuv, rope_tabs, rows_per_batch, bm):
    rows, d = h.shape
    tiles_per_batch = rows_per_batch // bm
    nb = mod.shape[0]

    def row_map(i):
        return (i, 0)

    def mod_map(i):
        return (i // tiles_per_batch if nb > 1 else 0, 0, 0)

    tabs, tab_specs = _rope_specs(rope_tabs, 4, bm, tiles_per_batch)
    outs = [
        (A_HEADS * HEAD_DIM), (A_KV_HEADS * HEAD_DIM), (A_KV_HEADS * HEAD_DIM),
        (B_HEADS * B_QK_PAD), (B_HEADS * B_QK_PAD), (B_HEADS * B_V),
    ]
    return pl.pallas_call(
        functools.partial(_inproj_ab_kernel, rope_tabs is not None),
        grid=(rows // bm,),
        in_specs=[
            pl.BlockSpec((bm, d), row_map),
            _resident((1, d)),
            pl.BlockSpec((None, 8, d), mod_map),
            _resident(w.shape),
            _resident((1, HEAD_DIM)), _resident((1, HEAD_DIM)),
            _resident((1, B_Q_RANK)), _resident((1, B_KV_RANK)),
            _resident(wuq.shape), _resident(wuk.shape), _resident(wuv.shape),
        ] + tab_specs,
        out_specs=[pl.BlockSpec((bm, n), row_map) for n in outs],
        out_shape=[jax.ShapeDtypeStruct((rows, n), BF16) for n in outs],
        scratch_shapes=[pltpu.VMEM((bm, d), BF16), pltpu.VMEM((bm, B_Q_RANK), BF16)],
        compiler_params=_cparams(("parallel",)),
        name="inproj_ab",
    )(h, g_pre, mod, w, aqn, akn, bqn, bkvn, wuq, wuk, wuv, *tabs)


def _inproj_cd_kernel(use_rope, h_ref, g_ref, mod_ref, w_ref, cqn_ref, ckn_ref, dqn_ref, dkn_ref,
                      cosh_ref, sinh_ref, q_ref, kv_ref, u_ref):
    bm = h_ref.shape[0]
    rb = min(PROJ_ROW_BLOCK, bm)
    scale = HEAD_DIM ** -0.5 * LOG2E
    kv_c = C_KV_HEADS * HEAD_DIM
    kv_d = D_HEADS * HEAD_DIM
    for r0 in range(0, bm, rb):
        rows = slice(r0, r0 + rb)
        _norm_mod_to(h_ref, g_ref, mod_ref, 0, 1, u_ref, r0, rb, inline=True)
        rope_h = (cosh_ref[rows, :], sinh_ref[rows, :]) if use_rope else None
        _emit_heads(u_ref, rows, w_ref, 0, C_HEADS, cqn_ref[...], scale, rope_h, q_ref, 0)
        _emit_heads(u_ref, rows, w_ref, C_HEADS * HEAD_DIM, C_KV_HEADS, ckn_ref[...], 1.0, rope_h, kv_ref, 2 * kv_d)
        _emit_heads(u_ref, rows, w_ref, C_HEADS * HEAD_DIM + kv_c, C_KV_HEADS, None, 1.0, None, kv_ref, 2 * kv_d + kv_c)
        _emit_heads(u_ref, rows, w_ref, C_IN, D_HEADS, dqn_ref[...], scale, None, q_ref, C_HEADS * HEAD_DIM)
        _emit_heads(u_ref, rows, w_ref, C_IN + kv_d, D_HEADS, dkn_ref[...], 1.0, None, kv_ref, 0)
        _emit_heads(u_ref, rows, w_ref, C_IN + 2 * kv_d, D_HEADS, None, 1.0, None, kv_ref, kv_d)


def _inproj_cd(h, g_pre, mod, w, cqn, ckn, dqn, dkn, rope_tabs, rows_per_batch, bm):
    rows, d = h.shape
    tiles_per_batch = rows_per_batch // bm
    nb = mod.shape[0]

    def row_map(i):
        return (i, 0)

    def mod_map(i):
        return (i // tiles_per_batch if nb > 1 else 0, 0, 0)

    tabs, tab_specs = _rope_specs(rope_tabs, 2, bm, tiles_per_batch)
    outs = [(C_HEADS + D_HEADS) * HEAD_DIM, 2 * (C_KV_HEADS + D_HEADS) * HEAD_DIM]
    return pl.pallas_call(
        functools.partial(_inproj_cd_kernel, rope_tabs is not None),
        grid=(rows // bm,),
        in_specs=[
            pl.BlockSpec((bm, d), row_map),
            _resident((1, d)),
            pl.BlockSpec((None, 8, d), mod_map),
            _resident(w.shape),
            _resident((1, HEAD_DIM)), _resident((1, HEAD_DIM)), _resident((1, HEAD_DIM)), _resident((1, HEAD_DIM)),
        ] + tab_specs,
        out_specs=[pl.BlockSpec((bm, n), row_map) for n in outs],
        out_shape=[jax.ShapeDtypeStruct((rows, n), BF16) for n in outs],
        scratch_shapes=[pltpu.VMEM((bm, d), BF16)],
        compiler_params=_cparams(("parallel",)),
        name="inproj_cd",
    )(h, g_pre, mod, w, cqn, ckn, dqn, dkn, *tabs)


def _flash_step(q_ref, kt, vt, m_ref, acc_ref, rep, dq, sub):
    bq = q_ref.shape[0]
    n_chunks = kt.shape[0] // LANES
    for r in range(rep):
        for sb in range(bq // sub):
            rows = slice(sb * sub, (sb + 1) * sub)
            mcol = slice(r * LANES, (r + 1) * LANES)
            acol = slice(r * 2 * LANES, (r + 1) * 2 * LANES)
            s = _dot_nt(q_ref[rows, r * dq:(r + 1) * dq], kt)
            chunks = [s[:, c * LANES:(c + 1) * LANES] for c in range(n_chunks)]
            lane_max = functools.reduce(jnp.maximum, chunks)
            m_prev = m_ref[rows, mcol]
            m_new = jnp.maximum(m_prev, jnp.max(lane_max, axis=-1, keepdims=True))
            alpha = jnp.exp2(m_prev - m_new)
            p = jnp.concatenate([jnp.exp2(ch - m_new) for ch in chunks], axis=1).astype(BF16)
            acc_ref[rows, acol] = jnp.concatenate([alpha, alpha], axis=1) * acc_ref[rows, acol] + _dot(p, vt)
            m_ref[rows, mcol] = m_new


def _flash_kernel(rep, dq, has_ctx, sub, *refs):
    if has_ctx:
        q_ref, k_ref, v_ref, kc_ref, vc_ref, o_ref, m_ref, acc_ref = refs
    else:
        q_ref, k_ref, v_ref, o_ref, m_ref, acc_ref = refs
    j = pl.program_id(3)

    def with_ones(v):
        return jnp.concatenate([v, jnp.ones(v.shape, BF16)], axis=1)

    @pl.when(j == 0)
    def _init():
        m_ref[...] = jnp.full(m_ref.shape, NEG, F32)
        acc_ref[...] = jnp.zeros(acc_ref.shape, F32)
        if has_ctx:
            _flash_step(q_ref, kc_ref[...], with_ones(vc_ref[...]), m_ref, acc_ref, rep, dq, sub)

    _flash_step(q_ref, k_ref[...], with_ones(v_ref[...]), m_ref, acc_ref, rep, dq, sub)

    @pl.when(j == pl.num_programs(3) - 1)
    def _fin():
        for r in range(rep):
            c0 = r * 2 * LANES
            o_ref[:, r * HEAD_DIM:(r + 1) * HEAD_DIM] = (
                acc_ref[:, c0:c0 + LANES] / acc_ref[:, c0 + LANES:c0 + 2 * LANES]).astype(BF16)


def _flash(q, k, v, kc, vc, batch, n_groups, rep, dq, bq, bk, sub):
    lq = q.shape[0] // batch
    lk = k.shape[0] // batch
    nq, nk = lq // bq, lk // bk
    has_ctx = kc is not None
    dv = HEAD_DIM
    assert dv == LANES
    in_specs = [
        pl.BlockSpec((bq, rep * dq), lambda b, g, i, j: (b * nq + i, g)),
        pl.BlockSpec((bk, dq), lambda b, g, i, j: (b * nk + j, g)),
        pl.BlockSpec((bk, dv), lambda b, g, i, j: (b * nk + j, g)),
    ]
    args = [q, k, v]
    if has_ctx:
        lc = kc.shape[0] // batch
        in_specs += [
            pl.BlockSpec((lc, dq), lambda b, g, i, j: (b, g)),
            pl.BlockSpec((lc, dv), lambda b, g, i, j: (b, g)),
        ]
        args += [kc, vc]
    return pl.pallas_call(
        functools.partial(_flash_kernel, rep, dq, has_ctx, min(sub, bq)),
        grid=(batch, n_groups, nq, nk),
        in_specs=in_specs,
        out_specs=pl.BlockSpec((bq, rep * dv), lambda b, g, i, j: (b * nq + i, g)),
        out_shape=jax.ShapeDtypeStruct((batch * lq, n_groups * rep * dv), BF16),
        scratch_shapes=[
            pltpu.VMEM((bq, rep * LANES), F32),
            pltpu.VMEM((bq, rep * 2 * LANES), F32),
        ],
        compiler_params=_cparams(("parallel", "parallel", "parallel", "arbitrary")),
        name="flash_attention",
    )(*args)


def _band_kernel(n_q, n_kv, per_head_bias, has_sink, *refs):
    if has_sink:
        q_ref, kvp_ref, kvx_ref, kvn_ref, kvc_ref, bias_ref, sink_ref, o_ref = refs
    else:
        q_ref, kvp_ref, kvx_ref, kvn_ref, kvc_ref, bias_ref, o_ref = refs
    t = BAND_TILE
    kv_refs = (kvc_ref, kvp_ref, kvx_ref, kvn_ref)
    lc = kvc_ref.shape[0]
    for r in range(n_q):
        hcol = slice(r * HEAD_DIM, (r + 1) * HEAD_DIM)
        kv_head = r * n_kv // n_q
        kcol = slice(kv_head * HEAD_DIM, (kv_head + 1) * HEAD_DIM)
        vcol = slice((n_kv + kv_head) * HEAD_DIM, (n_kv + kv_head + 1) * HEAD_DIM)
        bias = bias_ref.at[r] if per_head_bias else bias_ref
        keys = jnp.concatenate([ref[:, kcol] for ref in kv_refs], axis=0)
        vals = jnp.concatenate([ref[:, vcol] for ref in kv_refs], axis=0)
        vals = jnp.concatenate([vals, jnp.ones(vals.shape, BF16)], axis=1)
        s = _dot_nt(q_ref[:, hcol], keys)
        chunks = [s[:, c * LANES:(c + 1) * LANES] for c in range(lc // LANES)]
        chunks += [s[:, lc + c * LANES:lc + (c + 1) * LANES] + bias[:, c * LANES:(c + 1) * LANES]
                   for c in range(3 * t // LANES)]
        m = jnp.max(functools.reduce(jnp.maximum, chunks), axis=-1, keepdims=True)
        if has_sink:
            sink = sink_ref[r, 0:1, 0:1]
            m = jnp.maximum(m, sink)
        p = jnp.concatenate([jnp.exp2(ch - m) for ch in chunks], axis=1).astype(BF16)
        acc = _dot(p, vals)
        l = acc[:, HEAD_DIM:]
        if has_sink:
            l = l + jnp.exp2(sink - m)
        o_ref[:, hcol] = (acc[:, :HEAD_DIM] / l).astype(BF16)


def _band_attention(q, kv, kvc, bias, sink, batch, q_col0, kv_col0, n_kv):
    t = BAND_TILE
    n_q = 8
    seq = q.shape[0] // batch
    nq = seq // t
    lc = kvc.shape[0] // batch
    qw = n_q * HEAD_DIM
    kw = 2 * n_kv * HEAD_DIM
    has_sink = sink is not None
    per_head_bias = bias.ndim == 4

    def tile_map(off):
        def f(b, i):
            return (b * nq + jnp.clip(i + off, 0, nq - 1), kv_col0 // kw)
        return f

    def cls(i):
        return jnp.where(i == 0, 0, jnp.where(i == nq - 1, 2, 1))

    if per_head_bias:
        bias_spec = pl.BlockSpec((n_q, None, t, 3 * t), lambda b, i: (0, cls(i), 0, 0))
    else:
        bias_spec = pl.BlockSpec((None, t, 3 * t), lambda b, i: (cls(i), 0, 0))
    in_specs = [pl.BlockSpec((t, qw), lambda b, i: (b * nq + i, q_col0 // qw))]
    in_specs += [pl.BlockSpec((t, kw), tile_map(off)) for off in (-1, 0, 1)]
    in_specs += [pl.BlockSpec((lc, kw), lambda b, i: (b, kv_col0 // kw)), bias_spec]
    args = [q, kv, kv, kv, kvc, bias]
    if has_sink:
        in_specs.append(pl.BlockSpec((n_q, 8, LANES), lambda b, i: (0, 0, 0)))
        args.append(sink)
    return pl.pallas_call(
        functools.partial(_band_kernel, n_q, n_kv, per_head_bias, has_sink),
        grid=(batch, nq),
        in_specs=in_specs,
        out_specs=pl.BlockSpec((t, qw), lambda b, i: (b * nq + i, 0)),
        out_shape=jax.ShapeDtypeStruct((batch * seq, qw), BF16),
        compiler_params=_cparams(("parallel", "parallel")),
        name="band_attention",
    )(*args)


def _outproj_kernel(n_in, *refs):
    a_refs = refs[:n_in]
    w_ref, g_ref, mod_ref, h_ref, o_ref, acc_ref = refs[n_in:]
    bm = h_ref.shape[0]
    rb = min(PROJ_ROW_BLOCK, bm)
    for r0 in range(0, bm, rb):
        k0 = 0
        acc = None
        for a_ref in a_refs:
            kk = a_ref.shape[1]
            part = _dot(a_ref[r0:r0 + rb, :], w_ref[k0:k0 + kk, :])
            acc = part if acc is None else acc + part
            k0 += kk
        acc_ref[r0:r0 + rb, :] = acc
        _post_residual_to(acc_ref, h_ref, g_ref, mod_ref, 2, o_ref, r0, rb, inline=True)


def _outproj(a_list, w, g_post, mod, h, rows_per_batch, bm):
    rows, d = h.shape
    tiles_per_batch = rows_per_batch // bm
    nb = mod.shape[0]

    def row_map(i):
        return (i, 0)

    def mod_map(i):
        return (i // tiles_per_batch if nb > 1 else 0, 0, 0)

    return pl.pallas_call(
        functools.partial(_outproj_kernel, len(a_list)),
        grid=(rows // bm,),
        in_specs=[pl.BlockSpec((bm, a.shape[1]), row_map) for a in a_list] + [
            _resident(w.shape),
            _resident((1, d)),
            pl.BlockSpec((None, 8, d), mod_map),
            pl.BlockSpec((bm, d), row_map),
        ],
        out_specs=pl.BlockSpec((bm, d), row_map),
        out_shape=jax.ShapeDtypeStruct((rows, d), F32),
        scratch_shapes=[pltpu.VMEM((bm, d), F32)],
        compiler_params=_cparams(("parallel",)),
        name="outproj",
    )(*a_list, w, g_post, mod, h)


def _ffn_kernel(h_ref, gpre_ref, gpost_ref, mod_ref, wg_ref, wu_ref, wd_ref, o_ref, u_ref):
    f = pl.program_id(1)
    last = pl.num_programs(1) - 1
    bm = h_ref.shape[0]
    rb = min(FFN_ROW_BLOCK, bm)

    def partial_out(rows):
        u = u_ref[rows, :]
        act = (_silu(_dot(u, wg_ref[...])) * _dot(u, wu_ref[...])).astype(BF16)
        return _dot(act, wd_ref[...])

    @pl.when(f == 0)
    def _first():
        for r0 in range(0, bm, rb):
            _norm_mod_to(h_ref, gpre_ref, mod_ref, 3, 4, u_ref, r0, rb, inline=True)
            o_ref[r0:r0 + rb, :] = partial_out(slice(r0, r0 + rb))

    @pl.when(jnp.logical_and(f > 0, f < last))
    def _mid():
        o_ref[...] += partial_out(slice(None))

    @pl.when(f == last)
    def _last():
        for r0 in range(0, bm, rb):
            o_ref[r0:r0 + rb, :] += partial_out(slice(r0, r0 + rb))
            _post_residual_to(o_ref, h_ref, gpost_ref, mod_ref, 5, o_ref, r0, rb, inline=True)


def _cast_kernel(x_ref, o_ref):
    o_ref[...] = x_ref[...].astype(BF16)


def _to_bf16(w, br):
    depth, r, c = w.shape
    assert r % br == 0
    spec = pl.BlockSpec((None, br, c), lambda l, i: (l, i, 0))
    return pl.pallas_call(
        _cast_kernel,
        grid=(depth, r // br),
        in_specs=[spec],
        out_specs=spec,
        out_shape=jax.ShapeDtypeStruct(w.shape, BF16),
        compiler_params=_cparams(("parallel", "parallel")),
        name="cast_bf16",
    )(w)


def _ffn(h, g_pre, g_post, mod, wg, wu, wd, layer, rows_per_batch, bm, bf):
    rows, d = h.shape
    dff = wg.shape[2]
    tiles_per_batch = rows_per_batch // bm
    nb = mod.shape[0]

    def row_map(i, f):
        return (i, 0)

    def mod_map(i, f):
        return (i // tiles_per_batch if nb > 1 else 0, 0, 0)

    assert rows % bm == 0 and dff % bf == 0 and dff // bf >= 2
    return pl.pallas_call(
        _ffn_kernel,
        grid=(rows // bm, dff // bf),
        in_specs=[
            pl.BlockSpec((bm, d), row_map),
            _resident((1, d)), _resident((1, d)),
            pl.BlockSpec((None, 8, d), mod_map),
            pl.BlockSpec((None, d, bf), lambda i, f: (layer, 0, f)),
            pl.BlockSpec((None, d, bf), lambda i, f: (layer, 0, f)),
            pl.BlockSpec((None, bf, d), lambda i, f: (layer, f, 0)),
        ],
        out_specs=pl.BlockSpec((bm, d), row_map),
        out_shape=jax.ShapeDtypeStruct((rows, d), F32),
        scratch_shapes=[pltpu.VMEM((bm, d), BF16)],
        compiler_params=_cparams(("parallel", "arbitrary")),
        name="ffn",
    )(h, g_pre, g_post, mod, wg, wu, wd)


def _rope_table(n_tok, dim):
    t = jnp.arange(n_tok, dtype=jnp.int32)
    row = (t // GRID_W).astype(F32)
    col = (t % GRID_W).astype(F32)
    half = dim // 2
    inv_freq = ROPE_BASE ** (-jnp.arange(0, half, 2, dtype=F32) / half)
    ar = row[:, None] * inv_freq[None, :]
    ac = col[:, None] * inv_freq[None, :]
    ang = jnp.concatenate([ar, ar, ac, ac], axis=-1)
    return jnp.cos(ang), jnp.sin(ang)


def _rpb_cols_kernel(r_ref, e_ref, o_ref):
    x = r_ref[...] * LOG2E
    hi = x.astype(BF16)
    r1 = x - hi.astype(F32)
    mid = r1.astype(BF16)
    lo = (r1 - mid.astype(F32)).astype(BF16)
    e = e_ref[...]
    o_ref[...] = (_dot(hi, e) + _dot(mid, e)) + _dot(lo, e)


def _band_bias(rpb, seq):
    t, g = BAND_TILE, GRID_W
    rpt = t // g
    nq = seq // t
    rows = seq // g
    kh = min(NA_KH, rows)
    n_h, n_di, n_dj = rpb.shape
    tiles = (0, 1, nq - 1)
    neg = NEG * LOG2E

    qpos = np.array(tiles)[:, None, None] * t + np.arange(t)[None, :, None]
    kpos = (np.array(tiles)[:, None, None] - 1) * t + np.arange(3 * t)[None, None, :]
    win = (np.abs(kpos - qpos) <= C_WINDOW) & (kpos >= 0) & (kpos < seq)
    bias_c = jnp.asarray(np.where(win, 0.0, neg).astype(np.float32))

    qc = np.arange(g)[:, None]
    kc = np.arange(g)[None, :]
    cs = np.clip(qc - NA_KW // 2, 0, g - NA_KW)
    col_ok = (kc >= cs) & (kc < cs + NA_KW)
    onehot = np.zeros((LANES, g, g), np.float32)
    qq, kk = np.nonzero(col_ok)
    onehot[(kk - qq + NA_KW - 1), qq, kk] = 1.0
    assert n_h * n_di <= LANES and n_dj <= LANES
    rpb2d = jnp.pad(rpb.reshape(n_h * n_di, n_dj).astype(F32), ((0, LANES - n_h * n_di), (0, LANES - n_dj)))
    cols = pl.pallas_call(
        _rpb_cols_kernel,
        out_shape=jax.ShapeDtypeStruct((LANES, g * g), F32),
        name="rpb_columns",
    )(rpb2d, jnp.asarray(onehot.reshape(LANES, g * g), BF16))
    cols = cols[:n_h * n_di].reshape(n_h, n_di, g, g) + jnp.asarray(np.where(col_ok, 0.0, neg).astype(np.float32))
    masked = jnp.full((n_h, g, g), neg, F32)

    tile_rows = []
    for ti in tiles:
        for a in range(rpt):
            qr = ti * rpt + a
            rs = min(max(qr - kh // 2, 0), rows - kh)
            krs = [(ti - 1) * rpt + b for b in range(3 * rpt)]
            tile_rows.append(jnp.concatenate(
                [cols[:, kr - qr + NA_KH - 1] if rs <= kr < rs + kh else masked for kr in krs], axis=-1))
    bias_d = jnp.stack(tile_rows, axis=1).reshape(n_h, len(tiles), t, 3 * t)
    return bias_c, bias_d


def _pad_mod(m):
    nb = m.shape[0]
    return jnp.pad(m.reshape(nb, 6, D_MODEL), ((0, 0), (0, 2), (0, 0)))


def _row(v):
    return v.reshape(1, -1).astype(F32)


def _tiles(batch, seq, n_ctx):
    return dict(
        bm_lat=512,
        bm_ctx=min(512, batch * n_ctx),
        bm_ffn=1024 if seq % 1024 == 0 else 512,
        bf_ffn=512,
        bq_a=min(1024, seq),
        bq_b=min(4096, seq),
        bk=min(2048, seq),
        sub=256,
    )


def kernel(x, c, ctx, c_ctx, w_mod, b_mod, g_mix_pre, g_mix_post, g_ffn_pre, g_ffn_post, w_gate, w_up, w_down,
           ab_w_in, ab_w_out, a_q_norm, a_k_norm, b_q_norm, b_kv_norm, b_w_uq, b_w_ukv, cd_w_in, cd_w_out,
           c_q_norm, c_k_norm, c_sink, d_q_norm, d_k_norm, d_rpb):
    batch, seq, d = x.shape
    n_ctx = ctx.shape[1]
    depth = w_mod.shape[0]
    assert d == D_MODEL and seq % 512 == 0 and n_ctx % 256 == 0 and batch + 1 <= 8
    assert depth == 2

    cvec = jnp.zeros((8, d), F32).at[:batch].set(c).at[batch].set(c_ctx)
    mod_all = _modulation(cvec, w_mod, b_mod)

    cos_h, sin_h = _rope_table(seq, HEAD_DIM)
    cos_r, sin_r = _rope_table(seq, B_ROPE)
    pad_r = ((0, 0), (0, LANES - B_ROPE))
    tabs_ab = (cos_h, sin_h, jnp.pad(cos_r, pad_r), jnp.pad(sin_r, pad_r))
    tabs_cd = (cos_h, sin_h)

    tl = _tiles(batch, seq, n_ctx)
    bm_lat, bm_ctx, bm_ffn, bf_ffn = tl["bm_lat"], tl["bm_ctx"], tl["bm_ffn"], tl["bf_ffn"]
    bq_a, bq_b, bk, sub = tl["bq_a"], tl["bq_b"], tl["bk"], tl["sub"]

    h_lat = x.reshape(batch * seq, d)
    h_ctx = ctx.reshape(batch * n_ctx, d)
    wg, wu, wd = _to_bf16(w_gate, 256), _to_bf16(w_up, 256), _to_bf16(w_down, 512)

    for i in range(depth):
        last = i == depth - 1
        need_ctx = not last
        j = i // 2
        mod_lat = _pad_mod(mod_all[i, :batch])
        mod_ctx = _pad_mod(mod_all[i, batch:batch + 1])
        g1, g2, g3, g4 = _row(g_mix_pre[i]), _row(g_mix_post[i]), _row(g_ffn_pre[i]), _row(g_ffn_post[i])

        if i % 2 == 0:
            w_in = ab_w_in[j]
            kr_cols = jnp.pad(w_in[:, A_IN + B_Q_RANK + B_KV_RANK:], ((0, 0), (0, LANES - B_ROPE)))
            w_ab = jnp.concatenate([w_in[:, :A_IN + B_Q_RANK + B_KV_RANK], kr_cols], axis=1).astype(BF16)
            wuq = jnp.pad(b_w_uq[j].reshape(B_Q_RANK, B_HEADS, B_NOPE + B_ROPE),
                          ((0, 0), (0, 0), (0, B_QK_PAD - B_NOPE - B_ROPE))).reshape(B_Q_RANK, B_HEADS * B_QK_PAD)
            wukv = b_w_ukv[j].reshape(B_KV_RANK, B_HEADS, B_NOPE + B_V)
            wuk = wukv[:, :, :B_NOPE].reshape(B_KV_RANK, B_HEADS * B_NOPE)
            wuv = wukv[:, :, B_NOPE:].reshape(B_KV_RANK, B_HEADS * B_V)
            norms = (_row(a_q_norm[j]), _row(a_k_norm[j]), _row(b_q_norm[j]), _row(b_kv_norm[j]),
                     wuq.astype(BF16), wuk.astype(BF16), wuv.astype(BF16))
            qa, ka, va, qb, kb, vb = _inproj_ab(h_lat, g1, mod_lat, w_ab, *norms, tabs_ab, seq, bm_lat)
            qac, kac, vac, qbc, kbc, vbc = _inproj_ab(h_ctx, g1, mod_ctx, w_ab, *norms, None, batch * n_ctx, bm_ctx)
            rep_a = A_HEADS // A_KV_HEADS
            oa = _flash(qa, ka, va, kac, vac, batch, A_KV_HEADS, rep_a, HEAD_DIM, bq_a, bk, sub)
            ob = _flash(qb, kb, vb, kbc, vbc, batch, B_HEADS, 1, B_QK_PAD, bq_b, bk, sub)
            w_out = ab_w_out[j].astype(BF16)
            attn_lat = [oa, ob]
            if need_ctx:
                oac = _flash(qac, kac, vac, None, None, batch, A_KV_HEADS, rep_a, HEAD_DIM, n_ctx, n_ctx, sub)
                obc = _flash(qbc, kbc, vbc, None, None, batch, B_HEADS, 1, B_QK_PAD, n_ctx, n_ctx, sub)
                attn_ctx = [oac, obc]
        else:
            w_cd = cd_w_in[j].astype(BF16)
            norms = (_row(c_q_norm[j]), _row(c_k_norm[j]), _row(d_q_norm[j]), _row(d_k_norm[j]))
            q_l, kv_l = _inproj_cd(h_lat, g1, mod_lat, w_cd, *norms, tabs_cd, seq, bm_lat)
            _, kv_c = _inproj_cd(h_ctx, g1, mod_ctx, w_cd, *norms, None, batch * n_ctx, bm_ctx)
            bias_c, bias_d = _band_bias(d_rpb[j], seq)
            sink = jnp.broadcast_to((c_sink[j].astype(F32) * LOG2E)[:, None, None], (C_HEADS, 8, LANES))
            kv_d = 2 * D_HEADS * HEAD_DIM
            attn_lat = [
                _band_attention(q_l, kv_l, kv_c, bias_c, sink, batch, 0, kv_d, C_KV_HEADS),
                _band_attention(q_l, kv_l, kv_c, bias_d, None, batch, C_HEADS * HEAD_DIM, 0, D_HEADS),
            ]
            w_out = cd_w_out[j].astype(BF16)

        h_lat = _outproj(attn_lat, w_out, g2, mod_lat, h_lat, seq, bm_lat)
        h_lat = _ffn(h_lat, g3, g4, mod_lat, wg, wu, wd, i, seq, bm_ffn, bf_ffn)
        if need_ctx:
            h_ctx = _outproj(attn_ctx, w_out, g2, mod_ctx, h_ctx, batch * n_ctx, bm_ctx)
            h_ctx = _ffn(h_ctx, g3, g4, mod_ctx, wg, wu, wd, i, batch * n_ctx, bm_ctx, bf_ffn)
    return h_lat.reshape(batch, seq, d)
```

```python
import functools

import jax
import jax.numpy as jnp
import numpy as np
from jax import lax
from jax.experimental import pallas as pl
from jax.experimental.pallas import tpu as pltpu

F32 = jnp.float32
BF16 = jnp.bfloat16

D_MODEL = 2048
GRID_W = 64
HEAD_DIM = 128
ROPE_BASE = 10000.0
EPS = 1e-6
NEG = -1e30
LOG2E = 1.4426950408889634

A_HEADS, A_KV_HEADS = 8, 2
B_HEADS, B_Q_RANK, B_KV_RANK, B_NOPE, B_ROPE, B_V = 8, 512, 512, 128, 64, 128
C_HEADS, C_KV_HEADS, C_WINDOW = 8, 2, 128
D_HEADS, NA_KH, NA_KW = 8, 8, 16
A_IN = (A_HEADS + 2 * A_KV_HEADS) * HEAD_DIM
C_IN = (C_HEADS + 2 * C_KV_HEADS) * HEAD_DIM
D_FF = 5632

LANES = 128
MXU_COLS = 256
B_QK_PAD = 256
BAND_TILE = 256
VMEM_LIMIT = 56 * 1024 * 1024
ROW_CHUNK = 32
PROJ_ROW_BLOCK = 256
FFN_ROW_BLOCK = 256
CHUNK_UNROLL = 4


def _cparams(sem):
    return pltpu.CompilerParams(dimension_semantics=sem, vmem_limit_bytes=VMEM_LIMIT)


def _resident(shape):
    nd = len(shape)
    return pl.BlockSpec(shape, lambda *_: (0,) * nd, pipeline_mode=pl.Buffered(1))


def _silu(x):
    return x * (1.0 / (1.0 + jnp.exp(-x)))


def _rms(x, g):
    ms = jnp.mean(x * x, axis=-1, keepdims=True)
    return x * lax.rsqrt(ms + EPS) * g


def _dot(a, b):
    return jnp.dot(a, b, preferred_element_type=F32)


def _dot_nt(a, b):
    return lax.dot_general(a, b, (((1,), (1,)), ((), ())), preferred_element_type=F32)


def _mod_kernel(c_ref, w_ref, b_ref, o_ref):
    s = _silu(c_ref[...])
    s_hi = s.astype(BF16)
    s_lo = (s - s_hi.astype(F32)).astype(BF16)
    w = w_ref[...]
    w_hi = w.astype(BF16)
    w_lo = (w - w_hi.astype(F32)).astype(BF16)
    o_ref[...] = _dot(s_hi, w_hi) + (_dot(s_lo, w_hi) + _dot(s_hi, w_lo)) + b_ref[...]


def _modulation(cvec, w_mod, b_mod):
    depth, d, n = w_mod.shape
    bn = 1024
    return pl.pallas_call(
        _mod_kernel,
        grid=(depth, n // bn),
        in_specs=[
            pl.BlockSpec((8, d), lambda i, j: (0, 0)),
            pl.BlockSpec((None, d, bn), lambda i, j: (i, 0, j)),
            pl.BlockSpec((None, 1, bn), lambda i, j: (i, 0, j)),
        ],
        out_specs=pl.BlockSpec((None, 8, bn), lambda i, j: (i, 0, j)),
        out_shape=jax.ShapeDtypeStruct((depth, 8, n), F32),
        compiler_params=_cparams(("parallel", "parallel")),
        name="modulation",
    )(cvec, w_mod, b_mod.reshape(depth, 1, n))


def _chunked(fn, n_rows, row0, inline):
    chunk = min(ROW_CHUNK, n_rows)
    if inline:
        for c in range(n_rows // chunk):
            fn(row0 + c * chunk)
        return

    def body(c, carry):
        fn(pl.multiple_of(row0 + c * chunk, chunk))
        return carry

    lax.fori_loop(0, n_rows // chunk, body, 0, unroll=CHUNK_UNROLL)


def _norm_mod_to(h_ref, g_ref, mod_ref, shift_row, scale_row, u_ref, row0=0, n_rows=None, inline=False):
    n_rows = h_ref.shape[0] if n_rows is None else n_rows
    chunk = min(ROW_CHUNK, n_rows)
    gs = g_ref[...] * (1.0 + mod_ref[scale_row:scale_row + 1, :])
    sh = mod_ref[shift_row:shift_row + 1, :]

    def one(r):
        h = h_ref[pl.ds(r, chunk), :]
        inv = lax.rsqrt(jnp.mean(h * h, axis=-1, keepdims=True) + EPS)
        u_ref[pl.ds(r, chunk), :] = ((h * inv) * gs + sh).astype(BF16)

    _chunked(one, n_rows, row0, inline)


def _post_residual_to(acc_ref, h_ref, g_ref, mod_ref, gate_row, o_ref, row0=0, n_rows=None, inline=False):
    n_rows = h_ref.shape[0] if n_rows is None else n_rows
    chunk = min(ROW_CHUNK, n_rows)
    gg = g_ref[...] * mod_ref[gate_row:gate_row + 1, :]

    def one(r):
        a = acc_ref[pl.ds(r, chunk), :]
        inv = lax.rsqrt(jnp.mean(a * a, axis=-1, keepdims=True) + EPS)
        o_ref[pl.ds(r, chunk), :] = h_ref[pl.ds(r, chunk), :] + (a * inv) * gg

    _chunked(one, n_rows, row0, inline)


def _rope(x, cos, sin, quarter):
    lane = lax.broadcasted_iota(jnp.int32, x.shape, 1)
    first = (lane % (2 * quarter)) < quarter
    rot = jnp.where(first, -pltpu.roll(x, LANES - quarter, 1), pltpu.roll(x, quarter, 1))
    return x * cos + rot * sin


def _emit_heads(u_ref, rows, w_ref, col0, n_heads, gain, scale, rope, o_ref, ocol0):
    h = 0
    while h < n_heads:
        nh = min(MXU_COLS // HEAD_DIM, n_heads - h)
        c = col0 + h * HEAD_DIM
        p = _dot(u_ref[rows, :], w_ref[:, c:c + nh * HEAD_DIM])
        for t in range(nh):
            x = p[:, t * HEAD_DIM:(t + 1) * HEAD_DIM]
            if gain is not None:
                x = _rms(x, gain)
                if rope is not None:
                    x = _rope(x, rope[0], rope[1], HEAD_DIM // 4)
                if scale != 1.0:
                    x = x * scale
            oc = ocol0 + (h + t) * HEAD_DIM
            o_ref[rows, oc:oc + HEAD_DIM] = x.astype(BF16)
        h += nh


def _inproj_ab_kernel(use_rope, h_ref, g_ref, mod_ref, w_ref, aqn_ref, akn_ref, bqn_ref, bkvn_ref,
                      wuq_ref, wuk_ref, wuv_ref, cosh_ref, sinh_ref, cosr_ref, sinr_ref,
                      qa_ref, ka_ref, va_ref, qb_ref, kb_ref, vb_ref, u_ref, c_ref):
    bm = h_ref.shape[0]
    rb = min(PROJ_ROW_BLOCK, bm)
    a_scale = HEAD_DIM ** -0.5 * LOG2E
    b_scale = (B_NOPE + B_ROPE) ** -0.5 * LOG2E
    for r0 in range(0, bm, rb):
        rows = slice(r0, r0 + rb)
        _norm_mod_to(h_ref, g_ref, mod_ref, 0, 1, u_ref, r0, rb, inline=True)
        rope_h = (cosh_ref[rows, :], sinh_ref[rows, :]) if use_rope else None
        _emit_heads(u_ref, rows, w_ref, 0, A_HEADS, aqn_ref[...], a_scale, rope_h, qa_ref, 0)
        _emit_heads(u_ref, rows, w_ref, A_HEADS * HEAD_DIM, A_KV_HEADS, akn_ref[...], 1.0, rope_h, ka_ref, 0)
        _emit_heads(u_ref, rows, w_ref, (A_HEADS + A_KV_HEADS) * HEAD_DIM, A_KV_HEADS, None, 1.0, None, va_ref, 0)

        cq = _dot(u_ref[rows, :], w_ref[:, A_IN:A_IN + B_Q_RANK])
        c_ref[rows, :] = _rms(cq, bqn_ref[...]).astype(BF16)
        for hd in range(B_HEADS):
            q = _dot(c_ref[rows, :], wuq_ref[:, hd * B_QK_PAD:(hd + 1) * B_QK_PAD])
            q_nope = q[:, :B_NOPE]
            q_rope = q[:, B_NOPE:]
            if use_rope:
                q_rope = _rope(q_rope, cosr_ref[rows, :], sinr_ref[rows, :], B_ROPE // 4)
            qb_ref[rows, hd * B_QK_PAD:hd * B_QK_PAD + B_NOPE] = (q_nope * b_scale).astype(BF16)
            qb_ref[rows, hd * B_QK_PAD + B_NOPE:(hd + 1) * B_QK_PAD] = (q_rope * b_scale).astype(BF16)
        kr0 = A_IN + B_Q_RANK + B_KV_RANK
        kr = _dot(u_ref[rows, :], w_ref[:, kr0:kr0 + LANES])
        if use_rope:
            kr = _rope(kr, cosr_ref[rows, :], sinr_ref[rows, :], B_ROPE // 4)
        kr = kr.astype(BF16)
        ckv = _dot(u_ref[rows, :], w_ref[:, A_IN + B_Q_RANK:A_IN + B_Q_RANK + B_KV_RANK])
        c_ref[rows, :] = _rms(ckv, bkvn_ref[...]).astype(BF16)
        for pair in range(B_HEADS * B_NOPE // MXU_COLS):
            kn = _dot(c_ref[rows, :], wuk_ref[:, pair * MXU_COLS:(pair + 1) * MXU_COLS])
            for t in range(MXU_COLS // B_NOPE):
                hd = pair * (MXU_COLS // B_NOPE) + t
                kb_ref[rows, hd * B_QK_PAD:hd * B_QK_PAD + B_NOPE] = kn[:, t * B_NOPE:(t + 1) * B_NOPE].astype(BF16)
                kb_ref[rows, hd * B_QK_PAD + B_NOPE:(hd + 1) * B_QK_PAD] = kr
            vb_ref[rows, pair * MXU_COLS:(pair + 1) * MXU_COLS] = _dot(
                c_ref[rows, :], wuv_ref[:, pair * MXU_COLS:(pair + 1) * MXU_COLS]).astype(BF16)


def _rope_specs(rope_tabs, n_tabs, bm, tiles_per_batch):
    if rope_tabs is None:
        return [jnp.zeros((bm, LANES), F32)] * n_tabs, [pl.BlockSpec((bm, LANES), lambda i: (0, 0))] * n_tabs
    return list(rope_tabs), [pl.BlockSpec((bm, LANES), lambda i: (i % tiles_per_batch, 0))] * n_tabs


def _inproj_ab(h, g_pre, mod, w, aqn, akn, bqn, bkvn, wuq, wuk, wuv, rope_tabs, rows_per_batch, bm):
    rows, d = h.shape
    tiles_per_batch = rows_per_batch // bm
    nb = mod.shape[0]

    def row_map(i):
        return (i, 0)

    def mod_map(i):
        return (i // tiles_per_batch if nb > 1 else 0, 0, 0)

    tabs, tab_specs = _rope_specs(rope_tabs, 4, bm, tiles_per_batch)
    outs = [
        (A_HEADS * HEAD_DIM), (A_KV_HEADS * HEAD_DIM), (A_KV_HEADS * HEAD_DIM),
        (B_HEADS * B_QK_PAD), (B_HEADS * B_QK_PAD), (B_HEADS * B_V),
    ]
    return pl.pallas_call(
        functools.partial(_inproj_ab_kernel, rope_tabs is not None),
        grid=(rows // bm,),
        in_specs=[
            pl.BlockSpec((bm, d), row_map),
            _resident((1, d)),
            pl.BlockSpec((None, 8, d), mod_map),
            _resident(w.shape),
            _resident((1, HEAD_DIM)), _resident((1, HEAD_DIM)),
            _resident((1, B_Q_RANK)), _resident((1, B_KV_RANK)),
            _resident(wuq.shape), _resident(wuk.shape), _resident(wuv.shape),
        ] + tab_specs,
        out_specs=[pl.BlockSpec((bm, n), row_map) for n in outs],
        out_shape=[jax.ShapeDtypeStruct((rows, n), BF16) for n in outs],
        scratch_shapes=[pltpu.VMEM((bm, d), BF16), pltpu.VMEM((bm, B_Q_RANK), BF16)],
        compiler_params=_cparams(("parallel",)),
        name="inproj_ab",
    )(h, g_pre, mod, w, aqn, akn, bqn, bkvn, wuq, wuk, wuv, *tabs)


def _inproj_cd_kernel(use_rope, h_ref, g_ref, mod_ref, w_ref, cqn_ref, ckn_ref, dqn_ref, dkn_ref,
                      cosh_ref, sinh_ref, q_ref, kv_ref, u_ref):
    bm = h_ref.shape[0]
    rb = min(PROJ_ROW_BLOCK, bm)
    scale = HEAD_DIM ** -0.5 * LOG2E
    kv_c = C_KV_HEADS * HEAD_DIM
    kv_d = D_HEADS * HEAD_DIM
    for r0 in range(0, bm, rb):
        rows = slice(r0, r0 + rb)
        _norm_mod_to(h_ref, g_ref, mod_ref, 0, 1, u_ref, r0, rb, inline=True)
        rope_h = (cosh_ref[rows, :], sinh_ref[rows, :]) if use_rope else None
        _emit_heads(u_ref, rows, w_ref, 0, C_HEADS, cqn_ref[...], scale, rope_h, q_ref, 0)
        _emit_heads(u_ref, rows, w_ref, C_HEADS * HEAD_DIM, C_KV_HEADS, ckn_ref[...], 1.0, rope_h, kv_ref, 2 * kv_d)
        _emit_heads(u_ref, rows, w_ref, C_HEADS * HEAD_DIM + kv_c, C_KV_HEADS, None, 1.0, None, kv_ref, 2 * kv_d + kv_c)
        _emit_heads(u_ref, rows, w_ref, C_IN, D_HEADS, dqn_ref[...], scale, None, q_ref, C_HEADS * HEAD_DIM)
        _emit_heads(u_ref, rows, w_ref, C_IN + kv_d, D_HEADS, dkn_ref[...], 1.0, None, kv_ref, 0)
        _emit_heads(u_ref, rows, w_ref, C_IN + 2 * kv_d, D_HEADS, None, 1.0, None, kv_ref, kv_d)


def _inproj_cd(h, g_pre, mod, w, cqn, ckn, dqn, dkn, rope_tabs, rows_per_batch, bm):
    rows, d = h.shape
    tiles_per_batch = rows_per_batch // bm
    nb = mod.shape[0]

    def row_map(i):
        return (i, 0)

    def mod_map(i):
        return (i // tiles_per_batch if nb > 1 else 0, 0, 0)

    tabs, tab_specs = _rope_specs(rope_tabs, 2, bm, tiles_per_batch)
    outs = [(C_HEADS + D_HEADS) * HEAD_DIM, 2 * (C_KV_HEADS + D_HEADS) * HEAD_DIM]
    return pl.pallas_call(
        functools.partial(_inproj_cd_kernel, rope_tabs is not None),
        grid=(rows // bm,),
        in_specs=[
            pl.BlockSpec((bm, d), row_map),
            _resident((1, d)),
            pl.BlockSpec((None, 8, d), mod_map),
            _resident(w.shape),
            _resident((1, HEAD_DIM)), _resident((1, HEAD_DIM)), _resident((1, HEAD_DIM)), _resident((1, HEAD_DIM)),
        ] + tab_specs,
        out_specs=[pl.BlockSpec((bm, n), row_map) for n in outs],
        out_shape=[jax.ShapeDtypeStruct((rows, n), BF16) for n in outs],
        scratch_shapes=[pltpu.VMEM((bm, d), BF16)],
        compiler_params=_cparams(("parallel",)),
        name="inproj_cd",
    )(h, g_pre, mod, w, cqn, ckn, dqn, dkn, *tabs)


def _flash_step(q_ref, kt, vt, m_ref, acc_ref, rep, dq, sub):
    bq = q_ref.shape[0]
    n_chunks = kt.shape[0] // LANES
    for r in range(rep):
        for sb in range(bq // sub):
            rows = slice(sb * sub, (sb + 1) * sub)
            mcol = slice(r * LANES, (r + 1) * LANES)
            acol = slice(r * 2 * LANES, (r + 1) * 2 * LANES)
            s = _dot_nt(q_ref[rows, r * dq:(r + 1) * dq], kt)
            chunks = [s[:, c * LANES:(c + 1) * LANES] for c in range(n_chunks)]
            lane_max = functools.reduce(jnp.maximum, chunks)
            m_prev = m_ref[rows, mcol]
            m_new = jnp.maximum(m_prev, jnp.max(lane_max, axis=-1, keepdims=True))
            alpha = jnp.exp2(m_prev - m_new)
            p = jnp.concatenate([jnp.exp2(ch - m_new) for ch in chunks], axis=1).astype(BF16)
            acc_ref[rows, acol] = jnp.concatenate([alpha, alpha], axis=1) * acc_ref[rows, acol] + _dot(p, vt)
            m_ref[rows, mcol] = m_new


def _flash_kernel(rep, dq, has_ctx, sub, *refs):
    if has_ctx:
        q_ref, k_ref, v_ref, kc_ref, vc_ref, o_ref, m_ref, acc_ref = refs
    else:
        q_ref, k_ref, v_ref, o_ref, m_ref, acc_ref = refs
    j = pl.program_id(3)

    def with_ones(v):
        return jnp.concatenate([v, jnp.ones(v.shape, BF16)], axis=1)

    @pl.when(j == 0)
    def _init():
        m_ref[...] = jnp.full(m_ref.shape, NEG, F32)
        acc_ref[...] = jnp.zeros(acc_ref.shape, F32)
        if has_ctx:
            _flash_step(q_ref, kc_ref[...], with_ones(vc_ref[...]), m_ref, acc_ref, rep, dq, sub)

    _flash_step(q_ref, k_ref[...], with_ones(v_ref[...]), m_ref, acc_ref, rep, dq, sub)

    @pl.when(j == pl.num_programs(3) - 1)
    def _fin():
        for r in range(rep):
            c0 = r * 2 * LANES
            o_ref[:, r * HEAD_DIM:(r + 1) * HEAD_DIM] = (
                acc_ref[:, c0:c0 + LANES] / acc_ref[:, c0 + LANES:c0 + 2 * LANES]).astype(BF16)


def _flash(q, k, v, kc, vc, batch, n_groups, rep, dq, bq, bk, sub):
    lq = q.shape[0] // batch
    lk = k.shape[0] // batch
    nq, nk = lq // bq, lk // bk
    has_ctx = kc is not None
    dv = HEAD_DIM
    assert dv == LANES
    in_specs = [
        pl.BlockSpec((bq, rep * dq), lambda b, g, i, j: (b * nq + i, g)),
        pl.BlockSpec((bk, dq), lambda b, g, i, j: (b * nk + j, g)),
        pl.BlockSpec((bk, dv), lambda b, g, i, j: (b * nk + j, g)),
    ]
    args = [q, k, v]
    if has_ctx:
        lc = kc.shape[0] // batch
        in_specs += [
            pl.BlockSpec((lc, dq), lambda b, g, i, j: (b, g)),
            pl.BlockSpec((lc, dv), lambda b, g, i, j: (b, g)),
        ]
        args += [kc, vc]
    return pl.pallas_call(
        functools.partial(_flash_kernel, rep, dq, has_ctx, min(sub, bq)),
        grid=(batch, n_groups, nq, nk),
        in_specs=in_specs,
        out_specs=pl.BlockSpec((bq, rep * dv), lambda b, g, i, j: (b * nq + i, g)),
        out_shape=jax.ShapeDtypeStruct((batch * lq, n_groups * rep * dv), BF16),
        scratch_shapes=[
            pltpu.VMEM((bq, rep * LANES), F32),
            pltpu.VMEM((bq, rep * 2 * LANES), F32),
        ],
        compiler_params=_cparams(("parallel", "parallel", "parallel", "arbitrary")),
        name="flash_attention",
    )(*args)


def _band_kernel(n_q, n_kv, per_head_bias, has_sink, *refs):
    if has_sink:
        q_ref, kvp_ref, kvx_ref, kvn_ref, kvc_ref, bias_ref, sink_ref, o_ref = refs
    else:
        q_ref, kvp_ref, kvx_ref, kvn_ref, kvc_ref, bias_ref, o_ref = refs
    t = BAND_TILE
    kv_refs = (kvc_ref, kvp_ref, kvx_ref, kvn_ref)
    lc = kvc_ref.shape[0]
    for r in range(n_q):
        hcol = slice(r * HEAD_DIM, (r + 1) * HEAD_DIM)
        kv_head = r * n_kv // n_q
        kcol = slice(kv_head * HEAD_DIM, (kv_head + 1) * HEAD_DIM)
        vcol = slice((n_kv + kv_head) * HEAD_DIM, (n_kv + kv_head + 1) * HEAD_DIM)
        bias = bias_ref.at[r] if per_head_bias else bias_ref
        keys = jnp.concatenate([ref[:, kcol] for ref in kv_refs], axis=0)
        vals = jnp.concatenate([ref[:, vcol] for ref in kv_refs], axis=0)
        vals = jnp.concatenate([vals, jnp.ones(vals.shape, BF16)], axis=1)
        s = _dot_nt(q_ref[:, hcol], keys)
        chunks = [s[:, c * LANES:(c + 1) * LANES] for c in range(lc // LANES)]
        chunks += [s[:, lc + c * LANES:lc + (c + 1) * LANES] + bias[:, c * LANES:(c + 1) * LANES]
                   for c in range(3 * t // LANES)]
        m = jnp.max(functools.reduce(jnp.maximum, chunks), axis=-1, keepdims=True)
        if has_sink:
            sink = sink_ref[r, 0:1, 0:1]
            m = jnp.maximum(m, sink)
        p = jnp.concatenate([jnp.exp2(ch - m) for ch in chunks], axis=1).astype(BF16)
        acc = _dot(p, vals)
        l = acc[:, HEAD_DIM:]
        if has_sink:
            l = l + jnp.exp2(sink - m)
        o_ref[:, hcol] = (acc[:, :HEAD_DIM] / l).astype(BF16)


def _band_attention(q, kv, kvc, bias, sink, batch, q_col0, kv_col0, n_kv):
    t = BAND_TILE
    n_q = 8
    seq = q.shape[0] // batch
    nq = seq // t
    lc = kvc.shape[0] // batch
    qw = n_q * HEAD_DIM
    kw = 2 * n_kv * HEAD_DIM
    has_sink = sink is not None
    per_head_bias = bias.ndim == 4

    def tile_map(off):
        def f(b, i):
            return (b * nq + jnp.clip(i + off, 0, nq - 1), kv_col0 // kw)
        return f

    def cls(i):
        return jnp.where(i == 0, 0, jnp.where(i == nq - 1, 2, 1))

    if per_head_bias:
        bias_spec = pl.BlockSpec((n_q, None, t, 3 * t), lambda b, i: (0, cls(i), 0, 0))
    else:
        bias_spec = pl.BlockSpec((None, t, 3 * t), lambda b, i: (cls(i), 0, 0))
    in_specs = [pl.BlockSpec((t, qw), lambda b, i: (b * nq + i, q_col0 // qw))]
    in_specs += [pl.BlockSpec((t, kw), tile_map(off)) for off in (-1, 0, 1)]
    in_specs += [pl.BlockSpec((lc, kw), lambda b, i: (b, kv_col0 // kw)), bias_spec]
    args = [q, kv, kv, kv, kvc, bias]
    if has_sink:
        in_specs.append(pl.BlockSpec((n_q, 8, LANES), lambda b, i: (0, 0, 0)))
        args.append(sink)
    return pl.pallas_call(
        functools.partial(_band_kernel, n_q, n_kv, per_head_bias, has_sink),
        grid=(batch, nq),
        in_specs=in_specs,
        out_specs=pl.BlockSpec((t, qw), lambda b, i: (b * nq + i, 0)),
        out_shape=jax.ShapeDtypeStruct((batch * seq, qw), BF16),
        compiler_params=_cparams(("parallel", "parallel")),
        name="band_attention",
    )(*args)


def _outproj_kernel(n_in, *refs):
    a_refs = refs[:n_in]
    w_ref, g_ref, mod_ref, h_ref, o_ref, acc_ref = refs[n_in:]
    bm = h_ref.shape[0]
    rb = min(PROJ_ROW_BLOCK, bm)
    for r0 in range(0, bm, rb):
        k0 = 0
        acc = None
        for a_ref in a_refs:
            kk = a_ref.shape[1]
            part = _dot(a_ref[r0:r0 + rb, :], w_ref[k0:k0 + kk, :])
            acc = part if acc is None else acc + part
            k0 += kk
        acc_ref[r0:r0 + rb, :] = acc
        _post_residual_to(acc_ref, h_ref, g_ref, mod_ref, 2, o_ref, r0, rb, inline=True)


def _outproj(a_list, w, g_post, mod, h, rows_per_batch, bm):
    rows, d = h.shape
    tiles_per_batch = rows_per_batch // bm
    nb = mod.shape[0]

    def row_map(i):
        return (i, 0)

    def mod_map(i):
        return (i // tiles_per_batch if nb > 1 else 0, 0, 0)

    return pl.pallas_call(
        functools.partial(_outproj_kernel, len(a_list)),
        grid=(rows // bm,),
        in_specs=[pl.BlockSpec((bm, a.shape[1]), row_map) for a in a_list] + [
            _resident(w.shape),
            _resident((1, d)),
            pl.BlockSpec((None, 8, d), mod_map),
            pl.BlockSpec((bm, d), row_map),
        ],
        out_specs=pl.BlockSpec((bm, d), row_map),
        out_shape=jax.ShapeDtypeStruct((rows, d), F32),
        scratch_shapes=[pltpu.VMEM((bm, d), F32)],
        compiler_params=_cparams(("parallel",)),
        name="outproj",
    )(*a_list, w, g_post, mod, h)


def _ffn_kernel(h_ref, gpre_ref, gpost_ref, mod_ref, wg_ref, wu_ref, wd_ref, o_ref, u_ref):
    f = pl.program_id(1)
    last = pl.num_programs(1) - 1
    bm = h_ref.shape[0]
    rb = min(FFN_ROW_BLOCK, bm)

    def partial_out(rows):
        u = u_ref[rows, :]
        act = (_silu(_dot(u, wg_ref[...])) * _dot(u, wu_ref[...])).astype(BF16)
        return _dot(act, wd_ref[...])

    @pl.when(f == 0)
    def _first():
        for r0 in range(0, bm, rb):
            _norm_mod_to(h_ref, gpre_ref, mod_ref, 3, 4, u_ref, r0, rb, inline=True)
            o_ref[r0:r0 + rb, :] = partial_out(slice(r0, r0 + rb))

    @pl.when(jnp.logical_and(f > 0, f < last))
    def _mid():
        o_ref[...] += partial_out(slice(None))

    @pl.when(f == last)
    def _last():
        for r0 in range(0, bm, rb):
            o_ref[r0:r0 + rb, :] += partial_out(slice(r0, r0 + rb))
            _post_residual_to(o_ref, h_ref, gpost_ref, mod_ref, 5, o_ref, r0, rb, inline=True)


def _cast_kernel(x_ref, o_ref):
    o_ref[...] = x_ref[...].astype(BF16)


def _to_bf16(w, br):
    depth, r, c = w.shape
    assert r % br == 0
    spec = pl.BlockSpec((None, br, c), lambda l, i: (l, i, 0))
    return pl.pallas_call(
        _cast_kernel,
        grid=(depth, r // br),
        in_specs=[spec],
        out_specs=spec,
        out_shape=jax.ShapeDtypeStruct(w.shape, BF16),
        compiler_params=_cparams(("parallel", "parallel")),
        name="cast_bf16",
    )(w)


def _ffn(h, g_pre, g_post, mod, wg, wu, wd, layer, rows_per_batch, bm, bf):
    rows, d = h.shape
    dff = wg.shape[2]
    tiles_per_batch = rows_per_batch // bm
    nb = mod.shape[0]

    def row_map(i, f):
        return (i, 0)

    def mod_map(i, f):
        return (i // tiles_per_batch if nb > 1 else 0, 0, 0)

    assert rows % bm == 0 and dff % bf == 0 and dff // bf >= 2
    return pl.pallas_call(
        _ffn_kernel,
        grid=(rows // bm, dff // bf),
        in_specs=[
            pl.BlockSpec((bm, d), row_map),
            _resident((1, d)), _resident((1, d)),
            pl.BlockSpec((None, 8, d), mod_map),
            pl.BlockSpec((None, d, bf), lambda i, f: (layer, 0, f)),
            pl.BlockSpec((None, d, bf), lambda i, f: (layer, 0, f)),
            pl.BlockSpec((None, bf, d), lambda i, f: (layer, f, 0)),
        ],
        out_specs=pl.BlockSpec((bm, d), row_map),
        out_shape=jax.ShapeDtypeStruct((rows, d), F32),
        scratch_shapes=[pltpu.VMEM((bm, d), BF16)],
        compiler_params=_cparams(("parallel", "arbitrary")),
        name="ffn",
    )(h, g_pre, g_post, mod, wg, wu, wd)


def _rope_table(n_tok, dim):
    t = jnp.arange(n_tok, dtype=jnp.int32)
    row = (t // GRID_W).astype(F32)
    col = (t % GRID_W).astype(F32)
    half = dim // 2
    inv_freq = ROPE_BASE ** (-jnp.arange(0, half, 2, dtype=F32) / half)
    ar = row[:, None] * inv_freq[None, :]
    ac = col[:, None] * inv_freq[None, :]
    ang = jnp.concatenate([ar, ar, ac, ac], axis=-1)
    return jnp.cos(ang), jnp.sin(ang)


def _rpb_cols_kernel(r_ref, e_ref, o_ref):
    x = r_ref[...] * LOG2E
    hi = x.astype(BF16)
    r1 = x - hi.astype(F32)
    mid = r1.astype(BF16)
    lo = (r1 - mid.astype(F32)).astype(BF16)
    e = e_ref[...]
    o_ref[...] = (_dot(hi, e) + _dot(mid, e)) + _dot(lo, e)


def _band_bias(rpb, seq):
    t, g = BAND_TILE, GRID_W
    rpt = t // g
    nq = seq // t
    rows = seq // g
    kh = min(NA_KH, rows)
    n_h, n_di, n_dj = rpb.shape
    tiles = (0, 1, nq - 1)
    neg = NEG * LOG2E

    qpos = np.array(tiles)[:, None, None] * t + np.arange(t)[None, :, None]
    kpos = (np.array(tiles)[:, None, None] - 1) * t + np.arange(3 * t)[None, None, :]
    win = (np.abs(kpos - qpos) <= C_WINDOW) & (kpos >= 0) & (kpos < seq)
    bias_c = jnp.asarray(np.where(win, 0.0, neg).astype(np.float32))

    qc = np.arange(g)[:, None]
    kc = np.arange(g)[None, :]
    cs = np.clip(qc - NA_KW // 2, 0, g - NA_KW)
    col_ok = (kc >= cs) & (kc < cs + NA_KW)
    onehot = np.zeros((LANES, g, g), np.float32)
    qq, kk = np.nonzero(col_ok)
    onehot[(kk - qq + NA_KW - 1), qq, kk] = 1.0
    assert n_h * n_di <= LANES and n_dj <= LANES
    rpb2d = jnp.pad(rpb.reshape(n_h * n_di, n_dj).astype(F32), ((0, LANES - n_h * n_di), (0, LANES - n_dj)))
    cols = pl.pallas_call(
        _rpb_cols_kernel,
        out_shape=jax.ShapeDtypeStruct((LANES, g * g), F32),
        name="rpb_columns",
    )(rpb2d, jnp.asarray(onehot.reshape(LANES, g * g), BF16))
    cols = cols[:n_h * n_di].reshape(n_h, n_di, g, g) + jnp.asarray(np.where(col_ok, 0.0, neg).astype(np.float32))
    masked = jnp.full((n_h, g, g), neg, F32)

    tile_rows = []
    for ti in tiles:
        for a in range(rpt):
            qr = ti * rpt + a
            rs = min(max(qr - kh // 2, 0), rows - kh)
            krs = [(ti - 1) * rpt + b for b in range(3 * rpt)]
            tile_rows.append(jnp.concatenate(
                [cols[:, kr - qr + NA_KH - 1] if rs <= kr < rs + kh else masked for kr in krs], axis=-1))
    bias_d = jnp.stack(tile_rows, axis=1).reshape(n_h, len(tiles), t, 3 * t)
    return bias_c, bias_d


def _pad_mod(m):
    nb = m.shape[0]
    return jnp.pad(m.reshape(nb, 6, D_MODEL), ((0, 0), (0, 2), (0, 0)))


def _row(v):
    return v.reshape(1, -1).astype(F32)


def _tiles(batch, seq, n_ctx):
    return dict(
        bm_lat=512,
        bm_ctx=min(512, batch * n_ctx),
        bm_ffn=1024 if seq % 1024 == 0 else 512,
        bf_ffn=512,
        bq_a=min(2048, seq),
        bq_b=min(8192, seq),
        bk=min(2048, seq),
        sub=256,
    )


def kernel(x, c, ctx, c_ctx, w_mod, b_mod, g_mix_pre, g_mix_post, g_ffn_pre, g_ffn_post, w_gate, w_up, w_down,
           ab_w_in, ab_w_out, a_q_norm, a_k_norm, b_q_norm, b_kv_norm, b_w_uq, b_w_ukv, cd_w_in, cd_w_out,
           c_q_norm, c_k_norm, c_sink, d_q_norm, d_k_norm, d_rpb):
    batch, seq, d = x.shape
    n_ctx = ctx.shape[1]
    depth = w_mod.shape[0]
    assert d == D_MODEL and seq % 512 == 0 and n_ctx % 256 == 0 and batch + 1 <= 8
    assert depth == 2

    cvec = jnp.zeros((8, d), F32).at[:batch].set(c).at[batch].set(c_ctx)
    mod_all = _modulation(cvec, w_mod, b_mod)

    cos_h, sin_h = _rope_table(seq, HEAD_DIM)
    cos_r, sin_r = _rope_table(seq, B_ROPE)
    pad_r = ((0, 0), (0, LANES - B_ROPE))
    tabs_ab = (cos_h, sin_h, jnp.pad(cos_r, pad_r), jnp.pad(sin_r, pad_r))
    tabs_cd = (cos_h, sin_h)

    tl = _tiles(batch, seq, n_ctx)
    bm_lat, bm_ctx, bm_ffn, bf_ffn = tl["bm_lat"], tl["bm_ctx"], tl["bm_ffn"], tl["bf_ffn"]
    bq_a, bq_b, bk, sub = tl["bq_a"], tl["bq_b"], tl["bk"], tl["sub"]

    h_lat = x.reshape(batch * seq, d)
    h_ctx = ctx.reshape(batch * n_ctx, d)
    wg, wu, wd = _to_bf16(w_gate, 256), _to_bf16(w_up, 256), _to_bf16(w_down, 512)

    for i in range(depth):
        last = i == depth - 1
        need_ctx = not last
        j = i // 2
        mod_lat = _pad_mod(mod_all[i, :batch])
        mod_ctx = _pad_mod(mod_all[i, batch:batch + 1])
        g1, g2, g3, g4 = _row(g_mix_pre[i]), _row(g_mix_post[i]), _row(g_ffn_pre[i]), _row(g_ffn_post[i])

        if i % 2 == 0:
            w_in = ab_w_in[j]
            kr_cols = jnp.pad(w_in[:, A_IN + B_Q_RANK + B_KV_RANK:], ((0, 0), (0, LANES - B_ROPE)))
            w_ab = jnp.concatenate([w_in[:, :A_IN + B_Q_RANK + B_KV_RANK], kr_cols], axis=1).astype(BF16)
            wuq = jnp.pad(b_w_uq[j].reshape(B_Q_RANK, B_HEADS, B_NOPE + B_ROPE),
                          ((0, 0), (0, 0), (0, B_QK_PAD - B_NOPE - B_ROPE))).reshape(B_Q_RANK, B_HEADS * B_QK_PAD)
            wukv = b_w_ukv[j].reshape(B_KV_RANK, B_HEADS, B_NOPE + B_V)
            wuk = wukv[:, :, :B_NOPE].reshape(B_KV_RANK, B_HEADS * B_NOPE)
            wuv = wukv[:, :, B_NOPE:].reshape(B_KV_RANK, B_HEADS * B_V)
            norms = (_row(a_q_norm[j]), _row(a_k_norm[j]), _row(b_q_norm[j]), _row(b_kv_norm[j]),
                     wuq.astype(BF16), wuk.astype(BF16), wuv.astype(BF16))
            qa, ka, va, qb, kb, vb = _inproj_ab(h_lat, g1, mod_lat, w_ab, *norms, tabs_ab, seq, bm_lat)
            qac, kac, vac, qbc, kbc, vbc = _inproj_ab(h_ctx, g1, mod_ctx, w_ab, *norms, None, batch * n_ctx, bm_ctx)
            rep_a = A_HEADS // A_KV_HEADS
            oa = _flash(qa, ka, va, kac, vac, batch, A_KV_HEADS, rep_a, HEAD_DIM, bq_a, bk, sub)
            ob = _flash(qb, kb, vb, kbc, vbc, batch, B_HEADS, 1, B_QK_PAD, bq_b, bk, sub)
            w_out = ab_w_out[j].astype(BF16)
            attn_lat = [oa, ob]
            if need_ctx:
                oac = _flash(qac, kac, vac, None, None, batch, A_KV_HEADS, rep_a, HEAD_DIM, n_ctx, n_ctx, sub)
                obc = _flash(qbc, kbc, vbc, None, None, batch, B_HEADS, 1, B_QK_PAD, n_ctx, n_ctx, sub)
                attn_ctx = [oac, obc]
        else:
            w_cd = cd_w_in[j].astype(BF16)
            norms = (_row(c_q_norm[j]), _row(c_k_norm[j]), _row(d_q_norm[j]), _row(d_k_norm[j]))
            q_l, kv_l = _inproj_cd(h_lat, g1, mod_lat, w_cd, *norms, tabs_cd, seq, bm_lat)
            _, kv_c = _inproj_cd(h_ctx, g1, mod_ctx, w_cd, *norms, None, batch * n_ctx, bm_ctx)
            bias_c, bias_d = _band_bias(d_rpb[j], seq)
            sink = jnp.broadcast_to((c_sink[j].astype(F32) * LOG2E)[:, None, None], (C_HEADS, 8, LANES))
            kv_d = 2 * D_HEADS * HEAD_DIM
            attn_lat = [
                _band_attention(q_l, kv_l, kv_c, bias_c, sink, batch, 0, kv_d, C_KV_HEADS),
                _band_attention(q_l, kv_l, kv_c, bias_d, None, batch, C_HEADS * HEAD_DIM, 0, D_HEADS),
            ]
            w_out = cd_w_out[j].astype(BF16)

        h_lat = _outproj(attn_lat, w_out, g2, mod_lat, h_lat, seq, bm_lat)
        h_lat = _ffn(h_lat, g3, g4, mod_lat, wg, wu, wd, i, seq, bm_ffn, bf_ffn)
        if need_ctx:
            h_ctx = _outproj(attn_ctx, w_out, g2, mod_ctx, h_ctx, batch * n_ctx, bm_ctx)
            h_ctx = _ffn(h_ctx, g3, g4, mod_ctx, wg, wu, wd, i, batch * n_ctx, bm_ctx, bf_ffn)
    return h_lat.reshape(batch, seq, d)
```

```python
import functools

import jax
import jax.numpy as jnp
import numpy as np
from jax import lax
from jax.experimental import pallas as pl
from jax.experimental.pallas import tpu as pltpu

F32 = jnp.float32
BF16 = jnp.bfloat16

D_MODEL = 2048
GRID_W = 64
HEAD_DIM = 128
ROPE_BASE = 10000.0
EPS = 1e-6
NEG = -1e30
LOG2E = 1.4426950408889634

A_HEADS, A_KV_HEADS = 8, 2
B_HEADS, B_Q_RANK, B_KV_RANK, B_NOPE, B_ROPE, B_V = 8, 512, 512, 128, 64, 128
C_HEADS, C_KV_HEADS, C_WINDOW = 8, 2, 128
D_HEADS, NA_KH, NA_KW = 8, 8, 16
A_IN = (A_HEADS + 2 * A_KV_HEADS) * HEAD_DIM
C_IN = (C_HEADS + 2 * C_KV_HEADS) * HEAD_DIM
D_FF = 5632

LANES = 128
MXU_COLS = 256
B_QK_PAD = 256
BAND_TILE = 256
VMEM_LIMIT = 56 * 1024 * 1024
ROW_CHUNK = 32
PROJ_ROW_BLOCK = 256
FFN_ROW_BLOCK = 256
CHUNK_UNROLL = 4


def _cparams(sem):
    return pltpu.CompilerParams(dimension_semantics=sem, vmem_limit_bytes=VMEM_LIMIT)


def _resident(shape):
    nd = len(shape)
    return pl.BlockSpec(shape, lambda *_: (0,) * nd, pipeline_mode=pl.Buffered(1))


def _silu(x):
    return x * (1.0 / (1.0 + jnp.exp(-x)))


def _rms(x, g):
    ms = jnp.mean(x * x, axis=-1, keepdims=True)
    return x * lax.rsqrt(ms + EPS) * g


def _dot(a, b):
    return jnp.dot(a, b, preferred_element_type=F32)


def _dot_nt(a, b):
    return lax.dot_general(a, b, (((1,), (1,)), ((), ())), preferred_element_type=F32)


def _mod_kernel(c_ref, w_ref, b_ref, o_ref):
    s = _silu(c_ref[...])
    s_hi = s.astype(BF16)
    s_lo = (s - s_hi.astype(F32)).astype(BF16)
    w = w_ref[...]
    w_hi = w.astype(BF16)
    w_lo = (w - w_hi.astype(F32)).astype(BF16)
    o_ref[...] = _dot(s_hi, w_hi) + (_dot(s_lo, w_hi) + _dot(s_hi, w_lo)) + b_ref[...]


def _modulation(cvec, w_mod, b_mod):
    depth, d, n = w_mod.shape
    bn = 1024
    return pl.pallas_call(
        _mod_kernel,
        grid=(depth, n // bn),
        in_specs=[
            pl.BlockSpec((8, d), lambda i, j: (0, 0)),
            pl.BlockSpec((None, d, bn), lambda i, j: (i, 0, j)),
            pl.BlockSpec((None, 1, bn), lambda i, j: (i, 0, j)),
        ],
        out_specs=pl.BlockSpec((None, 8, bn), lambda i, j: (i, 0, j)),
        out_shape=jax.ShapeDtypeStruct((depth, 8, n), F32),
        compiler_params=_cparams(("parallel", "parallel")),
        name="modulation",
    )(cvec, w_mod, b_mod.reshape(depth, 1, n))


def _chunked(fn, n_rows, row0, inline):
    chunk = min(ROW_CHUNK, n_rows)
    if inline:
        for c in range(n_rows // chunk):
            fn(row0 + c * chunk)
        return

    def body(c, carry):
        fn(pl.multiple_of(row0 + c * chunk, chunk))
        return carry

    lax.fori_loop(0, n_rows // chunk, body, 0, unroll=CHUNK_UNROLL)


def _norm_mod_to(h_ref, g_ref, mod_ref, shift_row, scale_row, u_ref, row0=0, n_rows=None, inline=False):
    n_rows = h_ref.shape[0] if n_rows is None else n_rows
    chunk = min(ROW_CHUNK, n_rows)
    gs = g_ref[...] * (1.0 + mod_ref[scale_row:scale_row + 1, :])
    sh = mod_ref[shift_row:shift_row + 1, :]

    def one(r):
        h = h_ref[pl.ds(r, chunk), :]
        inv = lax.rsqrt(jnp.mean(h * h, axis=-1, keepdims=True) + EPS)
        u_ref[pl.ds(r, chunk), :] = ((h * inv) * gs + sh).astype(BF16)

    _chunked(one, n_rows, row0, inline)


def _post_residual_to(acc_ref, h_ref, g_ref, mod_ref, gate_row, o_ref, row0=0, n_rows=None, inline=False):
    n_rows = h_ref.shape[0] if n_rows is None else n_rows
    chunk = min(ROW_CHUNK, n_rows)
    gg = g_ref[...] * mod_ref[gate_row:gate_row + 1, :]

    def one(r):
        a = acc_ref[pl.ds(r, chunk), :]
        inv = lax.rsqrt(jnp.mean(a * a, axis=-1, keepdims=True) + EPS)
        o_ref[pl.ds(r, chunk), :] = h_ref[pl.ds(r, chunk), :] + (a * inv) * gg

    _chunked(one, n_rows, row0, inline)


def _rope(x, cos, sin, quarter):
    lane = lax.broadcasted_iota(jnp.int32, x.shape, 1)
    first = (lane % (2 * quarter)) < quarter
    rot = jnp.where(first, -pltpu.roll(x, LANES - quarter, 1), pltpu.roll(x, quarter, 1))
    return x * cos + rot * sin


def _emit_heads(u_ref, rows, w_ref, col0, n_heads, gain, scale, rope, o_ref, ocol0):
    h = 0
    while h < n_heads:
        nh = min(MXU_COLS // HEAD_DIM, n_heads - h)
        c = col0 + h * HEAD_DIM
        p = _dot(u_ref[rows, :], w_ref[:, c:c + nh * HEAD_DIM])
        for t in range(nh):
            x = p[:, t * HEAD_DIM:(t + 1) * HEAD_DIM]
            if gain is not None:
                x = _rms(x, gain)
                if rope is not None:
                    x = _rope(x, rope[0], rope[1], HEAD_DIM // 4)
                if scale != 1.0:
                    x = x * scale
            oc = ocol0 + (h + t) * HEAD_DIM
            o_ref[rows, oc:oc + HEAD_DIM] = x.astype(BF16)
        h += nh


def _inproj_ab_kernel(use_rope, h_ref, g_ref, mod_ref, w_ref, aqn_ref, akn_ref, bqn_ref, bkvn_ref,
                      wuq_ref, wuk_ref, wuv_ref, cosh_ref, sinh_ref, cosr_ref, sinr_ref,
                      qa_ref, ka_ref, va_ref, qb_ref, kb_ref, vb_ref, u_ref, c_ref):
    bm = h_ref.shape[0]
    rb = min(PROJ_ROW_BLOCK, bm)
    a_scale = HEAD_DIM ** -0.5 * LOG2E
    b_scale = (B_NOPE + B_ROPE) ** -0.5 * LOG2E
    for r0 in range(0, bm, rb):
        rows = slice(r0, r0 + rb)
        _norm_mod_to(h_ref, g_ref, mod_ref, 0, 1, u_ref, r0, rb, inline=True)
        rope_h = (cosh_ref[rows, :], sinh_ref[rows, :]) if use_rope else None
        _emit_heads(u_ref, rows, w_ref, 0, A_HEADS, aqn_ref[...], a_scale, rope_h, qa_ref, 0)
        _emit_heads(u_ref, rows, w_ref, A_HEADS * HEAD_DIM, A_KV_HEADS, akn_ref[...], 1.0, rope_h, ka_ref, 0)
        _emit_heads(u_ref, rows, w_ref, (A_HEADS + A_KV_HEADS) * HEAD_DIM, A_KV_HEADS, None, 1.0, None, va_ref, 0)

        cq = _dot(u_ref[rows, :], w_ref[:, A_IN:A_IN + B_Q_RANK])
        c_ref[rows, :] = _rms(cq, bqn_ref[...]).astype(BF16)
        for hd in range(B_HEADS):
            q = _dot(c_ref[rows, :], wuq_ref[:, hd * B_QK_PAD:(hd + 1) * B_QK_PAD])
            q_nope = q[:, :B_NOPE]
            q_rope = q[:, B_NOPE:]
            if use_rope:
                q_rope = _rope(q_rope, cosr_ref[rows, :], sinr_ref[rows, :], B_ROPE // 4)
            qb_ref[rows, hd * B_QK_PAD:hd * B_QK_PAD + B_NOPE] = (q_nope * b_scale).astype(BF16)
            qb_ref[rows, hd * B_QK_PAD + B_NOPE:(hd + 1) * B_QK_PAD] = (q_rope * b_scale).astype(BF16)
        kr0 = A_IN + B_Q_RANK + B_KV_RANK
        kr = _dot(u_ref[rows, :], w_ref[:, kr0:kr0 + LANES])
        if use_rope:
            kr = _rope(kr, cosr_ref[rows, :], sinr_ref[rows, :], B_ROPE // 4)
        kr = kr.astype(BF16)
        ckv = _dot(u_ref[rows, :], w_ref[:, A_IN + B_Q_RANK:A_IN + B_Q_RANK + B_KV_RANK])
        c_ref[rows, :] = _rms(ckv, bkvn_ref[...]).astype(BF16)
        for pair in range(B_HEADS * B_NOPE // MXU_COLS):
            kn = _dot(c_ref[rows, :], wuk_ref[:, pair * MXU_COLS:(pair + 1) * MXU_COLS])
            for t in range(MXU_COLS // B_NOPE):
                hd = pair * (MXU_COLS // B_NOPE) + t
                kb_ref[rows, hd * B_QK_PAD:hd * B_QK_PAD + B_NOPE] = kn[:, t * B_NOPE:(t + 1) * B_NOPE].astype(BF16)
                kb_ref[rows, hd * B_QK_PAD + B_NOPE:(hd + 1) * B_QK_PAD] = kr
            vb_ref[rows, pair * MXU_COLS:(pair + 1) * MXU_COLS] = _dot(
                c_ref[rows, :], wuv_ref[:, pair * MXU_COLS:(pair + 1) * MXU_COLS]).astype(BF16)


def _rope_specs(rope_tabs, n_tabs, bm, tiles_per_batch):
    if rope_tabs is None:
        return [jnp.zeros((bm, LANES), F32)] * n_tabs, [pl.BlockSpec((bm, LANES), lambda i: (0, 0))] * n_tabs
    return list(rope_tabs), [pl.BlockSpec((bm, LANES), lambda i: (i % tiles_per_batch, 0))] * n_tabs


def _inproj_ab(h, g_pre, mod, w, aqn, akn, bqn, bkvn, wuq, wuk, wuv, rope_tabs, rows_per_batch, bm):
    rows, d = h.shape
    tiles_per_batch = rows_per_batch // bm
    nb = mod.shape[0]

    def row_map(i):
        return (i, 0)

    def mod_map(i):
        return (i // tiles_per_batch if nb > 1 else 0, 0, 0)

    tabs, tab_specs = _rope_specs(rope_tabs, 4, bm, tiles_per_batch)
    outs = [
        (A_HEADS * HEAD_DIM), (A_KV_HEADS * HEAD_DIM), (A_KV_HEADS * HEAD_DIM),
        (B_HEADS * B_QK_PAD), (B_HEADS * B_QK_PAD), (B_HEADS * B_V),
    ]
    return pl.pallas_call(
        functools.partial(_inproj_ab_kernel, rope_tabs is not None),
        grid=(rows // bm,),
        in_specs=[
            pl.BlockSpec((bm, d), row_map),
            _resident((1, d)),
            pl.BlockSpec((None, 8, d), mod_map),
            _resident(w.shape),
            _resident((1, HEAD_DIM)), _resident((1, HEAD_DIM)),
            _resident((1, B_Q_RANK)), _resident((1, B_KV_RANK)),
            _resident(wuq.shape), _resident(wuk.shape), _resident(wuv.shape),
        ] + tab_specs,
        out_specs=[pl.BlockSpec((bm, n), row_map) for n in outs],
        out_shape=[jax.ShapeDtypeStruct((rows, n), BF16) for n in outs],
        scratch_shapes=[pltpu.VMEM((bm, d), BF16), pltpu.VMEM((bm, B_Q_RANK), BF16)],
        compiler_params=_cparams(("parallel",)),
        name="inproj_ab",
    )(h, g_pre, mod, w, aqn, akn, bqn, bkvn, wuq, wuk, wuv, *tabs)


def _inproj_cd_kernel(use_rope, h_ref, g_ref, mod_ref, w_ref, cqn_ref, ckn_ref, dqn_ref, dkn_ref,
                      cosh_ref, sinh_ref, q_ref, kv_ref, u_ref):
    bm = h_ref.shape[0]
    rb = min(PROJ_ROW_BLOCK, bm)
    scale = HEAD_DIM ** -0.5 * LOG2E
    kv_c = C_KV_HEADS * HEAD_DIM
    kv_d = D_HEADS * HEAD_DIM
    for r0 in range(0, bm, rb):
        rows = slice(r0, r0 + rb)
        _norm_mod_to(h_ref, g_ref, mod_ref, 0, 1, u_ref, r0, rb, inline=True)
        rope_h = (cosh_ref[rows, :], sinh_ref[rows, :]) if use_rope else None
        _emit_heads(u_ref, rows, w_ref, 0, C_HEADS, cqn_ref[...], scale, rope_h, q_ref, 0)
        _emit_heads(u_ref, rows, w_ref, C_HEADS * HEAD_DIM, C_KV_HEADS, ckn_ref[...], 1.0, rope_h, kv_ref, 2 * kv_d)
        _emit_heads(u_ref, rows, w_ref, C_HEADS * HEAD_DIM + kv_c, C_KV_HEADS, None, 1.0, None, kv_ref, 2 * kv_d + kv_c)
        _emit_heads(u_ref, rows, w_ref, C_IN, D_HEADS, dqn_ref[...], scale, None, q_ref, C_HEADS * HEAD_DIM)
        _emit_heads(u_ref, rows, w_ref, C_IN + kv_d, D_HEADS, dkn_ref[...], 1.0, None, kv_ref, 0)
        _emit_heads(u_ref, rows, w_ref, C_IN + 2 * kv_d, D_HEADS, None, 1.0, None, kv_ref, kv_d)


def _inproj_cd(h, g_pre, mod, w, cqn, ckn, dqn, dkn, rope_tabs, rows_per_batch, bm):
    rows, d = h.shape
    tiles_per_batch = rows_per_batch // bm
    nb = mod.shape[0]

    def row_map(i):
        return (i, 0)

    def mod_map(i):
        return (i // tiles_per_batch if nb > 1 else 0, 0, 0)

    tabs, tab_specs = _rope_specs(rope_tabs, 2, bm, tiles_per_batch)
    outs = [(C_HEADS + D_HEADS) * HEAD_DIM, 2 * (C_KV_HEADS + D_HEADS) * HEAD_DIM]
    return pl.pallas_call(
        functools.partial(_inproj_cd_kernel, rope_tabs is not None),
        grid=(rows // bm,),
        in_specs=[
            pl.BlockSpec((bm, d), row_map),
            _resident((1, d)),
            pl.BlockSpec((None, 8, d), mod_map),
            _resident(w.shape),
            _resident((1, HEAD_DIM)), _resident((1, HEAD_DIM)), _resident((1, HEAD_DIM)), _resident((1, HEAD_DIM)),
        ] + tab_specs,
        out_specs=[pl.BlockSpec((bm, n), row_map) for n in outs],
        out_shape=[jax.ShapeDtypeStruct((rows, n), BF16) for n in outs],
        scratch_shapes=[pltpu.VMEM((bm, d), BF16)],
        compiler_params=_cparams(("parallel",)),
        name="inproj_cd",
    )(h, g_pre, mod, w, cqn, ckn, dqn, dkn, *tabs)


def _flash_step(q_ref, kt, vt, m_ref, acc_ref, rep, dq, sub):
    bq = q_ref.shape[0]
    n_chunks = kt.shape[0] // LANES
    for r in range(rep):
        for sb in range(bq // sub):
            rows = slice(sb * sub, (sb + 1) * sub)
            mcol = slice(r * LANES, (r + 1) * LANES)
            acol = slice(r * 2 * LANES, (r + 1) * 2 * LANES)
            s = _dot_nt(q_ref[rows, r * dq:(r + 1) * dq], kt)
            chunks = [s[:, c * LANES:(c + 1) * LANES] for c in range(n_chunks)]
            lane_max = functools.reduce(jnp.maximum, chunks)
            m_prev = m_ref[rows, mcol]
            m_new = jnp.maximum(m_prev, jnp.max(lane_max, axis=-1, keepdims=True))
            alpha = jnp.exp2(m_prev - m_new)
            p = jnp.concatenate([jnp.exp2(ch - m_new) for ch in chunks], axis=1).astype(BF16)
            acc_ref[rows, acol] = jnp.concatenate([alpha, alpha], axis=1) * acc_ref[rows, acol] + _dot(p, vt)
            m_ref[rows, mcol] = m_new


def _flash_kernel(rep, dq, has_ctx, sub, *refs):
    if has_ctx:
        q_ref, k_ref, v_ref, kc_ref, vc_ref, o_ref, m_ref, acc_ref = refs
    else:
        q_ref, k_ref, v_ref, o_ref, m_ref, acc_ref = refs
    j = pl.program_id(3)

    def with_ones(v):
        return jnp.concatenate([v, jnp.ones(v.shape, BF16)], axis=1)

    @pl.when(j == 0)
    def _init():
        m_ref[...] = jnp.full(m_ref.shape, NEG, F32)
        acc_ref[...] = jnp.zeros(acc_ref.shape, F32)
        if has_ctx:
            _flash_step(q_ref, kc_ref[...], with_ones(vc_ref[...]), m_ref, acc_ref, rep, dq, sub)

    _flash_step(q_ref, k_ref[...], with_ones(v_ref[...]), m_ref, acc_ref, rep, dq, sub)

    @pl.when(j == pl.num_programs(3) - 1)
    def _fin():
        for r in range(rep):
            c0 = r * 2 * LANES
            o_ref[:, r * HEAD_DIM:(r + 1) * HEAD_DIM] = (
                acc_ref[:, c0:c0 + LANES] / acc_ref[:, c0 + LANES:c0 + 2 * LANES]).astype(BF16)


def _flash(q, k, v, kc, vc, batch, n_groups, rep, dq, bq, bk, sub):
    lq = q.shape[0] // batch
    lk = k.shape[0] // batch
    nq, nk = lq // bq, lk // bk
    has_ctx = kc is not None
    dv = HEAD_DIM
    assert dv == LANES
    in_specs = [
        pl.BlockSpec((bq, rep * dq), lambda b, g, i, j: (b * nq + i, g)),
        pl.BlockSpec((bk, dq), lambda b, g, i, j: (b * nk + j, g)),
        pl.BlockSpec((bk, dv), lambda b, g, i, j: (b * nk + j, g)),
    ]
    args = [q, k, v]
    if has_ctx:
        lc = kc.shape[0] // batch
        in_specs += [
            pl.BlockSpec((lc, dq), lambda b, g, i, j: (b, g)),
            pl.BlockSpec((lc, dv), lambda b, g, i, j: (b, g)),
        ]
        args += [kc, vc]
    return pl.pallas_call(
        functools.partial(_flash_kernel, rep, dq, has_ctx, min(sub, bq)),
        grid=(batch, n_groups, nq, nk),
        in_specs=in_specs,
        out_specs=pl.BlockSpec((bq, rep * dv), lambda b, g, i, j: (b * nq + i, g)),
        out_shape=jax.ShapeDtypeStruct((batch * lq, n_groups * rep * dv), BF16),
        scratch_shapes=[
            pltpu.VMEM((bq, rep * LANES), F32),
            pltpu.VMEM((bq, rep * 2 * LANES), F32),
        ],
        compiler_params=_cparams(("parallel", "parallel", "parallel", "arbitrary")),
        name="flash_attention",
    )(*args)


def _band_kernel(n_q, n_kv, per_head_bias, has_sink, *refs):
    if has_sink:
        q_ref, kvp_ref, kvx_ref, kvn_ref, kvc_ref, bias_ref, sink_ref, o_ref = refs
    else:
        q_ref, kvp_ref, kvx_ref, kvn_ref, kvc_ref, bias_ref, o_ref = refs
    t = BAND_TILE
    kv_refs = (kvc_ref, kvp_ref, kvx_ref, kvn_ref)
    lc = kvc_ref.shape[0]
    for r in range(n_q):
        hcol = slice(r * HEAD_DIM, (r + 1) * HEAD_DIM)
        kv_head = r * n_kv // n_q
        kcol = slice(kv_head * HEAD_DIM, (kv_head + 1) * HEAD_DIM)
        vcol = slice((n_kv + kv_head) * HEAD_DIM, (n_kv + kv_head + 1) * HEAD_DIM)
        bias = bias_ref.at[r] if per_head_bias else bias_ref
        keys = jnp.concatenate([ref[:, kcol] for ref in kv_refs], axis=0)
        vals = jnp.concatenate([ref[:, vcol] for ref in kv_refs], axis=0)
        vals = jnp.concatenate([vals, jnp.ones(vals.shape, BF16)], axis=1)
        s = _dot_nt(q_ref[:, hcol], keys)
        chunks = [s[:, c * LANES:(c + 1) * LANES] for c in range(lc // LANES)]
        chunks += [s[:, lc + c * LANES:lc + (c + 1) * LANES] + bias[:, c * LANES:(c + 1) * LANES]
                   for c in range(3 * t // LANES)]
        m = jnp.max(functools.reduce(jnp.maximum, chunks), axis=-1, keepdims=True)
        if has_sink:
            sink = sink_ref[r, 0:1, 0:1]
            m = jnp.maximum(m, sink)
        p = jnp.concatenate([jnp.exp2(ch - m) for ch in chunks], axis=1).astype(BF16)
        acc = _dot(p, vals)
        l = acc[:, HEAD_DIM:]
        if has_sink:
            l = l + jnp.exp2(sink - m)
        o_ref[:, hcol] = (acc[:, :HEAD_DIM] / l).astype(BF16)


def _band_attention(q, kv, kvc, bias, sink, batch, q_col0, kv_col0, n_kv):
    t = BAND_TILE
    n_q = 8
    seq = q.shape[0] // batch
    nq = seq // t
    lc = kvc.shape[0] // batch
    qw = n_q * HEAD_DIM
    kw = 2 * n_kv * HEAD_DIM
    has_sink = sink is not None
    per_head_bias = bias.ndim == 4

    def tile_map(off):
        def f(b, i):
            return (b * nq + jnp.clip(i + off, 0, nq - 1), kv_col0 // kw)
        return f

    def cls(i):
        return jnp.where(i == 0, 0, jnp.where(i == nq - 1, 2, 1))

    if per_head_bias:
        bias_spec = pl.BlockSpec((n_q, None, t, 3 * t), lambda b, i: (0, cls(i), 0, 0))
    else:
        bias_spec = pl.BlockSpec((None, t, 3 * t), lambda b, i: (cls(i), 0, 0))
    in_specs = [pl.BlockSpec((t, qw), lambda b, i: (b * nq + i, q_col0 // qw))]
    in_specs += [pl.BlockSpec((t, kw), tile_map(off)) for off in (-1, 0, 1)]
    in_specs += [pl.BlockSpec((lc, kw), lambda b, i: (b, kv_col0 // kw)), bias_spec]
    args = [q, kv, kv, kv, kvc, bias]
    if has_sink:
        in_specs.append(pl.BlockSpec((n_q, 8, LANES), lambda b, i: (0, 0, 0)))
        args.append(sink)
    return pl.pallas_call(
        functools.partial(_band_kernel, n_q, n_kv, per_head_bias, has_sink),
        grid=(batch, nq),
        in_specs=in_specs,
        out_specs=pl.BlockSpec((t, qw), lambda b, i: (b * nq + i, 0)),
        out_shape=jax.ShapeDtypeStruct((batch * seq, qw), BF16),
        compiler_params=_cparams(("parallel", "parallel")),
        name="band_attention",
    )(*args)


def _outproj_kernel(n_in, *refs):
    a_refs = refs[:n_in]
    w_ref, g_ref, mod_ref, h_ref, o_ref, acc_ref = refs[n_in:]
    bm = h_ref.shape[0]
    rb = min(PROJ_ROW_BLOCK, bm)
    for r0 in range(0, bm, rb):
        k0 = 0
        acc = None
        for a_ref in a_refs:
            kk = a_ref.shape[1]
            part = _dot(a_ref[r0:r0 + rb, :], w_ref[k0:k0 + kk, :])
            acc = part if acc is None else acc + part
            k0 += kk
        acc_ref[r0:r0 + rb, :] = acc
        _post_residual_to(acc_ref, h_ref, g_ref, mod_ref, 2, o_ref, r0, rb, inline=True)


def _outproj(a_list, w, g_post, mod, h, rows_per_batch, bm):
    rows, d = h.shape
    tiles_per_batch = rows_per_batch // bm
    nb = mod.shape[0]

    def row_map(i):
        return (i, 0)

    def mod_map(i):
        return (i // tiles_per_batch if nb > 1 else 0, 0, 0)

    return pl.pallas_call(
        functools.partial(_outproj_kernel, len(a_list)),
        grid=(rows // bm,),
        in_specs=[pl.BlockSpec((bm, a.shape[1]), row_map) for a in a_list] + [
            _resident(w.shape),
            _resident((1, d)),
            pl.BlockSpec((None, 8, d), mod_map),
            pl.BlockSpec((bm, d), row_map),
        ],
        out_specs=pl.BlockSpec((bm, d), row_map),
        out_shape=jax.ShapeDtypeStruct((rows, d), F32),
        scratch_shapes=[pltpu.VMEM((bm, d), F32)],
        compiler_params=_cparams(("parallel",)),
        name="outproj",
    )(*a_list, w, g_post, mod, h)


def _ffn_kernel(h_ref, gpre_ref, gpost_ref, mod_ref, wg_ref, wu_ref, wd_ref, o_ref, u_ref):
    f = pl.program_id(1)
    last = pl.num_programs(1) - 1
    bm = h_ref.shape[0]
    rb = min(FFN_ROW_BLOCK, bm)

    def partial_out(rows):
        u = u_ref[rows, :]
        act = (_silu(_dot(u, wg_ref[...])) * _dot(u, wu_ref[...])).astype(BF16)
        return _dot(act, wd_ref[...])

    @pl.when(f == 0)
    def _first():
        for r0 in range(0, bm, rb):
            _norm_mod_to(h_ref, gpre_ref, mod_ref, 3, 4, u_ref, r0, rb, inline=True)
            o_ref[r0:r0 + rb, :] = partial_out(slice(r0, r0 + rb))

    @pl.when(jnp.logical_and(f > 0, f < last))
    def _mid():
        o_ref[...] += partial_out(slice(None))

    @pl.when(f == last)
    def _last():
        for r0 in range(0, bm, rb):
            o_ref[r0:r0 + rb, :] += partial_out(slice(r0, r0 + rb))
            _post_residual_to(o_ref, h_ref, gpost_ref, mod_ref, 5, o_ref, r0, rb, inline=True)


def _cast_kernel(x_ref, o_ref):
    o_ref[...] = x_ref[...].astype(BF16)


def _cast_chunks_kernel(x_ref, o_ref):
    cc = o_ref.shape[-1]
    for f in range(o_ref.shape[0]):
        o_ref[f] = x_ref[:, f * cc:(f + 1) * cc].astype(BF16)


def _to_bf16(w, br, col_chunk=None):
    depth, r, c = w.shape
    assert r % br == 0
    spec = pl.BlockSpec((None, br, c), lambda l, i: (l, i, 0))
    if col_chunk is None:
        kern, out_spec, out_shape = _cast_kernel, spec, w.shape
    else:
        assert c % col_chunk == 0
        n_chunks = c // col_chunk
        kern = _cast_chunks_kernel
        out_spec = pl.BlockSpec((None, n_chunks, br, col_chunk), lambda l, i: (l, 0, i, 0))
        out_shape = (depth, n_chunks, r, col_chunk)
    return pl.pallas_call(
        kern,
        grid=(depth, r // br),
        in_specs=[spec],
        out_specs=out_spec,
        out_shape=jax.ShapeDtypeStruct(out_shape, BF16),
        compiler_params=_cparams(("parallel", "parallel")),
        name="cast_bf16",
    )(w)


def _ffn(h, g_pre, g_post, mod, wg, wu, wd, layer, rows_per_batch, bm, bf):
    rows, d = h.shape
    dff = wd.shape[1]
    assert wg.shape[1:] == (dff // bf, d, bf) and wu.shape == wg.shape
    tiles_per_batch = rows_per_batch // bm
    nb = mod.shape[0]

    def row_map(i, f):
        return (i, 0)

    def mod_map(i, f):
        return (i // tiles_per_batch if nb > 1 else 0, 0, 0)

    assert rows % bm == 0 and dff % bf == 0 and dff // bf >= 2
    return pl.pallas_call(
        _ffn_kernel,
        grid=(rows // bm, dff // bf),
        in_specs=[
            pl.BlockSpec((bm, d), row_map),
            _resident((1, d)), _resident((1, d)),
            pl.BlockSpec((None, 8, d), mod_map),
            pl.BlockSpec((None, None, d, bf), lambda i, f: (layer, f, 0, 0)),
            pl.BlockSpec((None, None, d, bf), lambda i, f: (layer, f, 0, 0)),
            pl.BlockSpec((None, bf, d), lambda i, f: (layer, f, 0)),
        ],
        out_specs=pl.BlockSpec((bm, d), row_map),
        out_shape=jax.ShapeDtypeStruct((rows, d), F32),
        scratch_shapes=[pltpu.VMEM((bm, d), BF16)],
        compiler_params=_cparams(("parallel", "arbitrary")),
        name="ffn",
    )(h, g_pre, g_post, mod, wg, wu, wd)


def _rope_table(n_tok, dim):
    assert n_tok % GRID_W == 0
    n_rows = n_tok // GRID_W
    half = dim // 2
    inv_freq = ROPE_BASE ** (-jnp.arange(0, half, 2, dtype=F32) / half)
    ar = jnp.arange(n_rows, dtype=jnp.int32).astype(F32)[:, None] * inv_freq[None, :]
    ac = jnp.arange(GRID_W, dtype=jnp.int32).astype(F32)[:, None] * inv_freq[None, :]

    def table(fn):
        r = jnp.concatenate([fn(ar), fn(ar)], axis=-1)
        c = jnp.concatenate([fn(ac), fn(ac)], axis=-1)
        full = jnp.concatenate([jnp.broadcast_to(r[:, None, :], (n_rows, GRID_W, half)),
                                jnp.broadcast_to(c[None, :, :], (n_rows, GRID_W, half))], axis=-1)
        return full.reshape(n_tok, dim)

    return table(jnp.cos), table(jnp.sin)


def _rpb_cols_kernel(r_ref, e_ref, o_ref):
    x = r_ref[...] * LOG2E
    hi = x.astype(BF16)
    r1 = x - hi.astype(F32)
    mid = r1.astype(BF16)
    lo = (r1 - mid.astype(F32)).astype(BF16)
    e = e_ref[...]
    o_ref[...] = (_dot(hi, e) + _dot(mid, e)) + _dot(lo, e)


def _band_bias(rpb, seq):
    t, g = BAND_TILE, GRID_W
    rpt = t // g
    nq = seq // t
    rows = seq // g
    kh = min(NA_KH, rows)
    n_h, n_di, n_dj = rpb.shape
    tiles = (0, 1, nq - 1)
    neg = NEG * LOG2E

    qpos = np.array(tiles)[:, None, None] * t + np.arange(t)[None, :, None]
    kpos = (np.array(tiles)[:, None, None] - 1) * t + np.arange(3 * t)[None, None, :]
    win = (np.abs(kpos - qpos) <= C_WINDOW) & (kpos >= 0) & (kpos < seq)
    bias_c = jnp.asarray(np.where(win, 0.0, neg).astype(np.float32))

    qc = np.arange(g)[:, None]
    kc = np.arange(g)[None, :]
    cs = np.clip(qc - NA_KW // 2, 0, g - NA_KW)
    col_ok = (kc >= cs) & (kc < cs + NA_KW)
    onehot = np.zeros((LANES, g, g), np.float32)
    qq, kk = np.nonzero(col_ok)
    onehot[(kk - qq + NA_KW - 1), qq, kk] = 1.0
    assert n_h * n_di <= LANES and n_dj <= LANES
    rpb2d = jnp.pad(rpb.reshape(n_h * n_di, n_dj).astype(F32), ((0, LANES - n_h * n_di), (0, LANES - n_dj)))
    cols = pl.pallas_call(
        _rpb_cols_kernel,
        out_shape=jax.ShapeDtypeStruct((LANES, g * g), F32),
        name="rpb_columns",
    )(rpb2d, jnp.asarray(onehot.reshape(LANES, g * g), BF16))
    cols = cols[:n_h * n_di].reshape(n_h, n_di, g, g) + jnp.asarray(np.where(col_ok, 0.0, neg).astype(np.float32))
    masked = jnp.full((n_h, g, g), neg, F32)

    tile_rows = []
    for ti in tiles:
        for a in range(rpt):
            qr = ti * rpt + a
            rs = min(max(qr - kh // 2, 0), rows - kh)
            krs = [(ti - 1) * rpt + b for b in range(3 * rpt)]
            tile_rows.append(jnp.concatenate(
                [cols[:, kr - qr + NA_KH - 1] if rs <= kr < rs + kh else masked for kr in krs], axis=-1))
    bias_d = jnp.stack(tile_rows, axis=1).reshape(n_h, len(tiles), t, 3 * t)
    return bias_c, bias_d


def _pad_mod(m):
    nb = m.shape[0]
    return jnp.pad(m.reshape(nb, 6, D_MODEL), ((0, 0), (0, 2), (0, 0)))


def _row(v):
    return v.reshape(1, -1).astype(F32)


def _tiles(batch, seq, n_ctx):
    return dict(
        bm_lat=512,
        bm_ctx=min(512, batch * n_ctx),
        bm_ffn=1024 if seq % 1024 == 0 else 512,
        bf_ffn=512,
        bq_a=min(2048, seq),
        bq_b=min(8192, seq),
        bk=min(2048, seq),
        sub=256,
    )


def kernel(x, c, ctx, c_ctx, w_mod, b_mod, g_mix_pre, g_mix_post, g_ffn_pre, g_ffn_post, w_gate, w_up, w_down,
           ab_w_in, ab_w_out, a_q_norm, a_k_norm, b_q_norm, b_kv_norm, b_w_uq, b_w_ukv, cd_w_in, cd_w_out,
           c_q_norm, c_k_norm, c_sink, d_q_norm, d_k_norm, d_rpb):
    batch, seq, d = x.shape
    n_ctx = ctx.shape[1]
    depth = w_mod.shape[0]
    assert d == D_MODEL and seq % 512 == 0 and n_ctx % 256 == 0 and batch + 1 <= 8
    assert depth == 2

    cvec = jnp.zeros((8, d), F32).at[:batch].set(c).at[batch].set(c_ctx)
    mod_all = _modulation(cvec, w_mod, b_mod)

    cos_h, sin_h = _rope_table(seq, HEAD_DIM)
    cos_r, sin_r = _rope_table(seq, B_ROPE)
    pad_r = ((0, 0), (0, LANES - B_ROPE))
    tabs_ab = (cos_h, sin_h, jnp.pad(cos_r, pad_r), jnp.pad(sin_r, pad_r))
    tabs_cd = (cos_h, sin_h)

    tl = _tiles(batch, seq, n_ctx)
    bm_lat, bm_ctx, bm_ffn, bf_ffn = tl["bm_lat"], tl["bm_ctx"], tl["bm_ffn"], tl["bf_ffn"]
    bq_a, bq_b, bk, sub = tl["bq_a"], tl["bq_b"], tl["bk"], tl["sub"]

    h_lat = x.reshape(batch * seq, d)
    h_ctx = ctx.reshape(batch * n_ctx, d)
    wg, wu = _to_bf16(w_gate, 256, bf_ffn), _to_bf16(w_up, 256, bf_ffn)
    wd = _to_bf16(w_down, 512)

    for i in range(depth):
        last = i == depth - 1
        need_ctx = not last
        j = i // 2
        mod_lat = _pad_mod(mod_all[i, :batch])
        mod_ctx = _pad_mod(mod_all[i, batch:batch + 1])
        g1, g2, g3, g4 = _row(g_mix_pre[i]), _row(g_mix_post[i]), _row(g_ffn_pre[i]), _row(g_ffn_post[i])

        if i % 2 == 0:
            w_in = ab_w_in[j]
            kr_cols = jnp.pad(w_in[:, A_IN + B_Q_RANK + B_KV_RANK:], ((0, 0), (0, LANES - B_ROPE)))
            w_ab = jnp.concatenate([w_in[:, :A_IN + B_Q_RANK + B_KV_RANK], kr_cols], axis=1).astype(BF16)
            wuq = jnp.pad(b_w_uq[j].reshape(B_Q_RANK, B_HEADS, B_NOPE + B_ROPE),
                          ((0, 0), (0, 0), (0, B_QK_PAD - B_NOPE - B_ROPE))).reshape(B_Q_RANK, B_HEADS * B_QK_PAD)
            wukv = b_w_ukv[j].reshape(B_KV_RANK, B_HEADS, B_NOPE + B_V)
            wuk = wukv[:, :, :B_NOPE].reshape(B_KV_RANK, B_HEADS * B_NOPE)
            wuv = wukv[:, :, B_NOPE:].reshape(B_KV_RANK, B_HEADS * B_V)
            norms = (_row(a_q_norm[j]), _row(a_k_norm[j]), _row(b_q_norm[j]), _row(b_kv_norm[j]),
                     wuq.astype(BF16), wuk.astype(BF16), wuv.astype(BF16))
            qa, ka, va, qb, kb, vb = _inproj_ab(h_lat, g1, mod_lat, w_ab, *norms, tabs_ab, seq, bm_lat)
            qac, kac, vac, qbc, kbc, vbc = _inproj_ab(h_ctx, g1, mod_ctx, w_ab, *norms, None, batch * n_ctx, bm_ctx)
            rep_a = A_HEADS // A_KV_HEADS
            oa = _flash(qa, ka, va, kac, vac, batch, A_KV_HEADS, rep_a, HEAD_DIM, bq_a, bk, sub)
            ob = _flash(qb, kb, vb, kbc, vbc, batch, B_HEADS, 1, B_QK_PAD, bq_b, bk, sub)
            w_out = ab_w_out[j].astype(BF16)
            attn_lat = [oa, ob]
            if need_ctx:
                oac = _flash(qac, kac, vac, None, None, batch, A_KV_HEADS, rep_a, HEAD_DIM, n_ctx, n_ctx, sub)
                obc = _flash(qbc, kbc, vbc, None, None, batch, B_HEADS, 1, B_QK_PAD, n_ctx, n_ctx, sub)
                attn_ctx = [oac, obc]
        else:
            w_cd = cd_w_in[j].astype(BF16)
            norms = (_row(c_q_norm[j]), _row(c_k_norm[j]), _row(d_q_norm[j]), _row(d_k_norm[j]))
            q_l, kv_l = _inproj_cd(h_lat, g1, mod_lat, w_cd, *norms, tabs_cd, seq, bm_lat)
            _, kv_c = _inproj_cd(h_ctx, g1, mod_ctx, w_cd, *norms, None, batch * n_ctx, bm_ctx)
            bias_c, bias_d = _band_bias(d_rpb[j], seq)
            sink = jnp.broadcast_to((c_sink[j].astype(F32) * LOG2E)[:, None, None], (C_HEADS, 8, LANES))
            kv_d = 2 * D_HEADS * HEAD_DIM
            attn_lat = [
                _band_attention(q_l, kv_l, kv_c, bias_c, sink, batch, 0, kv_d, C_KV_HEADS),
                _band_attention(q_l, kv_l, kv_c, bias_d, None, batch, C_HEADS * HEAD_DIM, 0, D_HEADS),
            ]
            w_out = cd_w_out[j].astype(BF16)

        h_lat = _outproj(attn_lat, w_out, g2, mod_lat, h_lat, seq, bm_lat)
        h_lat = _ffn(h_lat, g3, g4, mod_lat, wg, wu, wd, i, seq, bm_ffn, bf_ffn)
        if need_ctx:
            h_ctx = _outproj(attn_ctx, w_out, g2, mod_ctx, h_ctx, batch * n_ctx, bm_ctx)
            h_ctx = _ffn(h_ctx, g3, g4, mod_ctx, wg, wu, wd, i, batch * n_ctx, bm_ctx, bf_ffn)
    return h_lat.reshape(batch, seq, d)
```

```python
import functools

import jax
import jax.numpy as jnp
import numpy as np
from jax import lax
from jax.experimental import pallas as pl
from jax.experimental.pallas import tpu as pltpu

F32 = jnp.float32
BF16 = jnp.bfloat16

D_MODEL = 2048
GRID_W = 64
HEAD_DIM = 128
ROPE_BASE = 10000.0
EPS = 1e-6
NEG = -1e30
LOG2E = 1.4426950408889634

A_HEADS, A_KV_HEADS = 8, 2
B_HEADS, B_Q_RANK, B_KV_RANK, B_NOPE, B_ROPE, B_V = 8, 512, 512, 128, 64, 128
C_HEADS, C_KV_HEADS, C_WINDOW = 8, 2, 128
D_HEADS, NA_KH, NA_KW = 8, 8, 16
A_IN = (A_HEADS + 2 * A_KV_HEADS) * HEAD_DIM
C_IN = (C_HEADS + 2 * C_KV_HEADS) * HEAD_DIM
D_FF = 5632

LANES = 128
MXU_COLS = 256
B_QK_PAD = 256
BAND_TILE = 256
VMEM_LIMIT = 56 * 1024 * 1024
ROW_CHUNK = 32
PROJ_ROW_BLOCK = 256
FFN_ROW_BLOCK = 256
CHUNK_UNROLL = 4


def _cparams(sem):
    return pltpu.CompilerParams(dimension_semantics=sem, vmem_limit_bytes=VMEM_LIMIT)


def _resident(shape):
    nd = len(shape)
    return pl.BlockSpec(shape, lambda *_: (0,) * nd, pipeline_mode=pl.Buffered(1))


def _silu(x):
    return x * (1.0 / (1.0 + jnp.exp(-x)))


def _rms(x, g):
    ms = jnp.mean(x * x, axis=-1, keepdims=True)
    return x * lax.rsqrt(ms + EPS) * g


def _dot(a, b):
    return jnp.dot(a, b, preferred_element_type=F32)


def _dot_nt(a, b):
    return lax.dot_general(a, b, (((1,), (1,)), ((), ())), preferred_element_type=F32)


def _mod_kernel(c_ref, w_ref, b_ref, o_ref):
    s = _silu(c_ref[...])
    s_hi = s.astype(BF16)
    s_lo = (s - s_hi.astype(F32)).astype(BF16)
    w = w_ref[...]
    w_hi = w.astype(BF16)
    w_lo = (w - w_hi.astype(F32)).astype(BF16)
    o_ref[...] = _dot(s_hi, w_hi) + (_dot(s_lo, w_hi) + _dot(s_hi, w_lo)) + b_ref[...]


def _modulation(cvec, w_mod, b_mod):
    depth, d, n = w_mod.shape
    bn = 1024
    return pl.pallas_call(
        _mod_kernel,
        grid=(depth, n // bn),
        in_specs=[
            pl.BlockSpec((8, d), lambda i, j: (0, 0)),
            pl.BlockSpec((None, d, bn), lambda i, j: (i, 0, j)),
            pl.BlockSpec((None, 1, bn), lambda i, j: (i, 0, j)),
        ],
        out_specs=pl.BlockSpec((None, 8, bn), lambda i, j: (i, 0, j)),
        out_shape=jax.ShapeDtypeStruct((depth, 8, n), F32),
        compiler_params=_cparams(("parallel", "parallel")),
        name="modulation",
    )(cvec, w_mod, b_mod.reshape(depth, 1, n))


def _chunked(fn, n_rows, row0, inline):
    chunk = min(ROW_CHUNK, n_rows)
    if inline:
        for c in range(n_rows // chunk):
            fn(row0 + c * chunk)
        return

    def body(c, carry):
        fn(pl.multiple_of(row0 + c * chunk, chunk))
        return carry

    lax.fori_loop(0, n_rows // chunk, body, 0, unroll=CHUNK_UNROLL)


def _norm_mod_to(h_ref, g_ref, mod_ref, shift_row, scale_row, u_ref, row0=0, n_rows=None, inline=False):
    n_rows = h_ref.shape[0] if n_rows is None else n_rows
    chunk = min(ROW_CHUNK, n_rows)
    gs = g_ref[...] * (1.0 + mod_ref[scale_row:scale_row + 1, :])
    sh = mod_ref[shift_row:shift_row + 1, :]

    def one(r):
        h = h_ref[pl.ds(r, chunk), :]
        inv = lax.rsqrt(jnp.mean(h * h, axis=-1, keepdims=True) + EPS)
        u_ref[pl.ds(r, chunk), :] = ((h * inv) * gs + sh).astype(BF16)

    _chunked(one, n_rows, row0, inline)


def _post_residual_to(acc_ref, h_ref, g_ref, mod_ref, gate_row, o_ref, row0=0, n_rows=None, inline=False):
    n_rows = h_ref.shape[0] if n_rows is None else n_rows
    chunk = min(ROW_CHUNK, n_rows)
    gg = g_ref[...] * mod_ref[gate_row:gate_row + 1, :]

    def one(r):
        a = acc_ref[pl.ds(r, chunk), :]
        inv = lax.rsqrt(jnp.mean(a * a, axis=-1, keepdims=True) + EPS)
        o_ref[pl.ds(r, chunk), :] = h_ref[pl.ds(r, chunk), :] + (a * inv) * gg

    _chunked(one, n_rows, row0, inline)


def _rope(x, cos, sin, quarter):
    lane = lax.broadcasted_iota(jnp.int32, x.shape, 1)
    first = (lane % (2 * quarter)) < quarter
    rot = jnp.where(first, -pltpu.roll(x, LANES - quarter, 1), pltpu.roll(x, quarter, 1))
    return x * cos + rot * sin


def _emit_heads(u_ref, rows, w_ref, col0, n_heads, gain, scale, rope, o_ref, ocol0):
    h = 0
    while h < n_heads:
        nh = min(MXU_COLS // HEAD_DIM, n_heads - h)
        c = col0 + h * HEAD_DIM
        p = _dot(u_ref[rows, :], w_ref[:, c:c + nh * HEAD_DIM])
        for t in range(nh):
            x = p[:, t * HEAD_DIM:(t + 1) * HEAD_DIM]
            if gain is not None:
                x = _rms(x, gain)
                if rope is not None:
                    x = _rope(x, rope[0], rope[1], HEAD_DIM // 4)
                if scale != 1.0:
                    x = x * scale
            oc = ocol0 + (h + t) * HEAD_DIM
            o_ref[rows, oc:oc + HEAD_DIM] = x.astype(BF16)
        h += nh


def _inproj_ab_kernel(use_rope, h_ref, g_ref, mod_ref, w_ref, aqn_ref, akn_ref, bqn_ref, bkvn_ref,
                      wuq_ref, wuk_ref, wuv_ref, cosh_ref, sinh_ref, cosr_ref, sinr_ref,
                      qa_ref, ka_ref, va_ref, qb_ref, kb_ref, vb_ref, u_ref, c_ref):
    bm = h_ref.shape[0]
    rb = min(PROJ_ROW_BLOCK, bm)
    a_scale = HEAD_DIM ** -0.5 * LOG2E
    b_scale = (B_NOPE + B_ROPE) ** -0.5 * LOG2E
    for r0 in range(0, bm, rb):
        rows = slice(r0, r0 + rb)
        _norm_mod_to(h_ref, g_ref, mod_ref, 0, 1, u_ref, r0, rb, inline=True)
        rope_h = (cosh_ref[rows, :], sinh_ref[rows, :]) if use_rope else None
        _emit_heads(u_ref, rows, w_ref, 0, A_HEADS, aqn_ref[...], a_scale, rope_h, qa_ref, 0)
        _emit_heads(u_ref, rows, w_ref, A_HEADS * HEAD_DIM, A_KV_HEADS, akn_ref[...], 1.0, rope_h, ka_ref, 0)
        _emit_heads(u_ref, rows, w_ref, (A_HEADS + A_KV_HEADS) * HEAD_DIM, A_KV_HEADS, None, 1.0, None, va_ref, 0)

        cq = _dot(u_ref[rows, :], w_ref[:, A_IN:A_IN + B_Q_RANK])
        c_ref[rows, :] = _rms(cq, bqn_ref[...]).astype(BF16)
        for hd in range(B_HEADS):
            q = _dot(c_ref[rows, :], wuq_ref[:, hd * B_QK_PAD:(hd + 1) * B_QK_PAD])
            q_nope = q[:, :B_NOPE]
            q_rope = q[:, B_NOPE:]
            if use_rope:
                q_rope = _rope(q_rope, cosr_ref[rows, :], sinr_ref[rows, :], B_ROPE // 4)
            qb_ref[rows, hd * B_QK_PAD:hd * B_QK_PAD + B_NOPE] = (q_nope * b_scale).astype(BF16)
            qb_ref[rows, hd * B_QK_PAD + B_NOPE:(hd + 1) * B_QK_PAD] = (q_rope * b_scale).astype(BF16)
        kr0 = A_IN + B_Q_RANK + B_KV_RANK
        kr = _dot(u_ref[rows, :], w_ref[:, kr0:kr0 + LANES])
        if use_rope:
            kr = _rope(kr, cosr_ref[rows, :], sinr_ref[rows, :], B_ROPE // 4)
        kr = kr.astype(BF16)
        ckv = _dot(u_ref[rows, :], w_ref[:, A_IN + B_Q_RANK:A_IN + B_Q_RANK + B_KV_RANK])
        c_ref[rows, :] = _rms(ckv, bkvn_ref[...]).astype(BF16)
        for pair in range(B_HEADS * B_NOPE // MXU_COLS):
            kn = _dot(c_ref[rows, :], wuk_ref[:, pair * MXU_COLS:(pair + 1) * MXU_COLS])
            for t in range(MXU_COLS // B_NOPE):
                hd = pair * (MXU_COLS // B_NOPE) + t
                kb_ref[rows, hd * B_QK_PAD:hd * B_QK_PAD + B_NOPE] = kn[:, t * B_NOPE:(t + 1) * B_NOPE].astype(BF16)
                kb_ref[rows, hd * B_QK_PAD + B_NOPE:(hd + 1) * B_QK_PAD] = kr
            vb_ref[rows, pair * MXU_COLS:(pair + 1) * MXU_COLS] = _dot(
                c_ref[rows, :], wuv_ref[:, pair * MXU_COLS:(pair + 1) * MXU_COLS]).astype(BF16)


def _rope_specs(rope_tabs, n_tabs, bm, tiles_per_batch):
    if rope_tabs is None:
        return [jnp.zeros((bm, LANES), F32)] * n_tabs, [pl.BlockSpec((bm, LANES), lambda i: (0, 0))] * n_tabs
    return list(rope_tabs), [pl.BlockSpec((bm, LANES), lambda i: (i % tiles_per_batch, 0))] * n_tabs


def _inproj_ab(h, g_pre, mod, w, aqn, akn, bqn, bkvn, wuq, wuk, wuv, rope_tabs, rows_per_batch, bm):
    rows, d = h.shape
    tiles_per_batch = rows_per_batch // bm
    nb = mod.shape[0]

    def row_map(i):
        return (i, 0)

    def mod_map(i):
        return (i // tiles_per_batch if nb > 1 else 0, 0, 0)

    tabs, tab_specs = _rope_specs(rope_tabs, 4, bm, tiles_per_batch)
    outs = [
        (A_HEADS * HEAD_DIM), (A_KV_HEADS * HEAD_DIM), (A_KV_HEADS * HEAD_DIM),
        (B_HEADS * B_QK_PAD), (B_HEADS * B_QK_PAD), (B_HEADS * B_V),
    ]
    return pl.pallas_call(
        functools.partial(_inproj_ab_kernel, rope_tabs is not None),
        grid=(rows // bm,),
        in_specs=[
            pl.BlockSpec((bm, d), row_map),
            _resident((1, d)),
            pl.BlockSpec((None, 8, d), mod_map),
            _resident(w.shape),
            _resident((1, HEAD_DIM)), _resident((1, HEAD_DIM)),
            _resident((1, B_Q_RANK)), _resident((1, B_KV_RANK)),
            _resident(wuq.shape), _resident(wuk.shape), _resident(wuv.shape),
        ] + tab_specs,
        out_specs=[pl.BlockSpec((bm, n), row_map) for n in outs],
        out_shape=[jax.ShapeDtypeStruct((rows, n), BF16) for n in outs],
        scratch_shapes=[pltpu.VMEM((bm, d), BF16), pltpu.VMEM((bm, B_Q_RANK), BF16)],
        compiler_params=_cparams(("parallel",)),
        name="inproj_ab",
    )(h, g_pre, mod, w, aqn, akn, bqn, bkvn, wuq, wuk, wuv, *tabs)


def _inproj_cd_kernel(use_rope, h_ref, g_ref, mod_ref, w_ref, cqn_ref, ckn_ref, dqn_ref, dkn_ref,
                      cosh_ref, sinh_ref, q_ref, kv_ref, u_ref):
    bm = h_ref.shape[0]
    rb = min(PROJ_ROW_BLOCK, bm)
    scale = HEAD_DIM ** -0.5 * LOG2E
    kv_c = C_KV_HEADS * HEAD_DIM
    kv_d = D_HEADS * HEAD_DIM
    for r0 in range(0, bm, rb):
        rows = slice(r0, r0 + rb)
        _norm_mod_to(h_ref, g_ref, mod_ref, 0, 1, u_ref, r0, rb, inline=True)
        rope_h = (cosh_ref[rows, :], sinh_ref[rows, :]) if use_rope else None
        _emit_heads(u_ref, rows, w_ref, 0, C_HEADS, cqn_ref[...], scale, rope_h, q_ref, 0)
        _emit_heads(u_ref, rows, w_ref, C_HEADS * HEAD_DIM, C_KV_HEADS, ckn_ref[...], 1.0, rope_h, kv_ref, 2 * kv_d)
        _emit_heads(u_ref, rows, w_ref, C_HEADS * HEAD_DIM + kv_c, C_KV_HEADS, None, 1.0, None, kv_ref, 2 * kv_d + kv_c)
        _emit_heads(u_ref, rows, w_ref, C_IN, D_HEADS, dqn_ref[...], scale, None, q_ref, C_HEADS * HEAD_DIM)
        _emit_heads(u_ref, rows, w_ref, C_IN + kv_d, D_HEADS, dkn_ref[...], 1.0, None, kv_ref, 0)
        _emit_heads(u_ref, rows, w_ref, C_IN + 2 * kv_d, D_HEADS, None, 1.0, None, kv_ref, kv_d)


def _inproj_cd(h, g_pre, mod, w, cqn, ckn, dqn, dkn, rope_tabs, rows_per_batch, bm):
    rows, d = h.shape
    tiles_per_batch = rows_per_batch // bm
    nb = mod.shape[0]

    def row_map(i):
        return (i, 0)

    def mod_map(i):
        return (i // tiles_per_batch if nb > 1 else 0, 0, 0)

    tabs, tab_specs = _rope_specs(rope_tabs, 2, bm, tiles_per_batch)
    outs = [(C_HEADS + D_HEADS) * HEAD_DIM, 2 * (C_KV_HEADS + D_HEADS) * HEAD_DIM]
    return pl.pallas_call(
        functools.partial(_inproj_cd_kernel, rope_tabs is not None),
        grid=(rows // bm,),
        in_specs=[
            pl.BlockSpec((bm, d), row_map),
            _resident((1, d)),
            pl.BlockSpec((None, 8, d), mod_map),
            _resident(w.shape),
            _resident((1, HEAD_DIM)), _resident((1, HEAD_DIM)), _resident((1, HEAD_DIM)), _resident((1, HEAD_DIM)),
        ] + tab_specs,
        out_specs=[pl.BlockSpec((bm, n), row_map) for n in outs],
        out_shape=[jax.ShapeDtypeStruct((rows, n), BF16) for n in outs],
        scratch_shapes=[pltpu.VMEM((bm, d), BF16)],
        compiler_params=_cparams(("parallel",)),
        name="inproj_cd",
    )(h, g_pre, mod, w, cqn, ckn, dqn, dkn, *tabs)


def _flash_step(q_ref, kt, vt, m_ref, acc_ref, rep, dq, sub):
    bq = q_ref.shape[0]
    n_chunks = kt.shape[0] // LANES
    for r in range(rep):
        for sb in range(bq // sub):
            rows = slice(sb * sub, (sb + 1) * sub)
            mcol = slice(r * LANES, (r + 1) * LANES)
            acol = slice(r * 2 * LANES, (r + 1) * 2 * LANES)
            s = _dot_nt(q_ref[rows, r * dq:(r + 1) * dq], kt)
            chunks = [s[:, c * LANES:(c + 1) * LANES] for c in range(n_chunks)]
            lane_max = functools.reduce(jnp.maximum, chunks)
            m_prev = m_ref[rows, mcol]
            m_new = jnp.maximum(m_prev, jnp.max(lane_max, axis=-1, keepdims=True))
            alpha = jnp.exp2(m_prev - m_new)
            p = jnp.concatenate([jnp.exp2(ch - m_new) for ch in chunks], axis=1).astype(BF16)
            acc_ref[rows, acol] = jnp.concatenate([alpha, alpha], axis=1) * acc_ref[rows, acol] + _dot(p, vt)
            m_ref[rows, mcol] = m_new


def _flash_kernel(rep, dq, has_ctx, sub, *refs):
    if has_ctx:
        q_ref, k_ref, v_ref, kc_ref, vc_ref, o_ref, m_ref, acc_ref = refs
    else:
        q_ref, k_ref, v_ref, o_ref, m_ref, acc_ref = refs
    j = pl.program_id(3)

    def with_ones(v):
        return jnp.concatenate([v, jnp.ones(v.shape, BF16)], axis=1)

    @pl.when(j == 0)
    def _init():
        m_ref[...] = jnp.full(m_ref.shape, NEG, F32)
        acc_ref[...] = jnp.zeros(acc_ref.shape, F32)
        if has_ctx:
            _flash_step(q_ref, kc_ref[...], with_ones(vc_ref[...]), m_ref, acc_ref, rep, dq, sub)

    _flash_step(q_ref, k_ref[...], with_ones(v_ref[...]), m_ref, acc_ref, rep, dq, sub)

    @pl.when(j == pl.num_programs(3) - 1)
    def _fin():
        for r in range(rep):
            c0 = r * 2 * LANES
            o_ref[:, r * HEAD_DIM:(r + 1) * HEAD_DIM] = (
                acc_ref[:, c0:c0 + LANES] / acc_ref[:, c0 + LANES:c0 + 2 * LANES]).astype(BF16)


def _flash(q, k, v, kc, vc, batch, n_groups, rep, dq, bq, bk, sub):
    lq = q.shape[0] // batch
    lk = k.shape[0] // batch
    nq, nk = lq // bq, lk // bk
    has_ctx = kc is not None
    dv = HEAD_DIM
    assert dv == LANES
    in_specs = [
        pl.BlockSpec((bq, rep * dq), lambda b, g, i, j: (b * nq + i, g)),
        pl.BlockSpec((bk, dq), lambda b, g, i, j: (b * nk + j, g)),
        pl.BlockSpec((bk, dv), lambda b, g, i, j: (b * nk + j, g)),
    ]
    args = [q, k, v]
    if has_ctx:
        lc = kc.shape[0] // batch
        in_specs += [
            pl.BlockSpec((lc, dq), lambda b, g, i, j: (b, g)),
            pl.BlockSpec((lc, dv), lambda b, g, i, j: (b, g)),
        ]
        args += [kc, vc]
    return pl.pallas_call(
        functools.partial(_flash_kernel, rep, dq, has_ctx, min(sub, bq)),
        grid=(batch, n_groups, nq, nk),
        in_specs=in_specs,
        out_specs=pl.BlockSpec((bq, rep * dv), lambda b, g, i, j: (b * nq + i, g)),
        out_shape=jax.ShapeDtypeStruct((batch * lq, n_groups * rep * dv), BF16),
        scratch_shapes=[
            pltpu.VMEM((bq, rep * LANES), F32),
            pltpu.VMEM((bq, rep * 2 * LANES), F32),
        ],
        compiler_params=_cparams(("parallel", "parallel", "parallel", "arbitrary")),
        name="flash_attention",
    )(*args)


def _band_kernel(n_q, n_kv, per_head_bias, has_sink, *refs):
    if has_sink:
        q_ref, kvp_ref, kvx_ref, kvn_ref, kvc_ref, bias_ref, sink_ref, o_ref = refs
    else:
        q_ref, kvp_ref, kvx_ref, kvn_ref, kvc_ref, bias_ref, o_ref = refs
    t = BAND_TILE
    kv_refs = (kvc_ref, kvp_ref, kvx_ref, kvn_ref)
    lc = kvc_ref.shape[0]
    for r in range(n_q):
        hcol = slice(r * HEAD_DIM, (r + 1) * HEAD_DIM)
        kv_head = r * n_kv // n_q
        kcol = slice(kv_head * HEAD_DIM, (kv_head + 1) * HEAD_DIM)
        vcol = slice((n_kv + kv_head) * HEAD_DIM, (n_kv + kv_head + 1) * HEAD_DIM)
        bias = bias_ref.at[r] if per_head_bias else bias_ref
        keys = jnp.concatenate([ref[:, kcol] for ref in kv_refs], axis=0)
        vals = jnp.concatenate([ref[:, vcol] for ref in kv_refs], axis=0)
        vals = jnp.concatenate([vals, jnp.ones(vals.shape, BF16)], axis=1)
        s = _dot_nt(q_ref[:, hcol], keys)
        chunks = [s[:, c * LANES:(c + 1) * LANES] for c in range(lc // LANES)]
        chunks += [s[:, lc + c * LANES:lc + (c + 1) * LANES] + bias[:, c * LANES:(c + 1) * LANES]
                   for c in range(bias.shape[-1] // LANES)]
        m = jnp.max(functools.reduce(jnp.maximum, chunks), axis=-1, keepdims=True)
        if has_sink:
            sink = sink_ref[r, 0:1, 0:1]
            m = jnp.maximum(m, sink)
        p = jnp.concatenate([jnp.exp2(ch - m) for ch in chunks], axis=1).astype(BF16)
        acc = _dot(p, vals)
        l = acc[:, HEAD_DIM:]
        if has_sink:
            l = l + jnp.exp2(sink - m)
        o_ref[:, hcol] = (acc[:, :HEAD_DIM] / l).astype(BF16)


def _band_attention(q, kv, kvc, bias, sink, batch, q_col0, kv_col0, n_kv, halo):
    t = BAND_TILE
    n_q = 8
    seq = q.shape[0] // batch
    nq = seq // t
    lc = kvc.shape[0] // batch
    qw = n_q * HEAD_DIM
    kw = 2 * n_kv * HEAD_DIM
    has_sink = sink is not None
    per_head_bias = bias.ndim == 4

    assert t % halo == 0
    per = t // halo
    n_halo = seq // halo

    def halo_map(first_block):
        def f(b, i):
            return (b * n_halo + jnp.clip(first_block(i), 0, n_halo - 1), kv_col0 // kw)
        return f

    def cls(i):
        return jnp.where(i == 0, 0, jnp.where(i == nq - 1, 2, 1))

    bw = t + 2 * halo
    if per_head_bias:
        bias_spec = pl.BlockSpec((n_q, None, t, bw), lambda b, i: (0, cls(i), 0, 0))
    else:
        bias_spec = pl.BlockSpec((None, t, bw), lambda b, i: (cls(i), 0, 0))
    in_specs = [pl.BlockSpec((t, qw), lambda b, i: (b * nq + i, q_col0 // qw))]
    in_specs += [
        pl.BlockSpec((halo, kw), halo_map(lambda i: i * per - 1)),
        pl.BlockSpec((t, kw), lambda b, i: (b * nq + i, kv_col0 // kw)),
        pl.BlockSpec((halo, kw), halo_map(lambda i: (i + 1) * per)),
    ]
    in_specs += [pl.BlockSpec((lc, kw), lambda b, i: (b, kv_col0 // kw)), bias_spec]
    args = [q, kv, kv, kv, kvc, bias]
    if has_sink:
        in_specs.append(pl.BlockSpec((n_q, 8, LANES), lambda b, i: (0, 0, 0)))
        args.append(sink)
    return pl.pallas_call(
        functools.partial(_band_kernel, n_q, n_kv, per_head_bias, has_sink),
        grid=(batch, nq),
        in_specs=in_specs,
        out_specs=pl.BlockSpec((t, qw), lambda b, i: (b * nq + i, 0)),
        out_shape=jax.ShapeDtypeStruct((batch * seq, qw), BF16),
        compiler_params=_cparams(("parallel", "parallel")),
        name="band_attention",
    )(*args)


def _outproj_kernel(n_in, *refs):
    a_refs = refs[:n_in]
    w_ref, g_ref, mod_ref, h_ref, o_ref, acc_ref = refs[n_in:]
    bm = h_ref.shape[0]
    rb = min(PROJ_ROW_BLOCK, bm)
    for r0 in range(0, bm, rb):
        k0 = 0
        acc = None
        for a_ref in a_refs:
            kk = a_ref.shape[1]
            part = _dot(a_ref[r0:r0 + rb, :], w_ref[k0:k0 + kk, :])
            acc = part if acc is None else acc + part
            k0 += kk
        acc_ref[r0:r0 + rb, :] = acc
        _post_residual_to(acc_ref, h_ref, g_ref, mod_ref, 2, o_ref, r0, rb, inline=True)


def _outproj(a_list, w, g_post, mod, h, rows_per_batch, bm):
    rows, d = h.shape
    tiles_per_batch = rows_per_batch // bm
    nb = mod.shape[0]

    def row_map(i):
        return (i, 0)

    def mod_map(i):
        return (i // tiles_per_batch if nb > 1 else 0, 0, 0)

    return pl.pallas_call(
        functools.partial(_outproj_kernel, len(a_list)),
        grid=(rows // bm,),
        in_specs=[pl.BlockSpec((bm, a.shape[1]), row_map) for a in a_list] + [
            _resident(w.shape),
            _resident((1, d)),
            pl.BlockSpec((None, 8, d), mod_map),
            pl.BlockSpec((bm, d), row_map),
        ],
        out_specs=pl.BlockSpec((bm, d), row_map),
        out_shape=jax.ShapeDtypeStruct((rows, d), F32),
        scratch_shapes=[pltpu.VMEM((bm, d), F32)],
        compiler_params=_cparams(("parallel",)),
        name="outproj",
    )(*a_list, w, g_post, mod, h)


def _ffn_kernel(h_ref, gpre_ref, gpost_ref, mod_ref, wg_ref, wu_ref, wd_ref, o_ref, u_ref):
    f = pl.program_id(1)
    last = pl.num_programs(1) - 1
    bm = h_ref.shape[0]
    rb = min(FFN_ROW_BLOCK, bm)

    def partial_out(rows):
        u = u_ref[rows, :]
        act = (_silu(_dot(u, wg_ref[...])) * _dot(u, wu_ref[...])).astype(BF16)
        return _dot(act, wd_ref[...])

    @pl.when(f == 0)
    def _first():
        for r0 in range(0, bm, rb):
            _norm_mod_to(h_ref, gpre_ref, mod_ref, 3, 4, u_ref, r0, rb, inline=True)
            o_ref[r0:r0 + rb, :] = partial_out(slice(r0, r0 + rb))

    @pl.when(jnp.logical_and(f > 0, f < last))
    def _mid():
        o_ref[...] += partial_out(slice(None))

    @pl.when(f == last)
    def _last():
        for r0 in range(0, bm, rb):
            o_ref[r0:r0 + rb, :] += partial_out(slice(r0, r0 + rb))
            _post_residual_to(o_ref, h_ref, gpost_ref, mod_ref, 5, o_ref, r0, rb, inline=True)


def _cast_kernel(x_ref, o_ref):
    o_ref[...] = x_ref[...].astype(BF16)


def _cast_chunks_kernel(x_ref, o_ref):
    cc = o_ref.shape[-1]
    for f in range(o_ref.shape[0]):
        o_ref[f] = x_ref[:, f * cc:(f + 1) * cc].astype(BF16)


def _to_bf16(w, br, col_chunk=None):
    depth, r, c = w.shape
    assert r % br == 0
    spec = pl.BlockSpec((None, br, c), lambda l, i: (l, i, 0))
    if col_chunk is None:
        kern, out_spec, out_shape = _cast_kernel, spec, w.shape
    else:
        assert c % col_chunk == 0
        n_chunks = c // col_chunk
        kern = _cast_chunks_kernel
        out_spec = pl.BlockSpec((None, n_chunks, br, col_chunk), lambda l, i: (l, 0, i, 0))
        out_shape = (depth, n_chunks, r, col_chunk)
    return pl.pallas_call(
        kern,
        grid=(depth, r // br),
        in_specs=[spec],
        out_specs=out_spec,
        out_shape=jax.ShapeDtypeStruct(out_shape, BF16),
        compiler_params=_cparams(("parallel", "parallel")),
        name="cast_bf16",
    )(w)


def _ffn(h, g_pre, g_post, mod, wg, wu, wd, layer, rows_per_batch, bm, bf):
    rows, d = h.shape
    dff = wd.shape[1]
    assert wg.shape[1:] == (dff // bf, d, bf) and wu.shape == wg.shape
    tiles_per_batch = rows_per_batch // bm
    nb = mod.shape[0]

    def row_map(i, f):
        return (i, 0)

    def mod_map(i, f):
        return (i // tiles_per_batch if nb > 1 else 0, 0, 0)

    assert rows % bm == 0 and dff % bf == 0 and dff // bf >= 2
    return pl.pallas_call(
        _ffn_kernel,
        grid=(rows // bm, dff // bf),
        in_specs=[
            pl.BlockSpec((bm, d), row_map),
            _resident((1, d)), _resident((1, d)),
            pl.BlockSpec((None, 8, d), mod_map),
            pl.BlockSpec((None, None, d, bf), lambda i, f: (layer, f, 0, 0)),
            pl.BlockSpec((None, None, d, bf), lambda i, f: (layer, f, 0, 0)),
            pl.BlockSpec((None, bf, d), lambda i, f: (layer, f, 0)),
        ],
        out_specs=pl.BlockSpec((bm, d), row_map),
        out_shape=jax.ShapeDtypeStruct((rows, d), F32),
        scratch_shapes=[pltpu.VMEM((bm, d), BF16)],
        compiler_params=_cparams(("parallel", "arbitrary")),
        name="ffn",
    )(h, g_pre, g_post, mod, wg, wu, wd)


def _rope_table(n_tok, dim):
    assert n_tok % GRID_W == 0
    n_rows = n_tok // GRID_W
    half = dim // 2
    inv_freq = ROPE_BASE ** (-jnp.arange(0, half, 2, dtype=F32) / half)
    ar = jnp.arange(n_rows, dtype=jnp.int32).astype(F32)[:, None] * inv_freq[None, :]
    ac = jnp.arange(GRID_W, dtype=jnp.int32).astype(F32)[:, None] * inv_freq[None, :]

    def table(fn):
        r = jnp.concatenate([fn(ar), fn(ar)], axis=-1)
        c = jnp.concatenate([fn(ac), fn(ac)], axis=-1)
        full = jnp.concatenate([jnp.broadcast_to(r[:, None, :], (n_rows, GRID_W, half)),
                                jnp.broadcast_to(c[None, :, :], (n_rows, GRID_W, half))], axis=-1)
        return full.reshape(n_tok, dim)

    return table(jnp.cos), table(jnp.sin)


def _rpb_cols_kernel(r_ref, e_ref, o_ref):
    x = r_ref[...] * LOG2E
    hi = x.astype(BF16)
    r1 = x - hi.astype(F32)
    mid = r1.astype(BF16)
    lo = (r1 - mid.astype(F32)).astype(BF16)
    e = e_ref[...]
    o_ref[...] = (_dot(hi, e) + _dot(mid, e)) + _dot(lo, e)


def _band_bias(rpb, seq):
    t, g = BAND_TILE, GRID_W
    rpt = t // g
    nq = seq // t
    rows = seq // g
    kh = min(NA_KH, rows)
    n_h, n_di, n_dj = rpb.shape
    tiles = (0, 1, nq - 1)
    neg = NEG * LOG2E

    qpos = np.array(tiles)[:, None, None] * t + np.arange(t)[None, :, None]
    kpos = np.array(tiles)[:, None, None] * t - C_WINDOW + np.arange(t + 2 * C_WINDOW)[None, None, :]
    win = (np.abs(kpos - qpos) <= C_WINDOW) & (kpos >= 0) & (kpos < seq)
    bias_c = jnp.asarray(np.where(win, 0.0, neg).astype(np.float32))

    qc = np.arange(g)[:, None]
    kc = np.arange(g)[None, :]
    cs = np.clip(qc - NA_KW // 2, 0, g - NA_KW)
    col_ok = (kc >= cs) & (kc < cs + NA_KW)
    onehot = np.zeros((LANES, g, g), np.float32)
    qq, kk = np.nonzero(col_ok)
    onehot[(kk - qq + NA_KW - 1), qq, kk] = 1.0
    assert n_h * n_di <= LANES and n_dj <= LANES
    rpb2d = jnp.pad(rpb.reshape(n_h * n_di, n_dj).astype(F32), ((0, LANES - n_h * n_di), (0, LANES - n_dj)))
    cols = pl.pallas_call(
        _rpb_cols_kernel,
        out_shape=jax.ShapeDtypeStruct((LANES, g * g), F32),
        name="rpb_columns",
    )(rpb2d, jnp.asarray(onehot.reshape(LANES, g * g), BF16))
    cols = cols[:n_h * n_di].reshape(n_h, n_di, g, g) + jnp.asarray(np.where(col_ok, 0.0, neg).astype(np.float32))
    masked = jnp.full((n_h, g, g), neg, F32)

    tile_rows = []
    for ti in tiles:
        for a in range(rpt):
            qr = ti * rpt + a
            rs = min(max(qr - kh // 2, 0), rows - kh)
            krs = [(ti - 1) * rpt + b for b in range(3 * rpt)]
            tile_rows.append(jnp.concatenate(
                [cols[:, kr - qr + NA_KH - 1] if rs <= kr < rs + kh else masked for kr in krs], axis=-1))
    bias_d = jnp.stack(tile_rows, axis=1).reshape(n_h, len(tiles), t, 3 * t)
    return bias_c, bias_d


def _pad_mod(m):
    nb = m.shape[0]
    return jnp.pad(m.reshape(nb, 6, D_MODEL), ((0, 0), (0, 2), (0, 0)))


def _row(v):
    return v.reshape(1, -1).astype(F32)


def _tiles(batch, seq, n_ctx):
    return dict(
        bm_lat=512,
        bm_ctx=min(512, batch * n_ctx),
        bm_ffn=1024 if seq % 1024 == 0 else 512,
        bf_ffn=512,
        bq_a=min(2048, seq),
        bq_b=min(8192, seq),
        bk=min(2048, seq),
        sub=256,
    )


def kernel(x, c, ctx, c_ctx, w_mod, b_mod, g_mix_pre, g_mix_post, g_ffn_pre, g_ffn_post, w_gate, w_up, w_down,
           ab_w_in, ab_w_out, a_q_norm, a_k_norm, b_q_norm, b_kv_norm, b_w_uq, b_w_ukv, cd_w_in, cd_w_out,
           c_q_norm, c_k_norm, c_sink, d_q_norm, d_k_norm, d_rpb):
    batch, seq, d = x.shape
    n_ctx = ctx.shape[1]
    depth = w_mod.shape[0]
    assert d == D_MODEL and seq % 512 == 0 and n_ctx % 256 == 0 and batch + 1 <= 8
    assert depth == 2

    cvec = jnp.zeros((8, d), F32).at[:batch].set(c).at[batch].set(c_ctx)
    mod_all = _modulation(cvec, w_mod, b_mod)

    cos_h, sin_h = _rope_table(seq, HEAD_DIM)
    cos_r, sin_r = _rope_table(seq, B_ROPE)
    pad_r = ((0, 0), (0, LANES - B_ROPE))
    tabs_ab = (cos_h, sin_h, jnp.pad(cos_r, pad_r), jnp.pad(sin_r, pad_r))
    tabs_cd = (cos_h, sin_h)

    tl = _tiles(batch, seq, n_ctx)
    bm_lat, bm_ctx, bm_ffn, bf_ffn = tl["bm_lat"], tl["bm_ctx"], tl["bm_ffn"], tl["bf_ffn"]
    bq_a, bq_b, bk, sub = tl["bq_a"], tl["bq_b"], tl["bk"], tl["sub"]

    h_lat = x.reshape(batch * seq, d)
    h_ctx = ctx.reshape(batch * n_ctx, d)
    wg, wu = _to_bf16(w_gate, 256, bf_ffn), _to_bf16(w_up, 256, bf_ffn)
    wd = _to_bf16(w_down, 512)

    for i in range(depth):
        last = i == depth - 1
        need_ctx = not last
        j = i // 2
        mod_lat = _pad_mod(mod_all[i, :batch])
        mod_ctx = _pad_mod(mod_all[i, batch:batch + 1])
        g1, g2, g3, g4 = _row(g_mix_pre[i]), _row(g_mix_post[i]), _row(g_ffn_pre[i]), _row(g_ffn_post[i])

        if i % 2 == 0:
            w_in = ab_w_in[j]
            kr_cols = jnp.pad(w_in[:, A_IN + B_Q_RANK + B_KV_RANK:], ((0, 0), (0, LANES - B_ROPE)))
            w_ab = jnp.concatenate([w_in[:, :A_IN + B_Q_RANK + B_KV_RANK], kr_cols], axis=1).astype(BF16)
            wuq = jnp.pad(b_w_uq[j].reshape(B_Q_RANK, B_HEADS, B_NOPE + B_ROPE),
                          ((0, 0), (0, 0), (0, B_QK_PAD - B_NOPE - B_ROPE))).reshape(B_Q_RANK, B_HEADS * B_QK_PAD)
            wukv = b_w_ukv[j].reshape(B_KV_RANK, B_HEADS, B_NOPE + B_V)
            wuk = wukv[:, :, :B_NOPE].reshape(B_KV_RANK, B_HEADS * B_NOPE)
            wuv = wukv[:, :, B_NOPE:].reshape(B_KV_RANK, B_HEADS * B_V)
            norms = (_row(a_q_norm[j]), _row(a_k_norm[j]), _row(b_q_norm[j]), _row(b_kv_norm[j]),
                     wuq.astype(BF16), wuk.astype(BF16), wuv.astype(BF16))
            qa, ka, va, qb, kb, vb = _inproj_ab(h_lat, g1, mod_lat, w_ab, *norms, tabs_ab, seq, bm_lat)
            qac, kac, vac, qbc, kbc, vbc = _inproj_ab(h_ctx, g1, mod_ctx, w_ab, *norms, None, batch * n_ctx, bm_ctx)
            rep_a = A_HEADS // A_KV_HEADS
            oa = _flash(qa, ka, va, kac, vac, batch, A_KV_HEADS, rep_a, HEAD_DIM, bq_a, bk, sub)
            ob = _flash(qb, kb, vb, kbc, vbc, batch, B_HEADS, 1, B_QK_PAD, bq_b, bk, sub)
            w_out = ab_w_out[j].astype(BF16)
            attn_lat = [oa, ob]
            if need_ctx:
                oac = _flash(qac, kac, vac, None, None, batch, A_KV_HEADS, rep_a, HEAD_DIM, n_ctx, n_ctx, sub)
                obc = _flash(qbc, kbc, vbc, None, None, batch, B_HEADS, 1, B_QK_PAD, n_ctx, n_ctx, sub)
                attn_ctx = [oac, obc]
        else:
            w_cd = cd_w_in[j].astype(BF16)
            norms = (_row(c_q_norm[j]), _row(c_k_norm[j]), _row(d_q_norm[j]), _row(d_k_norm[j]))
            q_l, kv_l = _inproj_cd(h_lat, g1, mod_lat, w_cd, *norms, tabs_cd, seq, bm_lat)
            _, kv_c = _inproj_cd(h_ctx, g1, mod_ctx, w_cd, *norms, None, batch * n_ctx, bm_ctx)
            bias_c, bias_d = _band_bias(d_rpb[j], seq)
            sink = jnp.broadcast_to((c_sink[j].astype(F32) * LOG2E)[:, None, None], (C_HEADS, 8, LANES))
            kv_d = 2 * D_HEADS * HEAD_DIM
            attn_lat = [
                _band_attention(q_l, kv_l, kv_c, bias_c, sink, batch, 0, kv_d, C_KV_HEADS, C_WINDOW),
                _band_attention(q_l, kv_l, kv_c, bias_d, None, batch, C_HEADS * HEAD_DIM, 0, D_HEADS, BAND_TILE),
            ]
            w_out = cd_w_out[j].astype(BF16)

        h_lat = _outproj(attn_lat, w_out, g2, mod_lat, h_lat, seq, bm_lat)
        h_lat = _ffn(h_lat, g3, g4, mod_lat, wg, wu, wd, i, seq, bm_ffn, bf_ffn)
        if need_ctx:
            h_ctx = _outproj(attn_ctx, w_out, g2, mod_ctx, h_ctx, batch * n_ctx, bm_ctx)
            h_ctx = _ffn(h_ctx, g3, g4, mod_ctx, wg, wu, wd, i, batch * n_ctx, bm_ctx, bf_ffn)
    return h_lat.reshape(batch, seq, d)
```

```python
import functools

import jax
import jax.numpy as jnp
import numpy as np
from jax import lax
from jax.experimental import pallas as pl
from jax.experimental.pallas import tpu as pltpu

F32 = jnp.float32
BF16 = jnp.bfloat16

D_MODEL = 2048
GRID_W = 64
HEAD_DIM = 128
ROPE_BASE = 10000.0
EPS = 1e-6
NEG = -1e30
LOG2E = 1.4426950408889634

A_HEADS, A_KV_HEADS = 8, 2
B_HEADS, B_Q_RANK, B_KV_RANK, B_NOPE, B_ROPE, B_V = 8, 512, 512, 128, 64, 128
C_HEADS, C_KV_HEADS, C_WINDOW = 8, 2, 128
D_HEADS, NA_KH, NA_KW = 8, 8, 16
A_IN = (A_HEADS + 2 * A_KV_HEADS) * HEAD_DIM
C_IN = (C_HEADS + 2 * C_KV_HEADS) * HEAD_DIM
D_FF = 5632

LANES = 128
MXU_COLS = 256
B_QK_PAD = 256
BAND_TILE = 256
VMEM_LIMIT = 56 * 1024 * 1024
ROW_CHUNK = 32
PROJ_ROW_BLOCK = 256
FFN_ROW_BLOCK = 256
CHUNK_UNROLL = 4


def _cparams(sem):
    return pltpu.CompilerParams(dimension_semantics=sem, vmem_limit_bytes=VMEM_LIMIT)


def _resident(shape):
    nd = len(shape)
    return pl.BlockSpec(shape, lambda *_: (0,) * nd, pipeline_mode=pl.Buffered(1))


def _silu(x):
    return x * (1.0 / (1.0 + jnp.exp(-x)))


def _rms(x, g):
    ms = jnp.mean(x * x, axis=-1, keepdims=True)
    return x * lax.rsqrt(ms + EPS) * g


def _dot(a, b):
    return jnp.dot(a, b, preferred_element_type=F32)


def _dot_nt(a, b):
    return lax.dot_general(a, b, (((1,), (1,)), ((), ())), preferred_element_type=F32)


def _mod_kernel(c_ref, w_ref, b_ref, o_ref):
    s = _silu(c_ref[...])
    s_hi = s.astype(BF16)
    s_lo = (s - s_hi.astype(F32)).astype(BF16)
    w = w_ref[...]
    w_hi = w.astype(BF16)
    w_lo = (w - w_hi.astype(F32)).astype(BF16)
    o_ref[...] = _dot(s_hi, w_hi) + (_dot(s_lo, w_hi) + _dot(s_hi, w_lo)) + b_ref[...]


def _modulation(cvec, w_mod, b_mod):
    depth, d, n = w_mod.shape
    bn = 1024
    return pl.pallas_call(
        _mod_kernel,
        grid=(depth, n // bn),
        in_specs=[
            pl.BlockSpec((8, d), lambda i, j: (0, 0)),
            pl.BlockSpec((None, d, bn), lambda i, j: (i, 0, j)),
            pl.BlockSpec((None, 1, bn), lambda i, j: (i, 0, j)),
        ],
        out_specs=pl.BlockSpec((None, 8, bn), lambda i, j: (i, 0, j)),
        out_shape=jax.ShapeDtypeStruct((depth, 8, n), F32),
        compiler_params=_cparams(("parallel", "parallel")),
        name="modulation",
    )(cvec, w_mod, b_mod.reshape(depth, 1, n))


def _chunked(fn, n_rows, row0, inline):
    chunk = min(ROW_CHUNK, n_rows)
    if inline:
        for c in range(n_rows // chunk):
            fn(row0 + c * chunk)
        return

    def body(c, carry):
        fn(pl.multiple_of(row0 + c * chunk, chunk))
        return carry

    lax.fori_loop(0, n_rows // chunk, body, 0, unroll=CHUNK_UNROLL)


def _norm_mod_to(h_ref, g_ref, mod_ref, shift_row, scale_row, u_ref, row0=0, n_rows=None, inline=False):
    n_rows = h_ref.shape[0] if n_rows is None else n_rows
    chunk = min(ROW_CHUNK, n_rows)
    gs = g_ref[...] * (1.0 + mod_ref[scale_row:scale_row + 1, :])
    sh = mod_ref[shift_row:shift_row + 1, :]

    def one(r):
        h = h_ref[pl.ds(r, chunk), :]
        inv = lax.rsqrt(jnp.mean(h * h, axis=-1, keepdims=True) + EPS)
        u_ref[pl.ds(r, chunk), :] = ((h * inv) * gs + sh).astype(BF16)

    _chunked(one, n_rows, row0, inline)


def _post_residual_to(acc_ref, h_ref, g_ref, mod_ref, gate_row, o_ref, row0=0, n_rows=None, inline=False):
    n_rows = h_ref.shape[0] if n_rows is None else n_rows
    chunk = min(ROW_CHUNK, n_rows)
    gg = g_ref[...] * mod_ref[gate_row:gate_row + 1, :]

    def one(r):
        a = acc_ref[pl.ds(r, chunk), :]
        inv = lax.rsqrt(jnp.mean(a * a, axis=-1, keepdims=True) + EPS)
        o_ref[pl.ds(r, chunk), :] = h_ref[pl.ds(r, chunk), :] + (a * inv) * gg

    _chunked(one, n_rows, row0, inline)


def _rope(x, cos, sin, quarter):
    lane = lax.broadcasted_iota(jnp.int32, x.shape, 1)
    first = (lane % (2 * quarter)) < quarter
    rot = jnp.where(first, -pltpu.roll(x, LANES - quarter, 1), pltpu.roll(x, quarter, 1))
    return x * cos + rot * sin


def _emit_heads(u_ref, rows, w_ref, col0, n_heads, gain, scale, rope, o_ref, ocol0):
    h = 0
    while h < n_heads:
        nh = min(MXU_COLS // HEAD_DIM, n_heads - h)
        c = col0 + h * HEAD_DIM
        p = _dot(u_ref[rows, :], w_ref[:, c:c + nh * HEAD_DIM])
        for t in range(nh):
            x = p[:, t * HEAD_DIM:(t + 1) * HEAD_DIM]
            if gain is not None:
                x = _rms(x, gain)
                if rope is not None:
                    x = _rope(x, rope[0], rope[1], HEAD_DIM // 4)
                if scale != 1.0:
                    x = x * scale
            oc = ocol0 + (h + t) * HEAD_DIM
            o_ref[rows, oc:oc + HEAD_DIM] = x.astype(BF16)
        h += nh


def _inproj_ab_kernel(use_rope, h_ref, g_ref, mod_ref, w_ref, aqn_ref, akn_ref, bqn_ref, bkvn_ref,
                      wuq_ref, wuk_ref, wuv_ref, cosh_ref, sinh_ref, cosr_ref, sinr_ref,
                      qa_ref, ka_ref, va_ref, qb_ref, kb_ref, vb_ref, u_ref, c_ref):
    bm = h_ref.shape[0]
    rb = min(PROJ_ROW_BLOCK, bm)
    a_scale = HEAD_DIM ** -0.5 * LOG2E
    b_scale = (B_NOPE + B_ROPE) ** -0.5 * LOG2E
    for r0 in range(0, bm, rb):
        rows = slice(r0, r0 + rb)
        _norm_mod_to(h_ref, g_ref, mod_ref, 0, 1, u_ref, r0, rb, inline=True)
        rope_h = (cosh_ref[rows, :], sinh_ref[rows, :]) if use_rope else None
        _emit_heads(u_ref, rows, w_ref, 0, A_HEADS, aqn_ref[...], a_scale, rope_h, qa_ref, 0)
        _emit_heads(u_ref, rows, w_ref, A_HEADS * HEAD_DIM, A_KV_HEADS, akn_ref[...], 1.0, rope_h, ka_ref, 0)
        _emit_heads(u_ref, rows, w_ref, (A_HEADS + A_KV_HEADS) * HEAD_DIM, A_KV_HEADS, None, 1.0, None, va_ref, 0)

        cq = _dot(u_ref[rows, :], w_ref[:, A_IN:A_IN + B_Q_RANK])
        c_ref[rows, :] = _rms(cq, bqn_ref[...]).astype(BF16)
        for hd in range(B_HEADS):
            q = _dot(c_ref[rows, :], wuq_ref[:, hd * B_QK_PAD:(hd + 1) * B_QK_PAD])
            q_nope = q[:, :B_NOPE]
            q_rope = q[:, B_NOPE:]
            if use_rope:
                q_rope = _rope(q_rope, cosr_ref[rows, :], sinr_ref[rows, :], B_ROPE // 4)
            qb_ref[rows, hd * B_QK_PAD:hd * B_QK_PAD + B_NOPE] = (q_nope * b_scale).astype(BF16)
            qb_ref[rows, hd * B_QK_PAD + B_NOPE:(hd + 1) * B_QK_PAD] = (q_rope * b_scale).astype(BF16)
        kr0 = A_IN + B_Q_RANK + B_KV_RANK
        kr = _dot(u_ref[rows, :], w_ref[:, kr0:kr0 + LANES])
        if use_rope:
            kr = _rope(kr, cosr_ref[rows, :], sinr_ref[rows, :], B_ROPE // 4)
        kr = kr.astype(BF16)
        ckv = _dot(u_ref[rows, :], w_ref[:, A_IN + B_Q_RANK:A_IN + B_Q_RANK + B_KV_RANK])
        c_ref[rows, :] = _rms(ckv, bkvn_ref[...]).astype(BF16)
        for pair in range(B_HEADS * B_NOPE // MXU_COLS):
            kn = _dot(c_ref[rows, :], wuk_ref[:, pair * MXU_COLS:(pair + 1) * MXU_COLS])
            for t in range(MXU_COLS // B_NOPE):
                hd = pair * (MXU_COLS // B_NOPE) + t
                kb_ref[rows, hd * B_QK_PAD:hd * B_QK_PAD + B_NOPE] = kn[:, t * B_NOPE:(t + 1) * B_NOPE].astype(BF16)
                kb_ref[rows, hd * B_QK_PAD + B_NOPE:(hd + 1) * B_QK_PAD] = kr
            vb_ref[rows, pair * MXU_COLS:(pair + 1) * MXU_COLS] = _dot(
                c_ref[rows, :], wuv_ref[:, pair * MXU_COLS:(pair + 1) * MXU_COLS]).astype(BF16)


def _rope_specs(rope_tabs, n_tabs, bm, tiles_per_batch):
    if rope_tabs is None:
        return [jnp.zeros((bm, LANES), F32)] * n_tabs, [pl.BlockSpec((bm, LANES), lambda i: (0, 0))] * n_tabs
    return list(rope_tabs), [pl.BlockSpec((bm, LANES), lambda i: (i % tiles_per_batch, 0))] * n_tabs


def _inproj_ab(h, g_pre, mod, w, aqn, akn, bqn, bkvn, wuq, wuk, wuv, rope_tabs, rows_per_batch, bm):
    rows, d = h.shape
    tiles_per_batch = rows_per_batch // bm
    nb = mod.shape[0]

    def row_map(i):
        return (i, 0)

    def mod_map(i):
        return (i // tiles_per_batch if nb > 1 else 0, 0, 0)

    tabs, tab_specs = _rope_specs(rope_tabs, 4, bm, tiles_per_batch)
    outs = [
        (A_HEADS * HEAD_DIM), (A_KV_HEADS * HEAD_DIM), (A_KV_HEADS * HEAD_DIM),
        (B_HEADS * B_QK_PAD), (B_HEADS * B_QK_PAD), (B_HEADS * B_V),
    ]
    return pl.pallas_call(
        functools.partial(_inproj_ab_kernel, rope_tabs is not None),
        grid=(rows // bm,),
        in_specs=[
            pl.BlockSpec((bm, d), row_map),
            _resident((1, d)),
            pl.BlockSpec((None, 8, d), mod_map),
            _resident(w.shape),
            _resident((1, HEAD_DIM)), _resident((1, HEAD_DIM)),
            _resident((1, B_Q_RANK)), _resident((1, B_KV_RANK)),
            _resident(wuq.shape), _resident(wuk.shape), _resident(wuv.shape),
        ] + tab_specs,
        out_specs=[pl.BlockSpec((bm, n), row_map) for n in outs],
        out_shape=[jax.ShapeDtypeStruct((rows, n), BF16) for n in outs],
        scratch_shapes=[pltpu.VMEM((bm, d), BF16), pltpu.VMEM((bm, B_Q_RANK), BF16)],
        compiler_params=_cparams(("parallel",)),
        name="inproj_ab",
    )(h, g_pre, mod, w, aqn, akn, bqn, bkvn, wuq, wuk, wuv, *tabs)


def _inproj_cd_kernel(use_rope, h_ref, g_ref, mod_ref, w_ref, cqn_ref, ckn_ref, dqn_ref, dkn_ref,
                      cosh_ref, sinh_ref, q_ref, kv_ref, u_ref):
    bm = h_ref.shape[0]
    rb = min(PROJ_ROW_BLOCK, bm)
    scale = HEAD_DIM ** -0.5 * LOG2E
    kv_c = C_KV_HEADS * HEAD_DIM
    kv_d = D_HEADS * HEAD_DIM
    for r0 in range(0, bm, rb):
        rows = slice(r0, r0 + rb)
        _norm_mod_to(h_ref, g_ref, mod_ref, 0, 1, u_ref, r0, rb, inline=True)
        rope_h = (cosh_ref[rows, :], sinh_ref[rows, :]) if use_rope else None
        _emit_heads(u_ref, rows, w_ref, 0, C_HEADS, cqn_ref[...], scale, rope_h, q_ref, 0)
        _emit_heads(u_ref, rows, w_ref, C_HEADS * HEAD_DIM, C_KV_HEADS, ckn_ref[...], 1.0, rope_h, kv_ref, 2 * kv_d)
        _emit_heads(u_ref, rows, w_ref, C_HEADS * HEAD_DIM + kv_c, C_KV_HEADS, None, 1.0, None, kv_ref, 2 * kv_d + kv_c)
        _emit_heads(u_ref, rows, w_ref, C_IN, D_HEADS, dqn_ref[...], scale, None, q_ref, C_HEADS * HEAD_DIM)
        _emit_heads(u_ref, rows, w_ref, C_IN + kv_d, D_HEADS, dkn_ref[...], 1.0, None, kv_ref, 0)
        _emit_heads(u_ref, rows, w_ref, C_IN + 2 * kv_d, D_HEADS, None, 1.0, None, kv_ref, kv_d)


def _inproj_cd(h, g_pre, mod, w, cqn, ckn, dqn, dkn, rope_tabs, rows_per_batch, bm):
    rows, d = h.shape
    tiles_per_batch = rows_per_batch // bm
    nb = mod.shape[0]

    def row_map(i):
        return (i, 0)

    def mod_map(i):
        return (i // tiles_per_batch if nb > 1 else 0, 0, 0)

    tabs, tab_specs = _rope_specs(rope_tabs, 2, bm, tiles_per_batch)
    outs = [(C_HEADS + D_HEADS) * HEAD_DIM, 2 * (C_KV_HEADS + D_HEADS) * HEAD_DIM]
    return pl.pallas_call(
        functools.partial(_inproj_cd_kernel, rope_tabs is not None),
        grid=(rows // bm,),
        in_specs=[
            pl.BlockSpec((bm, d), row_map),
            _resident((1, d)),
            pl.BlockSpec((None, 8, d), mod_map),
            _resident(w.shape),
            _resident((1, HEAD_DIM)), _resident((1, HEAD_DIM)), _resident((1, HEAD_DIM)), _resident((1, HEAD_DIM)),
        ] + tab_specs,
        out_specs=[pl.BlockSpec((bm, n), row_map) for n in outs],
        out_shape=[jax.ShapeDtypeStruct((rows, n), BF16) for n in outs],
        scratch_shapes=[pltpu.VMEM((bm, d), BF16)],
        compiler_params=_cparams(("parallel",)),
        name="inproj_cd",
    )(h, g_pre, mod, w, cqn, ckn, dqn, dkn, *tabs)


def _flash_step(q_ref, kt, vt, m_ref, acc_ref, rep, dq, sub):
    bq = q_ref.shape[0]
    n_chunks = kt.shape[0] // LANES
    for r in range(rep):
        for sb in range(bq // sub):
            rows = slice(sb * sub, (sb + 1) * sub)
            mcol = slice(r * LANES, (r + 1) * LANES)
            acol = slice(r * 2 * LANES, (r + 1) * 2 * LANES)
            s = _dot_nt(q_ref[rows, r * dq:(r + 1) * dq], kt)
            chunks = [s[:, c * LANES:(c + 1) * LANES] for c in range(n_chunks)]
            lane_max = functools.reduce(jnp.maximum, chunks)
            m_prev = m_ref[rows, mcol]
            m_new = jnp.maximum(m_prev, jnp.max(lane_max, axis=-1, keepdims=True))
            alpha = jnp.exp2(m_prev - m_new)
            p = jnp.concatenate([jnp.exp2(ch - m_new) for ch in chunks], axis=1).astype(BF16)
            acc_ref[rows, acol] = jnp.concatenate([alpha, alpha], axis=1) * acc_ref[rows, acol] + _dot(p, vt)
            m_ref[rows, mcol] = m_new


def _flash_kernel(rep, dq, has_ctx, sub, *refs):
    if has_ctx:
        q_ref, k_ref, v_ref, kc_ref, vc_ref, o_ref, m_ref, acc_ref = refs
    else:
        q_ref, k_ref, v_ref, o_ref, m_ref, acc_ref = refs
    j = pl.program_id(3)

    def with_ones(v):
        return jnp.concatenate([v, jnp.ones(v.shape, BF16)], axis=1)

    @pl.when(j == 0)
    def _init():
        m_ref[...] = jnp.full(m_ref.shape, NEG, F32)
        acc_ref[...] = jnp.zeros(acc_ref.shape, F32)
        if has_ctx:
            _flash_step(q_ref, kc_ref[...], with_ones(vc_ref[...]), m_ref, acc_ref, rep, dq, sub)

    _flash_step(q_ref, k_ref[...], with_ones(v_ref[...]), m_ref, acc_ref, rep, dq, sub)

    @pl.when(j == pl.num_programs(3) - 1)
    def _fin():
        for r in range(rep):
            c0 = r * 2 * LANES
            o_ref[:, r * HEAD_DIM:(r + 1) * HEAD_DIM] = (
                acc_ref[:, c0:c0 + LANES] / acc_ref[:, c0 + LANES:c0 + 2 * LANES]).astype(BF16)


def _flash(q, k, v, kc, vc, batch, n_groups, rep, dq, bq, bk, sub):
    lq = q.shape[0] // batch
    lk = k.shape[0] // batch
    nq, nk = lq // bq, lk // bk
    has_ctx = kc is not None
    dv = HEAD_DIM
    assert dv == LANES
    in_specs = [
        pl.BlockSpec((bq, rep * dq), lambda b, g, i, j: (b * nq + i, g)),
        pl.BlockSpec((bk, dq), lambda b, g, i, j: (b * nk + j, g)),
        pl.BlockSpec((bk, dv), lambda b, g, i, j: (b * nk + j, g)),
    ]
    args = [q, k, v]
    if has_ctx:
        lc = kc.shape[0] // batch
        in_specs += [
            pl.BlockSpec((lc, dq), lambda b, g, i, j: (b, g)),
            pl.BlockSpec((lc, dv), lambda b, g, i, j: (b, g)),
        ]
        args += [kc, vc]
    return pl.pallas_call(
        functools.partial(_flash_kernel, rep, dq, has_ctx, min(sub, bq)),
        grid=(batch, n_groups, nq, nk),
        in_specs=in_specs,
        out_specs=pl.BlockSpec((bq, rep * dv), lambda b, g, i, j: (b * nq + i, g)),
        out_shape=jax.ShapeDtypeStruct((batch * lq, n_groups * rep * dv), BF16),
        scratch_shapes=[
            pltpu.VMEM((bq, rep * LANES), F32),
            pltpu.VMEM((bq, rep * 2 * LANES), F32),
        ],
        compiler_params=_cparams(("parallel", "parallel", "parallel", "arbitrary")),
        name="flash_attention",
    )(*args)


def _band_kernel(n_q, n_kv, per_head_bias, has_sink, *refs):
    if has_sink:
        q_ref, kvp_ref, kvx_ref, kvn_ref, kvc_ref, bias_ref, sink_ref, o_ref = refs
    else:
        q_ref, kvp_ref, kvx_ref, kvn_ref, kvc_ref, bias_ref, o_ref = refs
    t = BAND_TILE
    kv_refs = (kvc_ref, kvp_ref, kvx_ref, kvn_ref)
    lc = kvc_ref.shape[0]
    for r in range(n_q):
        hcol = slice(r * HEAD_DIM, (r + 1) * HEAD_DIM)
        kv_head = r * n_kv // n_q
        kcol = slice(kv_head * HEAD_DIM, (kv_head + 1) * HEAD_DIM)
        vcol = slice((n_kv + kv_head) * HEAD_DIM, (n_kv + kv_head + 1) * HEAD_DIM)
        bias = bias_ref.at[r] if per_head_bias else bias_ref
        keys = jnp.concatenate([ref[:, kcol] for ref in kv_refs], axis=0)
        vals = jnp.concatenate([ref[:, vcol] for ref in kv_refs], axis=0)
        vals = jnp.concatenate([vals, jnp.ones(vals.shape, BF16)], axis=1)
        s = _dot_nt(q_ref[:, hcol], keys)
        chunks = [s[:, c * LANES:(c + 1) * LANES] for c in range(lc // LANES)]
        chunks += [s[:, lc + c * LANES:lc + (c + 1) * LANES] + bias[:, c * LANES:(c + 1) * LANES]
                   for c in range(bias.shape[-1] // LANES)]
        m = jnp.max(functools.reduce(jnp.maximum, chunks), axis=-1, keepdims=True)
        if has_sink:
            sink = sink_ref[r, 0:1, 0:1]
            m = jnp.maximum(m, sink)
        p = jnp.concatenate([jnp.exp2(ch - m) for ch in chunks], axis=1).astype(BF16)
        acc = _dot(p, vals)
        l = acc[:, HEAD_DIM:]
        if has_sink:
            l = l + jnp.exp2(sink - m)
        o_ref[:, hcol] = (acc[:, :HEAD_DIM] / l).astype(BF16)


def _band_attention(q, kv, kvc, bias, sink, batch, q_col0, kv_col0, n_kv, halo):
    t = BAND_TILE
    n_q = 8
    seq = q.shape[0] // batch
    nq = seq // t
    lc = kvc.shape[0] // batch
    qw = n_q * HEAD_DIM
    kw = 2 * n_kv * HEAD_DIM
    has_sink = sink is not None
    per_head_bias = bias.ndim == 4

    assert t % halo == 0
    per = t // halo
    n_halo = seq // halo

    def halo_map(first_block):
        def f(b, i):
            return (b * n_halo + jnp.clip(first_block(i), 0, n_halo - 1), kv_col0 // kw)
        return f

    def cls(i):
        return jnp.where(i == 0, 0, jnp.where(i == nq - 1, 2, 1))

    bw = t + 2 * halo
    if per_head_bias:
        bias_spec = pl.BlockSpec((n_q, None, t, bw), lambda b, i: (0, cls(i), 0, 0))
    else:
        bias_spec = pl.BlockSpec((None, t, bw), lambda b, i: (cls(i), 0, 0))
    in_specs = [pl.BlockSpec((t, qw), lambda b, i: (b * nq + i, q_col0 // qw))]
    in_specs += [
        pl.BlockSpec((halo, kw), halo_map(lambda i: i * per - 1)),
        pl.BlockSpec((t, kw), lambda b, i: (b * nq + i, kv_col0 // kw)),
        pl.BlockSpec((halo, kw), halo_map(lambda i: (i + 1) * per)),
    ]
    in_specs += [pl.BlockSpec((lc, kw), lambda b, i: (b, kv_col0 // kw)), bias_spec]
    args = [q, kv, kv, kv, kvc, bias]
    if has_sink:
        in_specs.append(pl.BlockSpec((n_q, 8, LANES), lambda b, i: (0, 0, 0)))
        args.append(sink)
    return pl.pallas_call(
        functools.partial(_band_kernel, n_q, n_kv, per_head_bias, has_sink),
        grid=(batch, nq),
        in_specs=in_specs,
        out_specs=pl.BlockSpec((t, qw), lambda b, i: (b * nq + i, 0)),
        out_shape=jax.ShapeDtypeStruct((batch * seq, qw), BF16),
        compiler_params=_cparams(("parallel", "parallel")),
        name="band_attention",
    )(*args)


def _outproj_kernel(n_in, *refs):
    a_refs = refs[:n_in]
    w_ref, g_ref, mod_ref, h_ref, o_ref = refs[n_in:]
    bm = h_ref.shape[0]
    rb = min(PROJ_ROW_BLOCK, bm)
    for r0 in range(0, bm, rb):
        k0 = 0
        acc = None
        for a_ref in a_refs:
            kk = a_ref.shape[1]
            part = _dot(a_ref[r0:r0 + rb, :], w_ref[k0:k0 + kk, :])
            acc = part if acc is None else acc + part
            k0 += kk
        o_ref[r0:r0 + rb, :] = acc
        _post_residual_to(o_ref, h_ref, g_ref, mod_ref, 2, o_ref, r0, rb, inline=True)


def _outproj(a_list, w, g_post, mod, h, rows_per_batch, bm):
    rows, d = h.shape
    tiles_per_batch = rows_per_batch // bm
    nb = mod.shape[0]

    def row_map(i):
        return (i, 0)

    def mod_map(i):
        return (i // tiles_per_batch if nb > 1 else 0, 0, 0)

    return pl.pallas_call(
        functools.partial(_outproj_kernel, len(a_list)),
        grid=(rows // bm,),
        in_specs=[pl.BlockSpec((bm, a.shape[1]), row_map) for a in a_list] + [
            _resident(w.shape),
            _resident((1, d)),
            pl.BlockSpec((None, 8, d), mod_map),
            pl.BlockSpec((bm, d), row_map),
        ],
        out_specs=pl.BlockSpec((bm, d), row_map),
        out_shape=jax.ShapeDtypeStruct((rows, d), F32),
        compiler_params=_cparams(("parallel",)),
        name="outproj",
    )(*a_list, w, g_post, mod, h)


def _ffn_kernel(h_ref, gpre_ref, gpost_ref, mod_ref, wg_ref, wu_ref, wd_ref, o_ref, u_ref):
    f = pl.program_id(1)
    last = pl.num_programs(1) - 1
    bm = h_ref.shape[0]
    rb = min(FFN_ROW_BLOCK, bm)

    def partial_out(rows):
        u = u_ref[rows, :]
        act = (_silu(_dot(u, wg_ref[...])) * _dot(u, wu_ref[...])).astype(BF16)
        return _dot(act, wd_ref[...])

    @pl.when(f == 0)
    def _first():
        for r0 in range(0, bm, rb):
            _norm_mod_to(h_ref, gpre_ref, mod_ref, 3, 4, u_ref, r0, rb, inline=True)
            o_ref[r0:r0 + rb, :] = partial_out(slice(r0, r0 + rb))

    @pl.when(jnp.logical_and(f > 0, f < last))
    def _mid():
        o_ref[...] += partial_out(slice(None))

    @pl.when(f == last)
    def _last():
        for r0 in range(0, bm, rb):
            o_ref[r0:r0 + rb, :] += partial_out(slice(r0, r0 + rb))
            _post_residual_to(o_ref, h_ref, gpost_ref, mod_ref, 5, o_ref, r0, rb, inline=True)


def _cast_kernel(x_ref, o_ref):
    o_ref[...] = x_ref[...].astype(BF16)


def _cast_chunks_kernel(x_ref, o_ref):
    cc = o_ref.shape[-1]
    for f in range(o_ref.shape[0]):
        o_ref[f] = x_ref[:, f * cc:(f + 1) * cc].astype(BF16)


def _to_bf16(w, br, col_chunk=None):
    depth, r, c = w.shape
    assert r % br == 0
    spec = pl.BlockSpec((None, br, c), lambda l, i: (l, i, 0))
    if col_chunk is None:
        kern, out_spec, out_shape = _cast_kernel, spec, w.shape
    else:
        assert c % col_chunk == 0
        n_chunks = c // col_chunk
        kern = _cast_chunks_kernel
        out_spec = pl.BlockSpec((None, n_chunks, br, col_chunk), lambda l, i: (l, 0, i, 0))
        out_shape = (depth, n_chunks, r, col_chunk)
    return pl.pallas_call(
        kern,
        grid=(depth, r // br),
        in_specs=[spec],
        out_specs=out_spec,
        out_shape=jax.ShapeDtypeStruct(out_shape, BF16),
        compiler_params=_cparams(("parallel", "parallel")),
        name="cast_bf16",
    )(w)


def _ffn(h, g_pre, g_post, mod, wg, wu, wd, layer, rows_per_batch, bm, bf):
    rows, d = h.shape
    dff = wd.shape[1]
    assert wg.shape[1:] == (dff // bf, d, bf) and wu.shape == wg.shape
    tiles_per_batch = rows_per_batch // bm
    nb = mod.shape[0]

    def row_map(i, f):
        return (i, 0)

    def mod_map(i, f):
        return (i // tiles_per_batch if nb > 1 else 0, 0, 0)

    assert rows % bm == 0 and dff % bf == 0 and dff // bf >= 2
    return pl.pallas_call(
        _ffn_kernel,
        grid=(rows // bm, dff // bf),
        in_specs=[
            pl.BlockSpec((bm, d), row_map),
            _resident((1, d)), _resident((1, d)),
            pl.BlockSpec((None, 8, d), mod_map),
            pl.BlockSpec((None, None, d, bf), lambda i, f: (layer, f, 0, 0)),
            pl.BlockSpec((None, None, d, bf), lambda i, f: (layer, f, 0, 0)),
            pl.BlockSpec((None, bf, d), lambda i, f: (layer, f, 0)),
        ],
        out_specs=pl.BlockSpec((bm, d), row_map),
        out_shape=jax.ShapeDtypeStruct((rows, d), F32),
        scratch_shapes=[pltpu.VMEM((bm, d), BF16)],
        compiler_params=_cparams(("parallel", "arbitrary")),
        name="ffn",
    )(h, g_pre, g_post, mod, wg, wu, wd)


def _rope_table(n_tok, dim):
    assert n_tok % GRID_W == 0
    n_rows = n_tok // GRID_W
    half = dim // 2
    inv_freq = ROPE_BASE ** (-jnp.arange(0, half, 2, dtype=F32) / half)
    ar = jnp.arange(n_rows, dtype=jnp.int32).astype(F32)[:, None] * inv_freq[None, :]
    ac = jnp.arange(GRID_W, dtype=jnp.int32).astype(F32)[:, None] * inv_freq[None, :]

    def table(fn):
        r = jnp.concatenate([fn(ar), fn(ar)], axis=-1)
        c = jnp.concatenate([fn(ac), fn(ac)], axis=-1)
        full = jnp.concatenate([jnp.broadcast_to(r[:, None, :], (n_rows, GRID_W, half)),
                                jnp.broadcast_to(c[None, :, :], (n_rows, GRID_W, half))], axis=-1)
        return full.reshape(n_tok, dim)

    return table(jnp.cos), table(jnp.sin)


def _rpb_cols_kernel(r_ref, e_ref, o_ref):
    x = r_ref[...] * LOG2E
    hi = x.astype(BF16)
    r1 = x - hi.astype(F32)
    mid = r1.astype(BF16)
    lo = (r1 - mid.astype(F32)).astype(BF16)
    e = e_ref[...]
    o_ref[...] = (_dot(hi, e) + _dot(mid, e)) + _dot(lo, e)


def _band_bias(rpb, seq):
    t, g = BAND_TILE, GRID_W
    rpt = t // g
    nq = seq // t
    rows = seq // g
    kh = min(NA_KH, rows)
    n_h, n_di, n_dj = rpb.shape
    tiles = (0, 1, nq - 1)
    neg = NEG * LOG2E

    qpos = np.array(tiles)[:, None, None] * t + np.arange(t)[None, :, None]
    kpos = np.array(tiles)[:, None, None] * t - C_WINDOW + np.arange(t + 2 * C_WINDOW)[None, None, :]
    win = (np.abs(kpos - qpos) <= C_WINDOW) & (kpos >= 0) & (kpos < seq)
    bias_c = jnp.asarray(np.where(win, 0.0, neg).astype(np.float32))

    qc = np.arange(g)[:, None]
    kc = np.arange(g)[None, :]
    cs = np.clip(qc - NA_KW // 2, 0, g - NA_KW)
    col_ok = (kc >= cs) & (kc < cs + NA_KW)
    onehot = np.zeros((LANES, g, g), np.float32)
    qq, kk = np.nonzero(col_ok)
    onehot[(kk - qq + NA_KW - 1), qq, kk] = 1.0
    assert n_h * n_di <= LANES and n_dj <= LANES
    rpb2d = jnp.pad(rpb.reshape(n_h * n_di, n_dj).astype(F32), ((0, LANES - n_h * n_di), (0, LANES - n_dj)))
    cols = pl.pallas_call(
        _rpb_cols_kernel,
        out_shape=jax.ShapeDtypeStruct((LANES, g * g), F32),
        name="rpb_columns",
    )(rpb2d, jnp.asarray(onehot.reshape(LANES, g * g), BF16))
    cols = cols[:n_h * n_di].reshape(n_h, n_di, g, g) + jnp.asarray(np.where(col_ok, 0.0, neg).astype(np.float32))
    masked = jnp.full((n_h, g, g), neg, F32)

    tile_rows = []
    for ti in tiles:
        for a in range(rpt):
            qr = ti * rpt + a
            rs = min(max(qr - kh // 2, 0), rows - kh)
            krs = [(ti - 1) * rpt + b for b in range(3 * rpt)]
            tile_rows.append(jnp.concatenate(
                [cols[:, kr - qr + NA_KH - 1] if rs <= kr < rs + kh else masked for kr in krs], axis=-1))
    bias_d = jnp.stack(tile_rows, axis=1).reshape(n_h, len(tiles), t, 3 * t)
    return bias_c, bias_d


def _pad_mod(m):
    nb = m.shape[0]
    return jnp.pad(m.reshape(nb, 6, D_MODEL), ((0, 0), (0, 2), (0, 0)))


def _row(v):
    return v.reshape(1, -1).astype(F32)


def _tiles(batch, seq, n_ctx):
    return dict(
        bm_lat=512,
        bm_out=1024 if seq % 1024 == 0 else 512,
        bm_ctx=min(512, batch * n_ctx),
        bm_ffn=1024 if seq % 1024 == 0 else 512,
        bf_ffn=512,
        bq_a=min(2048, seq),
        bq_b=min(8192, seq),
        bk=min(2048, seq),
        sub=256,
    )


def kernel(x, c, ctx, c_ctx, w_mod, b_mod, g_mix_pre, g_mix_post, g_ffn_pre, g_ffn_post, w_gate, w_up, w_down,
           ab_w_in, ab_w_out, a_q_norm, a_k_norm, b_q_norm, b_kv_norm, b_w_uq, b_w_ukv, cd_w_in, cd_w_out,
           c_q_norm, c_k_norm, c_sink, d_q_norm, d_k_norm, d_rpb):
    batch, seq, d = x.shape
    n_ctx = ctx.shape[1]
    depth = w_mod.shape[0]
    assert d == D_MODEL and seq % 512 == 0 and n_ctx % 256 == 0 and batch + 1 <= 8
    assert depth == 2

    cvec = jnp.zeros((8, d), F32).at[:batch].set(c).at[batch].set(c_ctx)
    mod_all = _modulation(cvec, w_mod, b_mod)

    cos_h, sin_h = _rope_table(seq, HEAD_DIM)
    cos_r, sin_r = _rope_table(seq, B_ROPE)
    pad_r = ((0, 0), (0, LANES - B_ROPE))
    tabs_ab = (cos_h, sin_h, jnp.pad(cos_r, pad_r), jnp.pad(sin_r, pad_r))
    tabs_cd = (cos_h, sin_h)

    tl = _tiles(batch, seq, n_ctx)
    bm_lat, bm_ctx, bm_ffn, bf_ffn = tl["bm_lat"], tl["bm_ctx"], tl["bm_ffn"], tl["bf_ffn"]
    bq_a, bq_b, bk, sub = tl["bq_a"], tl["bq_b"], tl["bk"], tl["sub"]

    h_lat = x.reshape(batch * seq, d)
    h_ctx = ctx.reshape(batch * n_ctx, d)
    wg, wu = _to_bf16(w_gate, 256, bf_ffn), _to_bf16(w_up, 256, bf_ffn)
    wd = _to_bf16(w_down, 512)

    for i in range(depth):
        last = i == depth - 1
        need_ctx = not last
        j = i // 2
        mod_lat = _pad_mod(mod_all[i, :batch])
        mod_ctx = _pad_mod(mod_all[i, batch:batch + 1])
        g1, g2, g3, g4 = _row(g_mix_pre[i]), _row(g_mix_post[i]), _row(g_ffn_pre[i]), _row(g_ffn_post[i])

        if i % 2 == 0:
            w_in = ab_w_in[j]
            kr_cols = jnp.pad(w_in[:, A_IN + B_Q_RANK + B_KV_RANK:], ((0, 0), (0, LANES - B_ROPE)))
            w_ab = jnp.concatenate([w_in[:, :A_IN + B_Q_RANK + B_KV_RANK], kr_cols], axis=1).astype(BF16)
            wuq = jnp.pad(b_w_uq[j].reshape(B_Q_RANK, B_HEADS, B_NOPE + B_ROPE),
                          ((0, 0), (0, 0), (0, B_QK_PAD - B_NOPE - B_ROPE))).reshape(B_Q_RANK, B_HEADS * B_QK_PAD)
            wukv = b_w_ukv[j].reshape(B_KV_RANK, B_HEADS, B_NOPE + B_V)
            wuk = wukv[:, :, :B_NOPE].reshape(B_KV_RANK, B_HEADS * B_NOPE)
            wuv = wukv[:, :, B_NOPE:].reshape(B_KV_RANK, B_HEADS * B_V)
            norms = (_row(a_q_norm[j]), _row(a_k_norm[j]), _row(b_q_norm[j]), _row(b_kv_norm[j]),
                     wuq.astype(BF16), wuk.astype(BF16), wuv.astype(BF16))
            qa, ka, va, qb, kb, vb = _inproj_ab(h_lat, g1, mod_lat, w_ab, *norms, tabs_ab, seq, bm_lat)
            qac, kac, vac, qbc, kbc, vbc = _inproj_ab(h_ctx, g1, mod_ctx, w_ab, *norms, None, batch * n_ctx, bm_ctx)
            rep_a = A_HEADS // A_KV_HEADS
            oa = _flash(qa, ka, va, kac, vac, batch, A_KV_HEADS, rep_a, HEAD_DIM, bq_a, bk, sub)
            ob = _flash(qb, kb, vb, kbc, vbc, batch, B_HEADS, 1, B_QK_PAD, bq_b, bk, sub)
            w_out = ab_w_out[j].astype(BF16)
            attn_lat = [oa, ob]
            if need_ctx:
                oac = _flash(qac, kac, vac, None, None, batch, A_KV_HEADS, rep_a, HEAD_DIM, n_ctx, n_ctx, sub)
                obc = _flash(qbc, kbc, vbc, None, None, batch, B_HEADS, 1, B_QK_PAD, n_ctx, n_ctx, sub)
                attn_ctx = [oac, obc]
        else:
            w_cd = cd_w_in[j].astype(BF16)
            norms = (_row(c_q_norm[j]), _row(c_k_norm[j]), _row(d_q_norm[j]), _row(d_k_norm[j]))
            q_l, kv_l = _inproj_cd(h_lat, g1, mod_lat, w_cd, *norms, tabs_cd, seq, bm_lat)
            _, kv_c = _inproj_cd(h_ctx, g1, mod_ctx, w_cd, *norms, None, batch * n_ctx, bm_ctx)
            bias_c, bias_d = _band_bias(d_rpb[j], seq)
            sink = jnp.broadcast_to((c_sink[j].astype(F32) * LOG2E)[:, None, None], (C_HEADS, 8, LANES))
            kv_d = 2 * D_HEADS * HEAD_DIM
            attn_lat = [
                _band_attention(q_l, kv_l, kv_c, bias_c, sink, batch, 0, kv_d, C_KV_HEADS, C_WINDOW),
                _band_attention(q_l, kv_l, kv_c, bias_d, None, batch, C_HEADS * HEAD_DIM, 0, D_HEADS, BAND_TILE),
            ]
            w_out = cd_w_out[j].astype(BF16)

        h_lat = _outproj(attn_lat, w_out, g2, mod_lat, h_lat, seq, tl["bm_out"])
        h_lat = _ffn(h_lat, g3, g4, mod_lat, wg, wu, wd, i, seq, bm_ffn, bf_ffn)
        if need_ctx:
            h_ctx = _outproj(attn_ctx, w_out, g2, mod_ctx, h_ctx, batch * n_ctx, bm_ctx)
            h_ctx = _ffn(h_ctx, g3, g4, mod_ctx, wg, wu, wd, i, batch * n_ctx, bm_ctx, bf_ffn)
    return h_lat.reshape(batch, seq, d)
```
